```python
import jax, jax.numpy as jnp
from jax import lax
import numpy as np

D_MODEL = 1024
BATCH = 16
SEQ = 2048
DEPTH = 2

GRID_W = 64
CTX_LEN = 256
N_MIXERS = 2
N_HEADS = 16
N_KV_HEADS = 4
HEAD_DIM = 64
GROUP = N_HEADS // N_KV_HEADS
Q_WIDTH = N_HEADS * HEAD_DIM
KV_WIDTH = N_KV_HEADS * HEAD_DIM
WINDOW = 128
BLOCK = 128
SPAN = BLOCK + 2 * WINDOW
ROPE_BASE = 10000.0
AXIS_DIM = HEAD_DIM // 2
FOURIER_GROUPS = 4
FOURIER_GROUP_CH = D_MODEL // FOURIER_GROUPS
D_FF = 2816
N_MOD = 9
LN_EPS = 1e-5
ALPHA = (2.0 * DEPTH) ** 0.25
BETA = (8.0 * DEPTH) ** -0.25
NEG_INF = -1e30
N_ATTN_LAYERS = (DEPTH + N_MIXERS - 1) // N_MIXERS
N_FOURIER_LAYERS = (DEPTH + N_MIXERS - 2) // N_MIXERS

kernel_name = "hybrid_window_sink_gqa_fnet_macaron_deepnorm"


def layer_norm(x, g, b):
    xf = x.astype(jnp.float32)
    mu = xf.mean(-1, keepdims=True)
    var = jnp.square(xf - mu).mean(-1, keepdims=True)
    return ((xf - mu) * lax.rsqrt(var + LN_EPS) * g + b).astype(x.dtype)


def modulate(x, mod, j):
    shift, scale = mod[:, 3 * j], mod[:, 3 * j + 1]
    return x * (1 + scale[:, None, :]) + shift[:, None, :]


def post_norm_update(x, y, mod, j, g, b):
    gate = mod[:, 3 * j + 2][:, None, :]
    return layer_norm(ALPHA * x + gate * y, g, b)


def swiglu(h, wi, wo):
    a, g = jnp.split(h @ wi, 2, axis=-1)
    return (a * jax.nn.silu(g)) @ wo


def half_ffn_sublayer(x, mod, j, wi, wo, g, b):
    y = 0.5 * swiglu(modulate(x, mod, j), wi, wo)
    return post_norm_update(x, y, mod, j, g, b)


def axial_rope_tables(n_tokens):
    rows = n_tokens // GRID_W
    row = jnp.repeat(jnp.arange(rows), GRID_W).astype(jnp.float32)
    col = jnp.tile(jnp.arange(GRID_W), rows).astype(jnp.float32)
    inv = ROPE_BASE ** (-jnp.arange(0, AXIS_DIM, 2, dtype=jnp.float32) / AXIS_DIM)
    ang = jnp.concatenate([row[:, None] * inv, col[:, None] * inv], axis=-1)
    return jnp.cos(ang), jnp.sin(ang)


def apply_axial_rope(x, cos, sin):
    B_, L_, H_, Dh = x.shape
    half = AXIS_DIM // 2
    xr = x.astype(jnp.float32).reshape(B_, L_, H_, 2, 2, half)
    x1, x2 = xr[..., 0, :], xr[..., 1, :]
    c = cos.reshape(L_, 1, 2, half)
    s = sin.reshape(L_, 1, 2, half)
    out = jnp.stack([x1 * c - x2 * s, x1 * s + x2 * c], axis=-2)
    return out.reshape(B_, L_, H_, Dh).astype(x.dtype)


def window_sink_attention(h, hc, w_qkv, b_qkv, w_o, sink, cos, sin, ctx_queries):
    B_, S_, _ = h.shape
    C_ = hc.shape[1]
    n_blocks = S_ // BLOCK
    scale = HEAD_DIM ** -0.5
    q, k, v = jnp.split(h @ w_qkv + b_qkv, [Q_WIDTH, Q_WIDTH + KV_WIDTH], axis=-1)
    q = apply_axial_rope(q.reshape(B_, S_, N_HEADS, HEAD_DIM) * scale, cos, sin)
    q = q.reshape(B_, S_, N_KV_HEADS, GROUP, HEAD_DIM)
    k = apply_axial_rope(k.reshape(B_, S_, N_KV_HEADS, HEAD_DIM), cos, sin)
    v = v.reshape(B_, S_, N_KV_HEADS, HEAD_DIM)
    kvc = hc @ w_qkv[:, Q_WIDTH:] + b_qkv[Q_WIDTH:]
    kc, vc = jnp.split(kvc, [KV_WIDTH], axis=-1)
    kc = kc.reshape(B_, C_, N_KV_HEADS, HEAD_DIM)
    vc = vc.reshape(B_, C_, N_KV_HEADS, HEAD_DIM)

    pad = ((0, 0), (WINDOW, WINDOW), (0, 0), (0, 0))
    k_pad = jnp.pad(k, pad)
    v_pad = jnp.pad(v, pad)
    sink_logit = sink.astype(jnp.float32).reshape(1, N_KV_HEADS, GROUP, 1, 1)
    rel = jnp.arange(BLOCK)[:, None] - jnp.arange(SPAN)[None, :] + WINDOW
    band = jnp.abs(rel) <= WINDOW

    def block(bi):
        start = bi * BLOCK
        q_b = lax.dynamic_slice_in_dim(q, start, BLOCK, axis=1)
        k_b = lax.dynamic_slice_in_dim(k_pad, start, SPAN, axis=1)
        v_b = lax.dynamic_slice_in_dim(v_pad, start, SPAN, axis=1)
        key_pos = start - WINDOW + jnp.arange(SPAN)
        valid = band & ((key_pos >= 0) & (key_pos < S_))[None, :]
        s_win = jnp.einsum('bqhgd,bkhd->bhgqk', q_b, k_b, preferred_element_type=jnp.float32)
        s_win = jnp.where(valid, s_win, NEG_INF)
        s_ctx = jnp.einsum('bqhgd,bkhd->bhgqk', q_b, kc, preferred_element_type=jnp.float32)
        sinks = jnp.broadcast_to(sink_logit, s_ctx.shape[:-1] + (1,))
        p = jax.nn.softmax(jnp.concatenate([s_win, s_ctx, sinks], axis=-1), axis=-1).astype(v.dtype)
        o = (jnp.einsum('bhgqk,bkhd->bqhgd', p[..., :SPAN], v_b)
             + jnp.einsum('bhgqk,bkhd->bqhgd', p[..., SPAN:SPAN + C_], vc))
        return o

    o = lax.map(block, jnp.arange(n_blocks))
    o = jnp.moveaxis(o, 0, 1).reshape(B_, S_, Q_WIDTH)
    y = o @ w_o
    if not ctx_queries:
        return y, None
    qc = (hc @ w_qkv[:, :Q_WIDTH] + b_qkv[:Q_WIDTH]) * scale
    qc = qc.reshape(B_, C_, N_KV_HEADS, GROUP, HEAD_DIM)
    sc = jnp.einsum('bqhgd,bkhd->bhgqk', qc, kc, preferred_element_type=jnp.float32)
    sinks_c = jnp.broadcast_to(sink_logit, sc.shape[:-1] + (1,))
    pc = jax.nn.softmax(jnp.concatenate([sc, sinks_c], axis=-1), axis=-1).astype(vc.dtype)
    oc = jnp.einsum('bhgqk,bkhd->bqhgd', pc[..., :C_], vc).reshape(B_, C_, Q_WIDTH)
    return y, oc @ w_o


def fourier_mix(h, w_o):
    B_, L_, D_ = h.shape
    hg = h.astype(jnp.float32).reshape(B_, L_, FOURIER_GROUPS, FOURIER_GROUP_CH)
    f = jnp.fft.fft2(hg, axes=(1, 3), norm="ortho").real
    return f.reshape(B_, L_, D_).astype(h.dtype) @ w_o


def setup_inputs(seed: int = 0) -> dict:
    key = jax.random.key(seed)
    ks = jax.random.split(key, 16)

    def nrm(k, shape, s):
        return jax.random.normal(k, shape, jnp.float32) * s

    qkv_w = Q_WIDTH + 2 * KV_WIDTH
    return {
        "x": nrm(ks[0], (BATCH, SEQ, D_MODEL), 1.0),
        "c": nrm(ks[1], (BATCH, D_MODEL), 1.0),
        "ctx": nrm(ks[2], (BATCH, CTX_LEN, D_MODEL), 1.0),
        "c_ctx": nrm(ks[3], (D_MODEL,), 1.0),
        "mod_w": nrm(ks[4], (DEPTH, D_MODEL, N_MOD * D_MODEL), D_MODEL ** -0.5),
        "mod_b": nrm(ks[5], (DEPTH, N_MOD * D_MODEL), 0.02),
        "ln_g": 1.0 + nrm(ks[6], (DEPTH, 3, D_MODEL), 0.02),
        "ln_b": nrm(ks[7], (DEPTH, 3, D_MODEL), 0.02),
        "ffn_wi": nrm(ks[8], (DEPTH, 2, D_MODEL, 2 * D_FF), D_MODEL ** -0.5),
        "ffn_wo": nrm(ks[9], (DEPTH, 2, D_FF, D_MODEL), BETA * D_FF ** -0.5),
        "attn_wqkv": nrm(ks[10], (N_ATTN_LAYERS, D_MODEL, qkv_w), D_MODEL ** -0.5),
        "attn_bqkv": nrm(ks[11], (N_ATTN_LAYERS, qkv_w), 0.02),
        "attn_wo": nrm(ks[12], (N_ATTN_LAYERS, Q_WIDTH, D_MODEL), BETA * Q_WIDTH ** -0.5),
        "attn_sink": nrm(ks[13], (N_ATTN_LAYERS, N_HEADS), 1.0),
        "fourier_wo": nrm(ks[14], (N_FOURIER_LAYERS, D_MODEL, D_MODEL), BETA * D_MODEL ** -0.5),
    }


def reference(x, c, ctx, c_ctx, mod_w, mod_b, ln_g, ln_b, ffn_wi, ffn_wo,
              attn_wqkv, attn_bqkv, attn_wo, attn_sink, fourier_wo):
    cos, sin = axial_rope_tables(x.shape[1])
    silu_c = jax.nn.silu(c)
    silu_cc = jax.nn.silu(c_ctx)
    ctx_s = ctx
    for i in range(DEPTH):
        kind = i % N_MIXERS
        ctx_out_live = any(j % N_MIXERS == 0 for j in range(i + 1, DEPTH))
        ctx_in_live = (kind == 0) or ctx_out_live
        g, b = ln_g[i], ln_b[i]
        mod_lat = (silu_c @ mod_w[i] + mod_b[i]).reshape(-1, N_MOD, D_MODEL)
        x = half_ffn_sublayer(x, mod_lat, 0, ffn_wi[i, 0], ffn_wo[i, 0], g[0], b[0])
        if ctx_in_live:
            mod_ctx = (silu_cc @ mod_w[i] + mod_b[i]).reshape(1, N_MOD, D_MODEL)
            ctx_s = half_ffn_sublayer(ctx_s, mod_ctx, 0, ffn_wi[i, 0], ffn_wo[i, 0], g[0], b[0])
        h = modulate(x, mod_lat, 1)
        if kind == 0:
            a = i // N_MIXERS
            y, yc = window_sink_attention(h, modulate(ctx_s, mod_ctx, 1), attn_wqkv[a], attn_bqkv[a],
                                          attn_wo[a], attn_sink[a], cos, sin, ctx_out_live)
        else:
            f = i // N_MIXERS
            y = fourier_mix(h, fourier_wo[f])
            yc = fourier_mix(modulate(ctx_s, mod_ctx, 1), fourier_wo[f]) if ctx_out_live else None
        x = post_norm_update(x, y, mod_lat, 1, g[1], b[1])
        x = half_ffn_sublayer(x, mod_lat, 2, ffn_wi[i, 1], ffn_wo[i, 1], g[2], b[2])
        if ctx_out_live:
            ctx_s = post_norm_update(ctx_s, yc, mod_ctx, 1, g[1], b[1])
            ctx_s = half_ffn_sublayer(ctx_s, mod_ctx, 2, ffn_wi[i, 1], ffn_wo[i, 1], g[2], b[2])
    return x
```

```python
import functools

import jax
import jax.numpy as jnp
import numpy as np
from jax import lax
from jax.experimental import pallas as pl
from jax.experimental.pallas import tpu as pltpu

D_MODEL = 1024
DEPTH = 2
GRID_W = 64
N_HEADS = 16
N_KV_HEADS = 4
HEAD_DIM = 64
GROUP = N_HEADS // N_KV_HEADS
Q_WIDTH = N_HEADS * HEAD_DIM
KV_WIDTH = N_KV_HEADS * HEAD_DIM
QKV_WIDTH = Q_WIDTH + 2 * KV_WIDTH
WINDOW = 128
BLOCK = 128
SPAN = BLOCK + 2 * WINDOW
ROPE_BASE = 10000.0
AXIS_DIM = HEAD_DIM // 2
FOURIER_GROUPS = 4
FOURIER_GROUP_CH = D_MODEL // FOURIER_GROUPS
D_FF = 2816
N_MOD = 9
LN_EPS = 1e-5
ALPHA = (2.0 * DEPTH) ** 0.25
NEG_INF = -1e30

LANES = 128
MOD_ROWS = 24
VMEM_LIMIT = 56 * 1024 * 1024
FFN_CHUNKS = (1024, 1024, 768)

BF16 = jnp.bfloat16
F32 = jnp.float32


def _params(n_axes):
    return pltpu.CompilerParams(dimension_semantics=("parallel",) * n_axes,
                                vmem_limit_bytes=VMEM_LIMIT)


def _resident(shape):
    return pl.BlockSpec(shape, lambda *_: (0,) * len(shape), pipeline_mode=pl.Buffered(1))


def _layer_norm(z, g, b):
    mu = jnp.mean(z, axis=-1, keepdims=True)
    d = z - mu
    var = jnp.mean(d * d, axis=-1, keepdims=True)
    return d * lax.rsqrt(var + LN_EPS) * g + b


def _modulate(x, mod_ref, j):
    shift = mod_ref[3 * j:3 * j + 1, :]
    scale = mod_ref[3 * j + 1:3 * j + 2, :]
    return x * (1.0 + scale) + shift


def _post_norm(x, y, mod_ref, j, g_ref, b_ref):
    gate = mod_ref[3 * j + 2:3 * j + 3, :]
    return _layer_norm(ALPHA * x + gate * y, g_ref[...], b_ref[...])


def _mod_kernel(cv_ref, w_ref, b_ref, o_ref):
    cv = cv_ref[...]
    s = (cv * jax.nn.sigmoid(cv)).astype(BF16)
    o_ref[...] = jnp.dot(s, w_ref[...].astype(BF16), preferred_element_type=F32) + b_ref[...]


def _mod_call(cv, mod_w, mod_b):
    tn = D_MODEL
    n_out = N_MOD * D_MODEL
    return pl.pallas_call(
        _mod_kernel,
        out_shape=jax.ShapeDtypeStruct((DEPTH, MOD_ROWS, n_out), F32),
        grid=(DEPTH, n_out // tn),
        in_specs=[
            pl.BlockSpec((MOD_ROWS, D_MODEL), lambda i, n: (0, 0)),
            pl.BlockSpec((None, D_MODEL, tn), lambda i, n: (i, 0, n)),
            pl.BlockSpec((None, 1, tn), lambda i, n: (i, 0, n)),
        ],
        out_specs=pl.BlockSpec((None, MOD_ROWS, tn), lambda i, n: (i, 0, n)),
        compiler_params=_params(2),
        name="mod",
    )(cv, mod_w, mod_b.reshape(DEPTH, 1, n_out))


def _ffn_kernel(j, x_ref, mod_ref, wi_ref, wo_ref, g_ref, b_ref, o_ref, u_ref):
    x = x_ref[...]
    h = _modulate(x, mod_ref, j).astype(BF16)
    start = 0
    for width in FFN_CHUNKS:
        a = jnp.dot(h, wi_ref[:, start:start + width], preferred_element_type=F32)
        g = jnp.dot(h, wi_ref[:, D_FF + start:D_FF + start + width], preferred_element_type=F32)
        u_ref[:, start:start + width] = (a * (g * jax.nn.sigmoid(g))).astype(BF16)
        start += width
    y = 0.5 * jnp.dot(u_ref[...], wo_ref[...], preferred_element_type=F32)
    o_ref[...] = _post_norm(x, y, mod_ref, j, g_ref, b_ref)


def _ffn_call(x, mod, j, wi, wo, g, b, tm):
    bsz, length, _ = x.shape
    return pl.pallas_call(
        functools.partial(_ffn_kernel, j),
        out_shape=jax.ShapeDtypeStruct(x.shape, F32),
        grid=(bsz, length // tm),
        in_specs=[
            pl.BlockSpec((None, tm, D_MODEL), lambda bi, t: (bi, t, 0)),
            pl.BlockSpec((None, N_MOD, D_MODEL), lambda bi, t: (bi, 0, 0)),
            _resident((D_MODEL, 2 * D_FF)),
            _resident((D_FF, D_MODEL)),
            _resident((1, D_MODEL)),
            _resident((1, D_MODEL)),
        ],
        out_specs=pl.BlockSpec((None, tm, D_MODEL), lambda bi, t: (bi, t, 0)),
        scratch_shapes=[pltpu.VMEM((tm, D_FF), BF16)],
        compiler_params=_params(2),
        name="ffn",
    )(x, mod, wi, wo, g.reshape(1, D_MODEL), b.reshape(1, D_MODEL))


def _rope(t, cos_ref, sa_ref, sb_ref):
    half = AXIS_DIM // 2
    outs = []
    for c in range(t.shape[1] // LANES):
        tc = t[:, c * LANES:(c + 1) * LANES]
        up = pltpu.roll(tc, LANES - half, 1)
        down = pltpu.roll(tc, half, 1)
        outs.append(tc * cos_ref[...] + up * sa_ref[...] + down * sb_ref[...])
    return jnp.concatenate(outs, axis=1)


def _qkv_kernel(x_ref, mod_ref, w_ref, b_ref, cos_ref, sa_ref, sb_ref, q_ref, k_ref, v_ref):
    h = _modulate(x_ref[...], mod_ref, 1).astype(BF16)
    qkv = jnp.dot(h, w_ref[...], preferred_element_type=F32) + b_ref[...]
    q = qkv[:, :Q_WIDTH] * (HEAD_DIM ** -0.5)
    q_ref[...] = _rope(q, cos_ref, sa_ref, sb_ref).astype(BF16)
    k_ref[...] = _rope(qkv[:, Q_WIDTH:Q_WIDTH + KV_WIDTH], cos_ref, sa_ref, sb_ref).astype(BF16)
    v_ref[...] = qkv[:, Q_WIDTH + KV_WIDTH:].astype(BF16)


def _qkv_call(x, mod, w, b, cos_t, sa_t, sb_t, tm):
    bsz, length, _ = x.shape
    tok = lambda width: pl.BlockSpec((None, tm, width), lambda bi, t: (bi, t, 0))
    table = pl.BlockSpec((tm, LANES), lambda bi, t: (t, 0))
    return pl.pallas_call(
        _qkv_kernel,
        out_shape=(jax.ShapeDtypeStruct((bsz, length, Q_WIDTH), BF16),
                   jax.ShapeDtypeStruct((bsz, length, KV_WIDTH), BF16),
                   jax.ShapeDtypeStruct((bsz, length, KV_WIDTH), BF16)),
        grid=(bsz, length // tm),
        in_specs=[
            tok(D_MODEL),
            pl.BlockSpec((None, N_MOD, D_MODEL), lambda bi, t: (bi, 0, 0)),
            _resident((D_MODEL, QKV_WIDTH)),
            _resident((1, QKV_WIDTH)),
            table, table, table,
        ],
        out_specs=(tok(Q_WIDTH), tok(KV_WIDTH), tok(KV_WIDTH)),
        compiler_params=_params(2),
        name="qkv",
    )(x, mod, w, b.reshape(1, QKV_WIDTH), cos_t, sa_t, sb_t)


def _kv_ctx_kernel(x_ref, mod_ref, w_ref, b_ref, k_ref, v_ref):
    h = _modulate(x_ref[...], mod_ref, 1).astype(BF16)
    kv = jnp.dot(h, w_ref[...], preferred_element_type=F32) + b_ref[...]
    k_ref[...] = kv[:, :KV_WIDTH].astype(BF16)
    v_ref[...] = kv[:, KV_WIDTH:].astype(BF16)


def _kv_ctx_call(x, mod, w, b, tm):
    bsz, length, _ = x.shape
    tok = lambda width: pl.BlockSpec((None, tm, width), lambda bi, t: (bi, t, 0))
    return pl.pallas_call(
        _kv_ctx_kernel,
        out_shape=(jax.ShapeDtypeStruct((bsz, length, KV_WIDTH), BF16),
                   jax.ShapeDtypeStruct((bsz, length, KV_WIDTH), BF16)),
        grid=(bsz, length // tm),
        in_specs=[
            tok(D_MODEL),
            pl.BlockSpec((None, N_MOD, D_MODEL), lambda bi, t: (bi, 0, 0)),
            _resident((D_MODEL, 2 * KV_WIDTH)),
            _resident((1, 2 * KV_WIDTH)),
        ],
        out_specs=(tok(KV_WIDTH), tok(KV_WIDTH)),
        compiler_params=_params(2),
        name="kv_ctx",
    )(x, mod, w, b.reshape(1, 2 * KV_WIDTH))


def _attn_kernel(n_blocks, sink_ref, q_ref, kp_ref, vp_ref, kc_ref, vc_ref, o_ref):
    qb = pl.program_id(1)
    start = pl.multiple_of(qb * BLOCK, BLOCK)
    k_all = jnp.concatenate([kp_ref[pl.ds(start, SPAN), :], kc_ref[...]], axis=0)
    v_all = jnp.concatenate([vp_ref[pl.ds(start, SPAN), :], vc_ref[...]], axis=0)
    n_keys = k_all.shape[0]

    row = lax.broadcasted_iota(jnp.int32, (BLOCK, n_keys), 0)
    col = lax.broadcasted_iota(jnp.int32, (BLOCK, n_keys), 1)
    key_pos = start - WINDOW + col
    in_band = (jnp.abs(row - col + WINDOW) <= WINDOW) & (key_pos >= 0) & (key_pos < n_blocks * BLOCK)
    valid = in_band | (col >= SPAN)

    q = q_ref[...]
    outs = []
    for h in range(N_KV_HEADS):
        k_h = k_all[:, h * HEAD_DIM:(h + 1) * HEAD_DIM]
        v_h = v_all[:, h * HEAD_DIM:(h + 1) * HEAD_DIM]
        for g in range(GROUP):
            hq = h * GROUP + g
            q_h = q[:, hq * HEAD_DIM:(hq + 1) * HEAD_DIM]
            s = lax.dot_general(q_h, k_h, (((1,), (1,)), ((), ())), preferred_element_type=F32)
            s = jnp.where(valid, s, NEG_INF)
            sink = sink_ref[hq]
            m = jnp.maximum(jnp.max(s, axis=-1, keepdims=True), sink)
            p = jnp.exp(s - m)
            denom = jnp.sum(p, axis=-1, keepdims=True) + jnp.exp(sink - m)
            o = jnp.dot(p.astype(BF16), v_h, preferred_element_type=F32)
            outs.append(o / denom)
    o_ref[...] = jnp.concatenate(outs, axis=1).astype(BF16)


def _attn_call(q, kp, vp, kc, vc, sink):
    bsz, length, _ = q.shape
    n_blocks = length // BLOCK
    n_ctx = kc.shape[1]
    per_batch = lambda rows: pl.BlockSpec((None, rows, KV_WIDTH), lambda bi, t, *_: (bi, 0, 0))
    return pl.pallas_call(
        functools.partial(_attn_kernel, n_blocks),
        out_shape=jax.ShapeDtypeStruct((bsz, length, Q_WIDTH), BF16),
        grid_spec=pltpu.PrefetchScalarGridSpec(
            num_scalar_prefetch=1,
            grid=(bsz, n_blocks),
            in_specs=[
                pl.BlockSpec((None, BLOCK, Q_WIDTH), lambda bi, t, *_: (bi, t, 0)),
                per_batch(length + 2 * WINDOW),
                per_batch(length + 2 * WINDOW),
                per_batch(n_ctx),
                per_batch(n_ctx),
            ],
            out_specs=pl.BlockSpec((None, BLOCK, Q_WIDTH), lambda bi, t, *_: (bi, t, 0)),
        ),
        compiler_params=_params(2),
        name="attn",
    )(sink, q, kp, vp, kc, vc)


def _proj_norm_kernel(o_ref, x_ref, mod_ref, w_ref, g_ref, b_ref, out_ref):
    y = jnp.dot(o_ref[...], w_ref[...], preferred_element_type=F32)
    out_ref[...] = _post_norm(x_ref[...], y, mod_ref, 1, g_ref, b_ref)


def _proj_norm_call(o, x, mod, w, g, b, tm):
    bsz, length, _ = x.shape
    tok = pl.BlockSpec((None, tm, D_MODEL), lambda bi, t: (bi, t, 0))
    return pl.pallas_call(
        _proj_norm_kernel,
        out_shape=jax.ShapeDtypeStruct(x.shape, F32),
        grid=(bsz, length // tm),
        in_specs=[
            tok, tok,
            pl.BlockSpec((None, N_MOD, D_MODEL), lambda bi, t: (bi, 0, 0)),
            _resident((D_MODEL, D_MODEL)),
            _resident((1, D_MODEL)),
            _resident((1, D_MODEL)),
        ],
        out_specs=tok,
        compiler_params=_params(2),
        name="proj_norm",
    )(o, x, mod, w, g.reshape(1, D_MODEL), b.reshape(1, D_MODEL))


def _dft_ch_kernel(x_ref, mod_ref, cc_ref, sc_ref, a_ref, b_ref):
    h = _modulate(x_ref[...], mod_ref, 1).astype(BF16)
    for grp in range(FOURIER_GROUPS):
        cols = slice(grp * FOURIER_GROUP_CH, (grp + 1) * FOURIER_GROUP_CH)
        a_ref[:, cols] = jnp.dot(h[:, cols], cc_ref[...], preferred_element_type=F32).astype(BF16)
        b_ref[:, cols] = jnp.dot(h[:, cols], sc_ref[...], preferred_element_type=F32).astype(BF16)


def _dft_ch_call(x, mod, cc, sc, tm):
    bsz, length, _ = x.shape
    ch = FOURIER_GROUP_CH
    tok = pl.BlockSpec((None, tm, D_MODEL), lambda bi, t: (bi, t, 0))
    out = jax.ShapeDtypeStruct((bsz, length, D_MODEL), BF16)
    return pl.pallas_call(
        _dft_ch_kernel,
        out_shape=(out, out),
        grid=(bsz, length // tm),
        in_specs=[
            tok,
            pl.BlockSpec((None, N_MOD, D_MODEL), lambda bi, t: (bi, 0, 0)),
            _resident((ch, ch)),
            _resident((ch, ch)),
        ],
        out_specs=(tok, tok),
        compiler_params=_params(2),
        name="dft_ch",
    )(x, mod, cc, sc)


def _dft_seq_kernel(norm, wc_ref, ws_ref, a_ref, b_ref, x_ref, mod_ref, wo_ref, g_ref, beta_ref, out_ref):
    f = (jnp.dot(wc_ref[...], a_ref[...], preferred_element_type=F32)
         + jnp.dot(ws_ref[...], b_ref[...], preferred_element_type=F32)) * norm
    y = jnp.dot(f.astype(BF16), wo_ref[...], preferred_element_type=F32)
    out_ref[...] = _post_norm(x_ref[...], y, mod_ref, 1, g_ref, beta_ref)


def _dft_seq_call(w_cos, w_nsin, a, b, x, mod, wo, g, beta, tm):
    bsz, length, _ = x.shape
    norm = float(1.0 / np.sqrt(length * FOURIER_GROUP_CH))
    tok = pl.BlockSpec((None, tm, D_MODEL), lambda bi, t: (bi, t, 0))
    twiddle = pl.BlockSpec((tm, length), lambda bi, t: (t, 0))
    per_batch = pl.BlockSpec((None, length, D_MODEL), lambda bi, t: (bi, 0, 0))
    return pl.pallas_call(
        functools.partial(_dft_seq_kernel, norm),
        out_shape=jax.ShapeDtypeStruct(x.shape, F32),
        grid=(bsz, length // tm),
        in_specs=[
            twiddle, twiddle, per_batch, per_batch,
            tok,
            pl.BlockSpec((None, N_MOD, D_MODEL), lambda bi, t: (bi, 0, 0)),
            _resident((D_MODEL, D_MODEL)),
            _resident((1, D_MODEL)),
            _resident((1, D_MODEL)),
        ],
        out_specs=tok,
        compiler_params=_params(2),
        name="dft_seq",
    )(w_cos, w_nsin, a, b, x, mod, wo, g.reshape(1, D_MODEL), beta.reshape(1, D_MODEL))


def _rope_tables(n_tokens):
    rows = n_tokens // GRID_W
    row = jnp.repeat(jnp.arange(rows), GRID_W).astype(F32)
    col = jnp.tile(jnp.arange(GRID_W), rows).astype(F32)
    inv = ROPE_BASE ** (-jnp.arange(0, AXIS_DIM, 2, dtype=F32) / AXIS_DIM)
    ang_r, ang_c = row[:, None] * inv, col[:, None] * inv
    cos_r, sin_r, cos_c, sin_c = jnp.cos(ang_r), jnp.sin(ang_r), jnp.cos(ang_c), jnp.sin(ang_c)
    zero = jnp.zeros_like(sin_r)
    cos_h = jnp.concatenate([cos_r, cos_r, cos_c, cos_c], axis=-1)
    sa_h = jnp.concatenate([-sin_r, zero, -sin_c, zero], axis=-1)
    sb_h = jnp.concatenate([zero, sin_r, zero, sin_c], axis=-1)
    rep = LANES // HEAD_DIM
    return tuple(jnp.tile(t, (1, rep)) for t in (cos_h, sa_h, sb_h))


def _dft_tables(n):
    idx = jnp.arange(n, dtype=jnp.int32)
    ang = ((idx[:, None] * idx[None, :]) % n).astype(F32) * (2.0 * np.pi / n)
    return jnp.cos(ang), jnp.sin(ang)


def kernel(x, c, ctx, c_ctx, mod_w, mod_b, ln_g, ln_b, ffn_wi, ffn_wo,
           attn_wqkv, attn_bqkv, attn_wo, attn_sink, fourier_wo):
    bsz, seq, _ = x.shape
    n_ctx = ctx.shape[1]
    assert DEPTH == 2 and seq % BLOCK == 0 and seq % GRID_W == 0

    cv = jnp.concatenate([c, c_ctx[None, :], jnp.zeros((MOD_ROWS - bsz - 1, D_MODEL), F32)], axis=0)
    mod = _mod_call(cv, mod_w, mod_b).reshape(DEPTH, MOD_ROWS, N_MOD, D_MODEL)
    wi = ffn_wi.astype(BF16)
    wo = ffn_wo.astype(BF16)

    mod_lat, mod_ctx = mod[0, :bsz], mod[0, bsz:bsz + 1]
    g, b = ln_g[0], ln_b[0]
    x = _ffn_call(x, mod_lat, 0, wi[0, 0], wo[0, 0], g[0], b[0], tm=512)
    ctx_s = _ffn_call(ctx.reshape(1, bsz * n_ctx, D_MODEL), mod_ctx, 0, wi[0, 0], wo[0, 0], g[0], b[0], tm=512)

    w_qkv = attn_wqkv[0].astype(BF16)
    cos_t, sa_t, sb_t = _rope_tables(seq)
    q, k, v = _qkv_call(x, mod_lat, w_qkv, attn_bqkv[0], cos_t, sa_t, sb_t, tm=512)
    kc, vc = _kv_ctx_call(ctx_s, mod_ctx, w_qkv[:, Q_WIDTH:], attn_bqkv[0, Q_WIDTH:], tm=512)
    kc = kc.reshape(bsz, n_ctx, KV_WIDTH)
    vc = vc.reshape(bsz, n_ctx, KV_WIDTH)
    pad = ((0, 0), (WINDOW, WINDOW), (0, 0))
    o = _attn_call(q, jnp.pad(k, pad), jnp.pad(v, pad), kc, vc, attn_sink[0])
    x = _proj_norm_call(o, x, mod_lat, attn_wo[0].astype(BF16), g[1], b[1], tm=512)
    x = _ffn_call(x, mod_lat, 2, wi[0, 1], wo[0, 1], g[2], b[2], tm=512)

    mod_lat = mod[1, :bsz]
    g, b = ln_g[1], ln_b[1]
    x = _ffn_call(x, mod_lat, 0, wi[1, 0], wo[1, 0], g[0], b[0], tm=512)
    cos_c, sin_c = _dft_tables(FOURIER_GROUP_CH)
    cos_s, sin_s = _dft_tables(seq)
    fa, fb = _dft_ch_call(x, mod_lat, cos_c.astype(BF16), sin_c.astype(BF16), tm=512)
    x = _dft_seq_call(cos_s.astype(BF16), (-sin_s).astype(BF16), fa, fb, x, mod_lat,
                      fourier_wo[0].astype(BF16), g[1], b[1], tm=512)
    x = _ffn_call(x, mod_lat, 2, wi[1, 1], wo[1, 1], g[2], b[2], tm=512)
    return x
```

```python
import functools

import jax
import jax.numpy as jnp
import numpy as np
from jax import lax
from jax.experimental import pallas as pl
from jax.experimental.pallas import tpu as pltpu

D_MODEL = 1024
DEPTH = 2
GRID_W = 64
N_HEADS = 16
N_KV_HEADS = 4
HEAD_DIM = 64
GROUP = N_HEADS // N_KV_HEADS
Q_WIDTH = N_HEADS * HEAD_DIM
KV_WIDTH = N_KV_HEADS * HEAD_DIM
QKV_WIDTH = Q_WIDTH + 2 * KV_WIDTH
KV_DUP_WIDTH = 2 * KV_WIDTH
WINDOW = 128
BLOCK = 128
SPAN = BLOCK + 2 * WINDOW
ROPE_BASE = 10000.0
AXIS_DIM = HEAD_DIM // 2
FOURIER_GROUPS = 4
FOURIER_GROUP_CH = D_MODEL // FOURIER_GROUPS
D_FF = 2816
N_MOD = 9
LN_EPS = 1e-5
ALPHA = (2.0 * DEPTH) ** 0.25
NEG_INF = -1e30

LANES = 128
MOD_ROWS = 24
VMEM_LIMIT = 56 * 1024 * 1024
FFN_CHUNKS = (1024, 1024, 768)

BF16 = jnp.bfloat16
F32 = jnp.float32


def _params(n_axes):
    return pltpu.CompilerParams(dimension_semantics=("parallel",) * n_axes,
                                vmem_limit_bytes=VMEM_LIMIT)


def _resident(shape):
    return pl.BlockSpec(shape, lambda *_: (0,) * len(shape), pipeline_mode=pl.Buffered(1))


def _layer_norm(z, g, b):
    mu = jnp.mean(z, axis=-1, keepdims=True)
    d = z - mu
    var = jnp.mean(d * d, axis=-1, keepdims=True)
    return d * lax.rsqrt(var + LN_EPS) * g + b


def _modulate(x, mod_ref, j):
    shift = mod_ref[3 * j:3 * j + 1, :]
    scale = mod_ref[3 * j + 1:3 * j + 2, :]
    return x * (1.0 + scale) + shift


def _post_norm(x, y, mod_ref, j, g_ref, b_ref):
    gate = mod_ref[3 * j + 2:3 * j + 3, :]
    return _layer_norm(ALPHA * x + gate * y, g_ref[...], b_ref[...])


def _mod_kernel(cv_ref, w_ref, b_ref, o_ref):
    cv = cv_ref[...]
    s = (cv * jax.nn.sigmoid(cv)).astype(BF16)
    o_ref[...] = jnp.dot(s, w_ref[...].astype(BF16), preferred_element_type=F32) + b_ref[...]


def _mod_call(cv, mod_w, mod_b):
    tn = D_MODEL
    n_out = N_MOD * D_MODEL
    return pl.pallas_call(
        _mod_kernel,
        out_shape=jax.ShapeDtypeStruct((DEPTH, MOD_ROWS, n_out), F32),
        grid=(DEPTH, n_out // tn),
        in_specs=[
            pl.BlockSpec((MOD_ROWS, D_MODEL), lambda i, n: (0, 0)),
            pl.BlockSpec((None, D_MODEL, tn), lambda i, n: (i, 0, n)),
            pl.BlockSpec((None, 1, tn), lambda i, n: (i, 0, n)),
        ],
        out_specs=pl.BlockSpec((None, MOD_ROWS, tn), lambda i, n: (i, 0, n)),
        compiler_params=_params(2),
        name="mod",
    )(cv, mod_w, mod_b.reshape(DEPTH, 1, n_out))


def _ffn_kernel(j, x_ref, mod_ref, wi_ref, wo_ref, g_ref, b_ref, o_ref, u_ref):
    x = x_ref[...]
    h = _modulate(x, mod_ref, j).astype(BF16)
    start = 0
    for width in FFN_CHUNKS:
        a = jnp.dot(h, wi_ref[:, start:start + width], preferred_element_type=F32)
        g = jnp.dot(h, wi_ref[:, D_FF + start:D_FF + start + width], preferred_element_type=F32)
        u_ref[:, start:start + width] = (a * (g * jax.nn.sigmoid(g))).astype(BF16)
        start += width
    y = 0.5 * jnp.dot(u_ref[...], wo_ref[...], preferred_element_type=F32)
    o_ref[...] = _post_norm(x, y, mod_ref, j, g_ref, b_ref)


def _ffn_call(x, mod, j, wi, wo, g, b, tm):
    bsz, length, _ = x.shape
    return pl.pallas_call(
        functools.partial(_ffn_kernel, j),
        out_shape=jax.ShapeDtypeStruct(x.shape, F32),
        grid=(bsz, length // tm),
        in_specs=[
            pl.BlockSpec((None, tm, D_MODEL), lambda bi, t: (bi, t, 0)),
            pl.BlockSpec((None, N_MOD, D_MODEL), lambda bi, t: (bi, 0, 0)),
            _resident((D_MODEL, 2 * D_FF)),
            _resident((D_FF, D_MODEL)),
            _resident((1, D_MODEL)),
            _resident((1, D_MODEL)),
        ],
        out_specs=pl.BlockSpec((None, tm, D_MODEL), lambda bi, t: (bi, t, 0)),
        scratch_shapes=[pltpu.VMEM((tm, D_FF), BF16)],
        compiler_params=_params(2),
        name="ffn",
    )(x, mod, wi, wo, g.reshape(1, D_MODEL), b.reshape(1, D_MODEL))


def _rope(t, cos_ref, sa_ref, sb_ref):
    half = AXIS_DIM // 2
    outs = []
    for c in range(t.shape[1] // LANES):
        tc = t[:, c * LANES:(c + 1) * LANES]
        up = pltpu.roll(tc, LANES - half, 1)
        down = pltpu.roll(tc, half, 1)
        outs.append(tc * cos_ref[...] + up * sa_ref[...] + down * sb_ref[...])
    return jnp.concatenate(outs, axis=1)


def _dup_heads(t):
    pieces = []
    for h in range(t.shape[1] // HEAD_DIM):
        head = t[:, h * HEAD_DIM:(h + 1) * HEAD_DIM]
        pieces += [head, head]
    return jnp.concatenate(pieces, axis=1)


def _qkv_kernel(x_ref, mod_ref, w_ref, b_ref, cos_ref, sa_ref, sb_ref, q_ref, k_ref, v_ref):
    h = _modulate(x_ref[...], mod_ref, 1).astype(BF16)
    qkv = jnp.dot(h, w_ref[...], preferred_element_type=F32) + b_ref[...]
    q = qkv[:, :Q_WIDTH] * (HEAD_DIM ** -0.5)
    q_ref[...] = _rope(q, cos_ref, sa_ref, sb_ref).astype(BF16)
    k = _rope(qkv[:, Q_WIDTH:Q_WIDTH + KV_WIDTH], cos_ref, sa_ref, sb_ref)
    k_ref[...] = _dup_heads(k).astype(BF16)
    v_ref[...] = _dup_heads(qkv[:, Q_WIDTH + KV_WIDTH:]).astype(BF16)


def _qkv_call(x, mod, w, b, cos_t, sa_t, sb_t, tm):
    bsz, length, _ = x.shape
    tok = lambda width: pl.BlockSpec((None, tm, width), lambda bi, t: (bi, t, 0))
    table = pl.BlockSpec((tm, LANES), lambda bi, t: (t, 0))
    return pl.pallas_call(
        _qkv_kernel,
        out_shape=(jax.ShapeDtypeStruct((bsz, length, Q_WIDTH), BF16),
                   jax.ShapeDtypeStruct((bsz, length, KV_DUP_WIDTH), BF16),
                   jax.ShapeDtypeStruct((bsz, length, KV_DUP_WIDTH), BF16)),
        grid=(bsz, length // tm),
        in_specs=[
            tok(D_MODEL),
            pl.BlockSpec((None, N_MOD, D_MODEL), lambda bi, t: (bi, 0, 0)),
            _resident((D_MODEL, QKV_WIDTH)),
            _resident((1, QKV_WIDTH)),
            table, table, table,
        ],
        out_specs=(tok(Q_WIDTH), tok(KV_DUP_WIDTH), tok(KV_DUP_WIDTH)),
        compiler_params=_params(2),
        name="qkv",
    )(x, mod, w, b.reshape(1, QKV_WIDTH), cos_t, sa_t, sb_t)


def _kv_ctx_kernel(x_ref, mod_ref, w_ref, b_ref, k_ref, v_ref):
    h = _modulate(x_ref[...], mod_ref, 1).astype(BF16)
    kv = jnp.dot(h, w_ref[...], preferred_element_type=F32) + b_ref[...]
    k_ref[...] = _dup_heads(kv[:, :KV_WIDTH]).astype(BF16)
    v_ref[...] = _dup_heads(kv[:, KV_WIDTH:]).astype(BF16)


def _kv_ctx_call(x, mod, w, b, tm):
    bsz, length, _ = x.shape
    tok = lambda width: pl.BlockSpec((None, tm, width), lambda bi, t: (bi, t, 0))
    return pl.pallas_call(
        _kv_ctx_kernel,
        out_shape=(jax.ShapeDtypeStruct((bsz, length, KV_DUP_WIDTH), BF16),
                   jax.ShapeDtypeStruct((bsz, length, KV_DUP_WIDTH), BF16)),
        grid=(bsz, length // tm),
        in_specs=[
            tok(D_MODEL),
            pl.BlockSpec((None, N_MOD, D_MODEL), lambda bi, t: (bi, 0, 0)),
            _resident((D_MODEL, 2 * KV_WIDTH)),
            _resident((1, 2 * KV_WIDTH)),
        ],
        out_specs=(tok(KV_DUP_WIDTH), tok(KV_DUP_WIDTH)),
        compiler_params=_params(2),
        name="kv_ctx",
    )(x, mod, w, b.reshape(1, 2 * KV_WIDTH))


def _attn_kernel(n_blocks, sink_ref, q_ref, kp_ref, vp_ref, kc_ref, vc_ref, o_ref):
    qb = pl.program_id(1)
    start = pl.multiple_of(qb * BLOCK, BLOCK)
    k_all = jnp.concatenate([kp_ref[pl.ds(start, SPAN), :], kc_ref[...]], axis=0)
    v_all = jnp.concatenate([vp_ref[pl.ds(start, SPAN), :], vc_ref[...]], axis=0)

    rows2 = 2 * BLOCK
    r = lax.broadcasted_iota(jnp.int32, (rows2, BLOCK), 0)
    cidx = lax.broadcasted_iota(jnp.int32, (rows2, BLOCK), 1)
    r_in = r & (BLOCK - 1)
    left_ok = (r_in <= cidx) & (qb > 0)
    right_ok = (cidx <= r_in) & (qb < n_blocks - 1)
    top = r[:, :1] < BLOCK
    lo_q = lax.broadcasted_iota(jnp.int32, (rows2, LANES), 1) < HEAD_DIM
    lo_kv = lax.broadcasted_iota(jnp.int32, (k_all.shape[0], LANES), 1) < HEAD_DIM

    for h in range(N_KV_HEADS):
        k_h = k_all[:, h * LANES:(h + 1) * LANES]
        v_h = v_all[:, h * LANES:(h + 1) * LANES]
        c0 = 2 * h * LANES
        q_cat = jnp.concatenate([q_ref[:, c0:c0 + LANES], q_ref[:, c0 + LANES:c0 + 2 * LANES]], axis=0)
        acc = None
        inv = []
        for half in range(2):
            keep = lo_q if half == 0 else jnp.logical_not(lo_q)
            q_sel = jnp.where(keep, q_cat, jnp.zeros_like(q_cat))
            s = lax.dot_general(q_sel, k_h, (((1,), (1,)), ((), ())), preferred_element_type=F32)
            sink = jnp.where(top, sink_ref[4 * h + half], sink_ref[4 * h + 2 + half])
            parts = [jnp.where(left_ok, s[:, :BLOCK], NEG_INF),
                     s[:, BLOCK:2 * BLOCK],
                     jnp.where(right_ok, s[:, 2 * BLOCK:SPAN], NEG_INF)]
            parts += [s[:, c:c + BLOCK] for c in range(SPAN, s.shape[1], BLOCK)]
            m = functools.reduce(jnp.maximum, parts)
            m = jnp.maximum(jnp.max(m, axis=-1, keepdims=True), sink)
            p = [jnp.exp(t - m) for t in parts]
            denom = jnp.sum(functools.reduce(jnp.add, p), axis=-1, keepdims=True) + jnp.exp(sink - m)
            inv.append(1.0 / denom)
            keep_v = lo_kv if half == 0 else jnp.logical_not(lo_kv)
            v_sel = jnp.where(keep_v, v_h, jnp.zeros_like(v_h))
            pv = jnp.dot(jnp.concatenate(p, axis=1).astype(BF16), v_sel, preferred_element_type=F32)
            acc = pv if acc is None else acc + pv
        out = (acc * jnp.where(lo_q, inv[0], inv[1])).astype(BF16)
        o_ref[:, c0:c0 + LANES] = out[:BLOCK]
        o_ref[:, c0 + LANES:c0 + 2 * LANES] = out[BLOCK:]


def _attn_call(q, kp, vp, kc, vc, sink):
    bsz, length, _ = q.shape
    n_blocks = length // BLOCK
    n_ctx = kc.shape[1]
    per_batch = lambda rows: pl.BlockSpec((None, rows, KV_DUP_WIDTH), lambda bi, t, *_: (bi, 0, 0))
    return pl.pallas_call(
        functools.partial(_attn_kernel, n_blocks),
        out_shape=jax.ShapeDtypeStruct((bsz, length, Q_WIDTH), BF16),
        grid_spec=pltpu.PrefetchScalarGridSpec(
            num_scalar_prefetch=1,
            grid=(bsz, n_blocks),
            in_specs=[
                pl.BlockSpec((None, BLOCK, Q_WIDTH), lambda bi, t, *_: (bi, t, 0)),
                per_batch(length + 2 * WINDOW),
                per_batch(length + 2 * WINDOW),
                per_batch(n_ctx),
                per_batch(n_ctx),
            ],
            out_specs=pl.BlockSpec((None, BLOCK, Q_WIDTH), lambda bi, t, *_: (bi, t, 0)),
        ),
        compiler_params=_params(2),
        name="attn",
    )(sink, q, kp, vp, kc, vc)


def _proj_norm_kernel(o_ref, x_ref, mod_ref, w_ref, g_ref, b_ref, out_ref):
    y = jnp.dot(o_ref[...], w_ref[...], preferred_element_type=F32)
    out_ref[...] = _post_norm(x_ref[...], y, mod_ref, 1, g_ref, b_ref)


def _proj_norm_call(o, x, mod, w, g, b, tm):
    bsz, length, _ = x.shape
    tok = pl.BlockSpec((None, tm, D_MODEL), lambda bi, t: (bi, t, 0))
    return pl.pallas_call(
        _proj_norm_kernel,
        out_shape=jax.ShapeDtypeStruct(x.shape, F32),
        grid=(bsz, length // tm),
        in_specs=[
            tok, tok,
            pl.BlockSpec((None, N_MOD, D_MODEL), lambda bi, t: (bi, 0, 0)),
            _resident((D_MODEL, D_MODEL)),
            _resident((1, D_MODEL)),
            _resident((1, D_MODEL)),
        ],
        out_specs=tok,
        compiler_params=_params(2),
        name="proj_norm",
    )(o, x, mod, w, g.reshape(1, D_MODEL), b.reshape(1, D_MODEL))


def _dft_ch_kernel(x_ref, mod_ref, cc_ref, sc_ref, a_ref, b_ref):
    h = _modulate(x_ref[...], mod_ref, 1).astype(BF16)
    for grp in range(FOURIER_GROUPS):
        cols = slice(grp * FOURIER_GROUP_CH, (grp + 1) * FOURIER_GROUP_CH)
        a_ref[:, cols] = jnp.dot(h[:, cols], cc_ref[...], preferred_element_type=F32).astype(BF16)
        b_ref[:, cols] = jnp.dot(h[:, cols], sc_ref[...], preferred_element_type=F32).astype(BF16)


def _dft_ch_call(x, mod, cc, sc, tm):
    bsz, length, _ = x.shape
    ch = FOURIER_GROUP_CH
    tok = pl.BlockSpec((None, tm, D_MODEL), lambda bi, t: (bi, t, 0))
    out = jax.ShapeDtypeStruct((bsz, length, D_MODEL), BF16)
    return pl.pallas_call(
        _dft_ch_kernel,
        out_shape=(out, out),
        grid=(bsz, length // tm),
        in_specs=[
            tok,
            pl.BlockSpec((None, N_MOD, D_MODEL), lambda bi, t: (bi, 0, 0)),
            _resident((ch, ch)),
            _resident((ch, ch)),
        ],
        out_specs=(tok, tok),
        compiler_params=_params(2),
        name="dft_ch",
    )(x, mod, cc, sc)


def _dft_seq_kernel(norm, wc_ref, ws_ref, a_ref, b_ref, x_ref, mod_ref, wo_ref, g_ref, beta_ref, out_ref):
    f = (jnp.dot(wc_ref[...], a_ref[...], preferred_element_type=F32)
         + jnp.dot(ws_ref[...], b_ref[...], preferred_element_type=F32)) * norm
    y = jnp.dot(f.astype(BF16), wo_ref[...], preferred_element_type=F32)
    out_ref[...] = _post_norm(x_ref[...], y, mod_ref, 1, g_ref, beta_ref)


def _dft_seq_call(w_cos, w_nsin, a, b, x, mod, wo, g, beta, tm):
    bsz, length, _ = x.shape
    norm = float(1.0 / np.sqrt(length * FOURIER_GROUP_CH))
    tok = pl.BlockSpec((None, tm, D_MODEL), lambda bi, t: (bi, t, 0))
    twiddle = pl.BlockSpec((tm, length), lambda bi, t: (t, 0))
    per_batch = pl.BlockSpec((None, length, D_MODEL), lambda bi, t: (bi, 0, 0))
    return pl.pallas_call(
        functools.partial(_dft_seq_kernel, norm),
        out_shape=jax.ShapeDtypeStruct(x.shape, F32),
        grid=(bsz, length // tm),
        in_specs=[
            twiddle, twiddle, per_batch, per_batch,
            tok,
            pl.BlockSpec((None, N_MOD, D_MODEL), lambda bi, t: (bi, 0, 0)),
            _resident((D_MODEL, D_MODEL)),
            _resident((1, D_MODEL)),
            _resident((1, D_MODEL)),
        ],
        out_specs=tok,
        compiler_params=_params(2),
        name="dft_seq",
    )(w_cos, w_nsin, a, b, x, mod, wo, g.reshape(1, D_MODEL), beta.reshape(1, D_MODEL))


def _rope_tables(n_tokens):
    rows = n_tokens // GRID_W
    row = jnp.repeat(jnp.arange(rows), GRID_W).astype(F32)
    col = jnp.tile(jnp.arange(GRID_W), rows).astype(F32)
    inv = ROPE_BASE ** (-jnp.arange(0, AXIS_DIM, 2, dtype=F32) / AXIS_DIM)
    ang_r, ang_c = row[:, None] * inv, col[:, None] * inv
    cos_r, sin_r, cos_c, sin_c = jnp.cos(ang_r), jnp.sin(ang_r), jnp.cos(ang_c), jnp.sin(ang_c)
    zero = jnp.zeros_like(sin_r)
    cos_h = jnp.concatenate([cos_r, cos_r, cos_c, cos_c], axis=-1)
    sa_h = jnp.concatenate([-sin_r, zero, -sin_c, zero], axis=-1)
    sb_h = jnp.concatenate([zero, sin_r, zero, sin_c], axis=-1)
    rep = LANES // HEAD_DIM
    return tuple(jnp.tile(t, (1, rep)) for t in (cos_h, sa_h, sb_h))


def _dft_tables(n):
    idx = jnp.arange(n, dtype=jnp.int32)
    ang = ((idx[:, None] * idx[None, :]) % n).astype(F32) * (2.0 * np.pi / n)
    return jnp.cos(ang), jnp.sin(ang)


def kernel(x, c, ctx, c_ctx, mod_w, mod_b, ln_g, ln_b, ffn_wi, ffn_wo,
           attn_wqkv, attn_bqkv, attn_wo, attn_sink, fourier_wo):
    bsz, seq, _ = x.shape
    n_ctx = ctx.shape[1]
    assert DEPTH == 2 and seq % BLOCK == 0 and seq % GRID_W == 0

    cv = jnp.concatenate([c, c_ctx[None, :], jnp.zeros((MOD_ROWS - bsz - 1, D_MODEL), F32)], axis=0)
    mod = _mod_call(cv, mod_w, mod_b).reshape(DEPTH, MOD_ROWS, N_MOD, D_MODEL)
    wi = ffn_wi.astype(BF16)
    wo = ffn_wo.astype(BF16)

    mod_lat, mod_ctx = mod[0, :bsz], mod[0, bsz:bsz + 1]
    g, b = ln_g[0], ln_b[0]
    x = _ffn_call(x, mod_lat, 0, wi[0, 0], wo[0, 0], g[0], b[0], tm=512)
    ctx_s = _ffn_call(ctx.reshape(1, bsz * n_ctx, D_MODEL), mod_ctx, 0, wi[0, 0], wo[0, 0], g[0], b[0], tm=512)

    w_qkv = attn_wqkv[0].astype(BF16)
    cos_t, sa_t, sb_t = _rope_tables(seq)
    q, k, v = _qkv_call(x, mod_lat, w_qkv, attn_bqkv[0], cos_t, sa_t, sb_t, tm=512)
    kc, vc = _kv_ctx_call(ctx_s, mod_ctx, w_qkv[:, Q_WIDTH:], attn_bqkv[0, Q_WIDTH:], tm=512)
    kc = kc.reshape(bsz, n_ctx, KV_DUP_WIDTH)
    vc = vc.reshape(bsz, n_ctx, KV_DUP_WIDTH)
    pad = ((0, 0), (WINDOW, WINDOW), (0, 0))
    o = _attn_call(q, jnp.pad(k, pad), jnp.pad(v, pad), kc, vc, attn_sink[0])
    x = _proj_norm_call(o, x, mod_lat, attn_wo[0].astype(BF16), g[1], b[1], tm=512)
    x = _ffn_call(x, mod_lat, 2, wi[0, 1], wo[0, 1], g[2], b[2], tm=512)

    mod_lat = mod[1, :bsz]
    g, b = ln_g[1], ln_b[1]
    x = _ffn_call(x, mod_lat, 0, wi[1, 0], wo[1, 0], g[0], b[0], tm=512)
    cos_c, sin_c = _dft_tables(FOURIER_GROUP_CH)
    cos_s, sin_s = _dft_tables(seq)
    fa, fb = _dft_ch_call(x, mod_lat, cos_c.astype(BF16), sin_c.astype(BF16), tm=512)
    x = _dft_seq_call(cos_s.astype(BF16), (-sin_s).astype(BF16), fa, fb, x, mod_lat,
                      fourier_wo[0].astype(BF16), g[1], b[1], tm=512)
    x = _ffn_call(x, mod_lat, 2, wi[1, 1], wo[1, 1], g[2], b[2], tm=512)
    return x
```

```python
import functools

import jax
import jax.numpy as jnp
import numpy as np
from jax import lax
from jax.experimental import pallas as pl
from jax.experimental.pallas import tpu as pltpu

D_MODEL = 1024
DEPTH = 2
GRID_W = 64
N_HEADS = 16
N_KV_HEADS = 4
HEAD_DIM = 64
GROUP = N_HEADS // N_KV_HEADS
Q_WIDTH = N_HEADS * HEAD_DIM
KV_WIDTH = N_KV_HEADS * HEAD_DIM
QKV_WIDTH = Q_WIDTH + 2 * KV_WIDTH
KV_DUP_WIDTH = 2 * KV_WIDTH
WINDOW = 128
BLOCK = 128
SPAN = BLOCK + 2 * WINDOW
ROPE_BASE = 10000.0
AXIS_DIM = HEAD_DIM // 2
FOURIER_GROUPS = 4
FOURIER_GROUP_CH = D_MODEL // FOURIER_GROUPS
D_FF = 2816
N_MOD = 9
LN_EPS = 1e-5
ALPHA = (2.0 * DEPTH) ** 0.25
NEG_INF = -1e30

LANES = 128
MOD_ROWS = 24
VMEM_LIMIT = 56 * 1024 * 1024
FFN_CHUNKS = (1024, 1024, 768)

BF16 = jnp.bfloat16
F32 = jnp.float32


def _params(n_axes):
    return pltpu.CompilerParams(dimension_semantics=("parallel",) * n_axes,
                                vmem_limit_bytes=VMEM_LIMIT)


def _resident(shape):
    return pl.BlockSpec(shape, lambda *_: (0,) * len(shape), pipeline_mode=pl.Buffered(1))


def _layer_norm(z, g, b):
    mu = jnp.mean(z, axis=-1, keepdims=True)
    d = z - mu
    var = jnp.mean(d * d, axis=-1, keepdims=True)
    return d * lax.rsqrt(var + LN_EPS) * g + b


def _modulate(x, mod_ref, j):
    shift = mod_ref[3 * j:3 * j + 1, :]
    scale = mod_ref[3 * j + 1:3 * j + 2, :]
    return x * (1.0 + scale) + shift


def _post_norm(x, y, mod_ref, j, g_ref, b_ref):
    gate = mod_ref[3 * j + 2:3 * j + 3, :]
    return _layer_norm(ALPHA * x + gate * y, g_ref[...], b_ref[...])


def _mod_kernel(cv_ref, w_ref, b_ref, o_ref):
    cv = cv_ref[...]
    s = (cv * jax.nn.sigmoid(cv)).astype(BF16)
    o_ref[...] = jnp.dot(s, w_ref[...].astype(BF16), preferred_element_type=F32) + b_ref[...]


def _mod_call(cv, mod_w, mod_b):
    tn = D_MODEL
    n_out = N_MOD * D_MODEL
    return pl.pallas_call(
        _mod_kernel,
        out_shape=jax.ShapeDtypeStruct((DEPTH, MOD_ROWS, n_out), F32),
        grid=(DEPTH, n_out // tn),
        in_specs=[
            pl.BlockSpec((MOD_ROWS, D_MODEL), lambda i, n: (0, 0)),
            pl.BlockSpec((None, D_MODEL, tn), lambda i, n: (i, 0, n)),
            pl.BlockSpec((None, 1, tn), lambda i, n: (i, 0, n)),
        ],
        out_specs=pl.BlockSpec((None, MOD_ROWS, tn), lambda i, n: (i, 0, n)),
        compiler_params=_params(2),
        name="mod",
    )(cv, mod_w, mod_b.reshape(DEPTH, 1, n_out))


def _ffn_kernel(j, n_tiles, x_ref, xp_ref, mod_ref, modp_ref, wi_ref, wo_ref, g_ref, b_ref, o_ref, u_ref, y_ref):
    i = pl.program_id(0)

    @pl.when(i == 0)
    def _():
        y_ref[...] = jnp.zeros_like(y_ref)

    def post_norm_previous():
        o_ref[...] = _post_norm(xp_ref[...], y_ref[...], modp_ref, j, g_ref, b_ref)

    @pl.when(i < n_tiles)
    def _():
        post_norm_previous()
        h = _modulate(x_ref[...], mod_ref, j).astype(BF16)
        start = 0
        for width in FFN_CHUNKS:
            a = jnp.dot(h, wi_ref[:, start:start + width], preferred_element_type=F32)
            g = jnp.dot(h, wi_ref[:, D_FF + start:D_FF + start + width], preferred_element_type=F32)
            u_ref[:, start:start + width] = (a * (g * jax.nn.sigmoid(g))).astype(BF16)
            start += width
        y_ref[...] = 0.5 * jnp.dot(u_ref[...], wo_ref[...], preferred_element_type=F32)

    @pl.when(i == n_tiles)
    def _():
        post_norm_previous()


def _ffn_call(x, mod, j, wi_all, wo_all, layer, slot, g, b, tm):
    bsz, length, _ = x.shape
    per_batch = length // tm
    n_tiles = bsz * per_batch
    cur = lambda i: jnp.minimum(i, n_tiles - 1)
    prev = lambda i: jnp.maximum(i - 1, 0)
    tile = lambda pick: pl.BlockSpec((None, tm, D_MODEL), lambda i: (pick(i) // per_batch, pick(i) % per_batch, 0))
    mod_rows = lambda pick: pl.BlockSpec((None, N_MOD, D_MODEL), lambda i: (pick(i) // per_batch, 0, 0))
    weight = lambda rows, cols: pl.BlockSpec((None, None, rows, cols), lambda i: (layer, slot, 0, 0),
                                             pipeline_mode=pl.Buffered(1))
    return pl.pallas_call(
        functools.partial(_ffn_kernel, j, n_tiles),
        out_shape=jax.ShapeDtypeStruct(x.shape, F32),
        grid=(n_tiles + 1,),
        in_specs=[
            tile(cur), tile(prev), mod_rows(cur), mod_rows(prev),
            weight(D_MODEL, 2 * D_FF),
            weight(D_FF, D_MODEL),
            _resident((1, D_MODEL)),
            _resident((1, D_MODEL)),
        ],
        out_specs=tile(prev),
        scratch_shapes=[pltpu.VMEM((tm, D_FF), BF16), pltpu.VMEM((tm, D_MODEL), F32)],
        compiler_params=pltpu.CompilerParams(dimension_semantics=("arbitrary",), vmem_limit_bytes=VMEM_LIMIT),
        name="ffn",
    )(x, x, mod, mod, wi_all, wo_all, g.reshape(1, D_MODEL), b.reshape(1, D_MODEL))


def _rope(t, cos_ref, sa_ref, sb_ref):
    half = AXIS_DIM // 2
    outs = []
    for c in range(t.shape[1] // LANES):
        tc = t[:, c * LANES:(c + 1) * LANES]
        up = pltpu.roll(tc, LANES - half, 1)
        down = pltpu.roll(tc, half, 1)
        outs.append(tc * cos_ref[...] + up * sa_ref[...] + down * sb_ref[...])
    return jnp.concatenate(outs, axis=1)


def _dup_heads(t):
    pieces = []
    for h in range(t.shape[1] // HEAD_DIM):
        head = t[:, h * HEAD_DIM:(h + 1) * HEAD_DIM]
        pieces += [head, head]
    return jnp.concatenate(pieces, axis=1)


def _qkv_kernel(x_ref, mod_ref, w_ref, b_ref, cos_ref, sa_ref, sb_ref, q_ref, k_ref, v_ref):
    h = _modulate(x_ref[...], mod_ref, 1).astype(BF16)
    qkv = jnp.dot(h, w_ref[...], preferred_element_type=F32) + b_ref[...]
    q = qkv[:, :Q_WIDTH] * (HEAD_DIM ** -0.5)
    q_ref[...] = _rope(q, cos_ref, sa_ref, sb_ref).astype(BF16)
    k = _rope(qkv[:, Q_WIDTH:Q_WIDTH + KV_WIDTH], cos_ref, sa_ref, sb_ref)
    k_ref[...] = _dup_heads(k).astype(BF16)
    v_ref[...] = _dup_heads(qkv[:, Q_WIDTH + KV_WIDTH:]).astype(BF16)


def _qkv_call(x, mod, w, b, cos_t, sa_t, sb_t, tm):
    bsz, length, _ = x.shape
    tok = lambda width: pl.BlockSpec((None, tm, width), lambda bi, t: (bi, t, 0))
    table = pl.BlockSpec((tm, LANES), lambda bi, t: (t, 0))
    return pl.pallas_call(
        _qkv_kernel,
        out_shape=(jax.ShapeDtypeStruct((bsz, length, Q_WIDTH), BF16),
                   jax.ShapeDtypeStruct((bsz, length, KV_DUP_WIDTH), BF16),
                   jax.ShapeDtypeStruct((bsz, length, KV_DUP_WIDTH), BF16)),
        grid=(bsz, length // tm),
        in_specs=[
            tok(D_MODEL),
            pl.BlockSpec((None, N_MOD, D_MODEL), lambda bi, t: (bi, 0, 0)),
            _resident((D_MODEL, QKV_WIDTH)),
            _resident((1, QKV_WIDTH)),
            table, table, table,
        ],
        out_specs=(tok(Q_WIDTH), tok(KV_DUP_WIDTH), tok(KV_DUP_WIDTH)),
        compiler_params=_params(2),
        name="qkv",
    )(x, mod, w, b.reshape(1, QKV_WIDTH), cos_t, sa_t, sb_t)


def _kv_ctx_kernel(x_ref, mod_ref, w_ref, b_ref, k_ref, v_ref):
    h = _modulate(x_ref[...], mod_ref, 1).astype(BF16)
    kv = jnp.dot(h, w_ref[...], preferred_element_type=F32) + b_ref[...]
    k_ref[...] = _dup_heads(kv[:, :KV_WIDTH]).astype(BF16)
    v_ref[...] = _dup_heads(kv[:, KV_WIDTH:]).astype(BF16)


def _kv_ctx_call(x, mod, w, b, tm):
    bsz, length, _ = x.shape
    tok = lambda width: pl.BlockSpec((None, tm, width), lambda bi, t: (bi, t, 0))
    return pl.pallas_call(
        _kv_ctx_kernel,
        out_shape=(jax.ShapeDtypeStruct((bsz, length, KV_DUP_WIDTH), BF16),
                   jax.ShapeDtypeStruct((bsz, length, KV_DUP_WIDTH), BF16)),
        grid=(bsz, length // tm),
        in_specs=[
            tok(D_MODEL),
            pl.BlockSpec((None, N_MOD, D_MODEL), lambda bi, t: (bi, 0, 0)),
            _resident((D_MODEL, 2 * KV_WIDTH)),
            _resident((1, 2 * KV_WIDTH)),
        ],
        out_specs=(tok(KV_DUP_WIDTH), tok(KV_DUP_WIDTH)),
        compiler_params=_params(2),
        name="kv_ctx",
    )(x, mod, w, b.reshape(1, 2 * KV_WIDTH))


def _attn_kernel(n_blocks, sink_ref, q_ref, kp_ref, vp_ref, kc_ref, vc_ref, o_ref):
    qb = pl.program_id(1)
    starts = [pl.multiple_of(jnp.clip(qb + d, 0, n_blocks - 1) * BLOCK, BLOCK) for d in (-1, 0, 1)]
    k_all = jnp.concatenate([kp_ref[pl.ds(s0, BLOCK), :] for s0 in starts] + [kc_ref[...]], axis=0)
    v_all = jnp.concatenate([vp_ref[pl.ds(s0, BLOCK), :] for s0 in starts] + [vc_ref[...]], axis=0)

    rows2 = 2 * BLOCK
    r = lax.broadcasted_iota(jnp.int32, (rows2, BLOCK), 0)
    cidx = lax.broadcasted_iota(jnp.int32, (rows2, BLOCK), 1)
    r_in = r & (BLOCK - 1)
    left_ok = (r_in <= cidx) & (qb > 0)
    right_ok = (cidx <= r_in) & (qb < n_blocks - 1)
    top = r[:, :1] < BLOCK
    lo_q = lax.broadcasted_iota(jnp.int32, (rows2, LANES), 1) < HEAD_DIM
    lo_kv = lax.broadcasted_iota(jnp.int32, (k_all.shape[0], LANES), 1) < HEAD_DIM

    for h in range(N_KV_HEADS):
        k_h = k_all[:, h * LANES:(h + 1) * LANES]
        v_h = v_all[:, h * LANES:(h + 1) * LANES]
        c0 = 2 * h * LANES
        q_cat = jnp.concatenate([q_ref[:, c0:c0 + LANES], q_ref[:, c0 + LANES:c0 + 2 * LANES]], axis=0)
        acc = None
        inv = []
        for half in range(2):
            keep = lo_q if half == 0 else jnp.logical_not(lo_q)
            q_sel = jnp.where(keep, q_cat, jnp.zeros_like(q_cat))
            s = lax.dot_general(q_sel, k_h, (((1,), (1,)), ((), ())), preferred_element_type=F32)
            sink = jnp.where(top, sink_ref[4 * h + half], sink_ref[4 * h + 2 + half])
            parts = [jnp.where(left_ok, s[:, :BLOCK], NEG_INF),
                     s[:, BLOCK:2 * BLOCK],
                     jnp.where(right_ok, s[:, 2 * BLOCK:SPAN], NEG_INF)]
            parts += [s[:, c:c + BLOCK] for c in range(SPAN, s.shape[1], BLOCK)]
            m = functools.reduce(jnp.maximum, parts)
            m = jnp.maximum(jnp.max(m, axis=-1, keepdims=True), sink)
            p = [jnp.exp(t - m) for t in parts]
            denom = jnp.sum(functools.reduce(jnp.add, p), axis=-1, keepdims=True) + jnp.exp(sink - m)
            inv.append(1.0 / denom)
            keep_v = lo_kv if half == 0 else jnp.logical_not(lo_kv)
            v_sel = jnp.where(keep_v, v_h, jnp.zeros_like(v_h))
            pv = jnp.dot(jnp.concatenate(p, axis=1).astype(BF16), v_sel, preferred_element_type=F32)
            acc = pv if acc is None else acc + pv
        out = (acc * jnp.where(lo_q, inv[0], inv[1])).astype(BF16)
        o_ref[:, c0:c0 + LANES] = out[:BLOCK]
        o_ref[:, c0 + LANES:c0 + 2 * LANES] = out[BLOCK:]


def _attn_call(q, kp, vp, kc, vc, sink):
    bsz, length, _ = q.shape
    n_blocks = length // BLOCK
    n_ctx = kc.shape[1]
    per_batch = lambda rows: pl.BlockSpec((None, rows, KV_DUP_WIDTH), lambda bi, t, *_: (bi, 0, 0))
    return pl.pallas_call(
        functools.partial(_attn_kernel, n_blocks),
        out_shape=jax.ShapeDtypeStruct((bsz, length, Q_WIDTH), BF16),
        grid_spec=pltpu.PrefetchScalarGridSpec(
            num_scalar_prefetch=1,
            grid=(bsz, n_blocks),
            in_specs=[
                pl.BlockSpec((None, BLOCK, Q_WIDTH), lambda bi, t, *_: (bi, t, 0)),
                per_batch(length),
                per_batch(length),
                per_batch(n_ctx),
                per_batch(n_ctx),
            ],
            out_specs=pl.BlockSpec((None, BLOCK, Q_WIDTH), lambda bi, t, *_: (bi, t, 0)),
        ),
        compiler_params=_params(2),
        name="attn",
    )(sink, q, kp, vp, kc, vc)


def _proj_norm_kernel(o_ref, x_ref, mod_ref, w_ref, g_ref, b_ref, out_ref):
    y = jnp.dot(o_ref[...], w_ref[...], preferred_element_type=F32)
    out_ref[...] = _post_norm(x_ref[...], y, mod_ref, 1, g_ref, b_ref)


def _proj_norm_call(o, x, mod, w, g, b, tm):
    bsz, length, _ = x.shape
    tok = pl.BlockSpec((None, tm, D_MODEL), lambda bi, t: (bi, t, 0))
    return pl.pallas_call(
        _proj_norm_kernel,
        out_shape=jax.ShapeDtypeStruct(x.shape, F32),
        grid=(bsz, length // tm),
        in_specs=[
            tok, tok,
            pl.BlockSpec((None, N_MOD, D_MODEL), lambda bi, t: (bi, 0, 0)),
            _resident((D_MODEL, D_MODEL)),
            _resident((1, D_MODEL)),
            _resident((1, D_MODEL)),
        ],
        out_specs=tok,
        compiler_params=_params(2),
        name="proj_norm",
    )(o, x, mod, w, g.reshape(1, D_MODEL), b.reshape(1, D_MODEL))


def _dft_ch_kernel(x_ref, mod_ref, cc_ref, sc_ref, a_ref, b_ref):
    h = _modulate(x_ref[...], mod_ref, 1).astype(BF16)
    for grp in range(FOURIER_GROUPS):
        cols = slice(grp * FOURIER_GROUP_CH, (grp + 1) * FOURIER_GROUP_CH)
        a_ref[:, cols] = jnp.dot(h[:, cols], cc_ref[...], preferred_element_type=F32).astype(BF16)
        b_ref[:, cols] = jnp.dot(h[:, cols], sc_ref[...], preferred_element_type=F32).astype(BF16)


def _dft_ch_call(x, mod, cc, sc, tm):
    bsz, length, _ = x.shape
    ch = FOURIER_GROUP_CH
    tok = pl.BlockSpec((None, tm, D_MODEL), lambda bi, t: (bi, t, 0))
    out = jax.ShapeDtypeStruct((bsz, length, D_MODEL), BF16)
    return pl.pallas_call(
        _dft_ch_kernel,
        out_shape=(out, out),
        grid=(bsz, length // tm),
        in_specs=[
            tok,
            pl.BlockSpec((None, N_MOD, D_MODEL), lambda bi, t: (bi, 0, 0)),
            _resident((ch, ch)),
            _resident((ch, ch)),
        ],
        out_specs=(tok, tok),
        compiler_params=_params(2),
        name="dft_ch",
    )(x, mod, cc, sc)


def _dft_seq_kernel(norm, wc_ref, ws_ref, a_ref, b_ref, x_ref, mod_ref, wo_ref, g_ref, beta_ref, out_ref):
    f = (jnp.dot(wc_ref[...], a_ref[...], preferred_element_type=F32)
         + jnp.dot(ws_ref[...], b_ref[...], preferred_element_type=F32)) * norm
    y = jnp.dot(f.astype(BF16), wo_ref[...], preferred_element_type=F32)
    out_ref[...] = _post_norm(x_ref[...], y, mod_ref, 1, g_ref, beta_ref)


def _dft_seq_call(w_cos, w_nsin, a, b, x, mod, wo, g, beta, tm):
    bsz, length, _ = x.shape
    norm = float(1.0 / np.sqrt(length * FOURIER_GROUP_CH))
    tok = pl.BlockSpec((None, tm, D_MODEL), lambda bi, t: (bi, t, 0))
    twiddle = pl.BlockSpec((tm, length), lambda bi, t: (t, 0))
    per_batch = pl.BlockSpec((None, length, D_MODEL), lambda bi, t: (bi, 0, 0))
    return pl.pallas_call(
        functools.partial(_dft_seq_kernel, norm),
        out_shape=jax.ShapeDtypeStruct(x.shape, F32),
        grid=(bsz, length // tm),
        in_specs=[
            twiddle, twiddle, per_batch, per_batch,
            tok,
            pl.BlockSpec((None, N_MOD, D_MODEL), lambda bi, t: (bi, 0, 0)),
            _resident((D_MODEL, D_MODEL)),
            _resident((1, D_MODEL)),
            _resident((1, D_MODEL)),
        ],
        out_specs=tok,
        compiler_params=_params(2),
        name="dft_seq",
    )(w_cos, w_nsin, a, b, x, mod, wo, g.reshape(1, D_MODEL), beta.reshape(1, D_MODEL))


def _rope_tables(n_tokens):
    rows = n_tokens // GRID_W
    row = jnp.repeat(jnp.arange(rows), GRID_W).astype(F32)
    col = jnp.tile(jnp.arange(GRID_W), rows).astype(F32)
    inv = ROPE_BASE ** (-jnp.arange(0, AXIS_DIM, 2, dtype=F32) / AXIS_DIM)
    ang_r, ang_c = row[:, None] * inv, col[:, None] * inv
    cos_r, sin_r, cos_c, sin_c = jnp.cos(ang_r), jnp.sin(ang_r), jnp.cos(ang_c), jnp.sin(ang_c)
    zero = jnp.zeros_like(sin_r)
    cos_h = jnp.concatenate([cos_r, cos_r, cos_c, cos_c], axis=-1)
    sa_h = jnp.concatenate([-sin_r, zero, -sin_c, zero], axis=-1)
    sb_h = jnp.concatenate([zero, sin_r, zero, sin_c], axis=-1)
    rep = LANES // HEAD_DIM
    return tuple(jnp.tile(t, (1, rep)) for t in (cos_h, sa_h, sb_h))


def _dft_tables(n):
    idx = jnp.arange(n, dtype=jnp.int32)
    ang = ((idx[:, None] * idx[None, :]) % n).astype(F32) * (2.0 * np.pi / n)
    return jnp.cos(ang), jnp.sin(ang)


def kernel(x, c, ctx, c_ctx, mod_w, mod_b, ln_g, ln_b, ffn_wi, ffn_wo,
           attn_wqkv, attn_bqkv, attn_wo, attn_sink, fourier_wo):
    bsz, seq, _ = x.shape
    n_ctx = ctx.shape[1]
    assert DEPTH == 2 and seq % BLOCK == 0 and seq % GRID_W == 0

    cv = jnp.concatenate([c, c_ctx[None, :], jnp.zeros((MOD_ROWS - bsz - 1, D_MODEL), F32)], axis=0)
    mod = _mod_call(cv, mod_w, mod_b).reshape(DEPTH, MOD_ROWS, N_MOD, D_MODEL)
    wi = ffn_wi.astype(BF16)
    wo = ffn_wo.astype(BF16)

    mod_lat, mod_ctx = mod[0, :bsz], mod[0, bsz:bsz + 1]
    g, b = ln_g[0], ln_b[0]
    x = _ffn_call(x, mod_lat, 0, wi, wo, 0, 0, g[0], b[0], tm=512)
    ctx_s = _ffn_call(ctx.reshape(1, bsz * n_ctx, D_MODEL), mod_ctx, 0, wi, wo, 0, 0, g[0], b[0], tm=512)

    w_qkv = attn_wqkv[0].astype(BF16)
    cos_t, sa_t, sb_t = _rope_tables(seq)
    q, k, v = _qkv_call(x, mod_lat, w_qkv, attn_bqkv[0], cos_t, sa_t, sb_t, tm=512)
    kc, vc = _kv_ctx_call(ctx_s, mod_ctx, w_qkv[:, Q_WIDTH:], attn_bqkv[0, Q_WIDTH:], tm=512)
    kc = kc.reshape(bsz, n_ctx, KV_DUP_WIDTH)
    vc = vc.reshape(bsz, n_ctx, KV_DUP_WIDTH)
    o = _attn_call(q, k, v, kc, vc, attn_sink[0])
    x = _proj_norm_call(o, x, mod_lat, attn_wo[0].astype(BF16), g[1], b[1], tm=512)
    x = _ffn_call(x, mod_lat, 2, wi, wo, 0, 1, g[2], b[2], tm=512)

    mod_lat = mod[1, :bsz]
    g, b = ln_g[1], ln_b[1]
    x = _ffn_call(x, mod_lat, 0, wi, wo, 1, 0, g[0], b[0], tm=512)
    cos_c, sin_c = _dft_tables(FOURIER_GROUP_CH)
    cos_s, sin_s = _dft_tables(seq)
    fa, fb = _dft_ch_call(x, mod_lat, cos_c.astype(BF16), sin_c.astype(BF16), tm=512)
    x = _dft_seq_call(cos_s.astype(BF16), (-sin_s).astype(BF16), fa, fb, x, mod_lat,
                      fourier_wo[0].astype(BF16), g[1], b[1], tm=512)
    x = _ffn_call(x, mod_lat, 2, wi, wo, 1, 1, g[2], b[2], tm=512)
    return x
```

```python
import functools

import jax
import jax.numpy as jnp
import numpy as np
from jax import lax
from jax.experimental import pallas as pl
from jax.experimental.pallas import tpu as pltpu

D_MODEL = 1024
DEPTH = 2
GRID_W = 64
N_HEADS = 16
N_KV_HEADS = 4
HEAD_DIM = 64
GROUP = N_HEADS // N_KV_HEADS
Q_WIDTH = N_HEADS * HEAD_DIM
KV_WIDTH = N_KV_HEADS * HEAD_DIM
QKV_WIDTH = Q_WIDTH + 2 * KV_WIDTH
KV_DUP_WIDTH = 2 * KV_WIDTH
WINDOW = 128
BLOCK = 128
SPAN = BLOCK + 2 * WINDOW
ROPE_BASE = 10000.0
AXIS_DIM = HEAD_DIM // 2
FOURIER_GROUPS = 4
FOURIER_GROUP_CH = D_MODEL // FOURIER_GROUPS
D_FF = 2816
N_MOD = 9
LN_EPS = 1e-5
ALPHA = (2.0 * DEPTH) ** 0.25
NEG_INF = -1e30

LANES = 128
MOD_ROWS = 24
VMEM_LIMIT = 56 * 1024 * 1024
FFN_CHUNKS = (1024, 1024, 768)
ATTN_ROWS = 32

BF16 = jnp.bfloat16
F32 = jnp.float32


def _params(n_axes):
    return pltpu.CompilerParams(dimension_semantics=("parallel",) * n_axes,
                                vmem_limit_bytes=VMEM_LIMIT)


def _resident(shape):
    return pl.BlockSpec(shape, lambda *_: (0,) * len(shape), pipeline_mode=pl.Buffered(1))


def _layer_norm(z, g, b):
    mu = jnp.mean(z, axis=-1, keepdims=True)
    d = z - mu
    var = jnp.mean(d * d, axis=-1, keepdims=True)
    return d * lax.rsqrt(var + LN_EPS) * g + b


def _modulate(x, mod_ref, j):
    shift = mod_ref[3 * j:3 * j + 1, :]
    scale = mod_ref[3 * j + 1:3 * j + 2, :]
    return x * (1.0 + scale) + shift


def _post_norm(x, y, mod_ref, j, g_ref, b_ref):
    gate = mod_ref[3 * j + 2:3 * j + 3, :]
    return _layer_norm(ALPHA * x + gate * y, g_ref[...], b_ref[...])


def _mod_kernel(cv_ref, w_ref, b_ref, o_ref):
    cv = cv_ref[...]
    s = (cv * jax.nn.sigmoid(cv)).astype(BF16)
    o_ref[...] = jnp.dot(s, w_ref[...].astype(BF16), preferred_element_type=F32) + b_ref[...]


def _mod_call(cv, mod_w, mod_b):
    tn = D_MODEL
    n_out = N_MOD * D_MODEL
    return pl.pallas_call(
        _mod_kernel,
        out_shape=jax.ShapeDtypeStruct((DEPTH, MOD_ROWS, n_out), F32),
        grid=(DEPTH, n_out // tn),
        in_specs=[
            pl.BlockSpec((MOD_ROWS, D_MODEL), lambda i, n: (0, 0)),
            pl.BlockSpec((None, D_MODEL, tn), lambda i, n: (i, 0, n)),
            pl.BlockSpec((None, 1, tn), lambda i, n: (i, 0, n)),
        ],
        out_specs=pl.BlockSpec((None, MOD_ROWS, tn), lambda i, n: (i, 0, n)),
        compiler_params=_params(2),
        name="mod",
    )(cv, mod_w, mod_b.reshape(DEPTH, 1, n_out))


def _ffn_kernel(j, n_tiles, x_ref, xp_ref, mod_ref, modp_ref, wi_ref, wo_ref, g_ref, b_ref, o_ref, u_ref, y_ref):
    i = pl.program_id(0)

    @pl.when(i == 0)
    def _():
        y_ref[...] = jnp.zeros_like(y_ref)

    def post_norm_previous():
        o_ref[...] = _post_norm(xp_ref[...], y_ref[...], modp_ref, j, g_ref, b_ref)

    @pl.when(i < n_tiles)
    def _():
        post_norm_previous()
        h = _modulate(x_ref[...], mod_ref, j).astype(BF16)
        start = 0
        for width in FFN_CHUNKS:
            a = jnp.dot(h, wi_ref[:, start:start + width], preferred_element_type=F32)
            g = jnp.dot(h, wi_ref[:, D_FF + start:D_FF + start + width], preferred_element_type=F32)
            u_ref[:, start:start + width] = (a * (g * jax.nn.sigmoid(g))).astype(BF16)
            start += width
        y_ref[...] = 0.5 * jnp.dot(u_ref[...], wo_ref[...], preferred_element_type=F32)

    @pl.when(i == n_tiles)
    def _():
        post_norm_previous()


def _ffn_call(x, mod, j, wi_all, wo_all, layer, slot, g, b, tm):
    bsz, length, _ = x.shape
    per_batch = length // tm
    n_tiles = bsz * per_batch
    cur = lambda i: jnp.minimum(i, n_tiles - 1)
    prev = lambda i: jnp.maximum(i - 1, 0)
    tile = lambda pick: pl.BlockSpec((None, tm, D_MODEL), lambda i: (pick(i) // per_batch, pick(i) % per_batch, 0))
    mod_rows = lambda pick: pl.BlockSpec((None, N_MOD, D_MODEL), lambda i: (pick(i) // per_batch, 0, 0))
    weight = lambda rows, cols: pl.BlockSpec((None, None, rows, cols), lambda i: (layer, slot, 0, 0),
                                             pipeline_mode=pl.Buffered(1))
    return pl.pallas_call(
        functools.partial(_ffn_kernel, j, n_tiles),
        out_shape=jax.ShapeDtypeStruct(x.shape, F32),
        grid=(n_tiles + 1,),
        in_specs=[
            tile(cur), tile(prev), mod_rows(cur), mod_rows(prev),
            weight(D_MODEL, 2 * D_FF),
            weight(D_FF, D_MODEL),
            _resident((1, D_MODEL)),
            _resident((1, D_MODEL)),
        ],
        out_specs=tile(prev),
        scratch_shapes=[pltpu.VMEM((tm, D_FF), BF16), pltpu.VMEM((tm, D_MODEL), F32)],
        compiler_params=pltpu.CompilerParams(dimension_semantics=("arbitrary",), vmem_limit_bytes=VMEM_LIMIT),
        name="ffn",
    )(x, x, mod, mod, wi_all, wo_all, g.reshape(1, D_MODEL), b.reshape(1, D_MODEL))


def _rope(t, cos_ref, sa_ref, sb_ref):
    half = AXIS_DIM // 2
    outs = []
    for c in range(t.shape[1] // LANES):
        tc = t[:, c * LANES:(c + 1) * LANES]
        up = pltpu.roll(tc, LANES - half, 1)
        down = pltpu.roll(tc, half, 1)
        outs.append(tc * cos_ref[...] + up * sa_ref[...] + down * sb_ref[...])
    return jnp.concatenate(outs, axis=1)


def _dup_heads(t):
    pieces = []
    for h in range(t.shape[1] // HEAD_DIM):
        head = t[:, h * HEAD_DIM:(h + 1) * HEAD_DIM]
        pieces += [head, head]
    return jnp.concatenate(pieces, axis=1)


def _qkv_kernel(x_ref, mod_ref, w_ref, b_ref, cos_ref, sa_ref, sb_ref, q_ref, k_ref, v_ref):
    h = _modulate(x_ref[...], mod_ref, 1).astype(BF16)
    qkv = jnp.dot(h, w_ref[...], preferred_element_type=F32) + b_ref[...]
    q = qkv[:, :Q_WIDTH] * (HEAD_DIM ** -0.5)
    q_ref[...] = _rope(q, cos_ref, sa_ref, sb_ref).astype(BF16)
    k = _rope(qkv[:, Q_WIDTH:Q_WIDTH + KV_WIDTH], cos_ref, sa_ref, sb_ref)
    k_ref[...] = _dup_heads(k).astype(BF16)
    v_ref[...] = _dup_heads(qkv[:, Q_WIDTH + KV_WIDTH:]).astype(BF16)


def _qkv_call(x, mod, w, b, cos_t, sa_t, sb_t, tm):
    bsz, length, _ = x.shape
    tok = lambda width: pl.BlockSpec((None, tm, width), lambda bi, t: (bi, t, 0))
    table = pl.BlockSpec((tm, LANES), lambda bi, t: (t, 0))
    return pl.pallas_call(
        _qkv_kernel,
        out_shape=(jax.ShapeDtypeStruct((bsz, length, Q_WIDTH), BF16),
                   jax.ShapeDtypeStruct((bsz, length, KV_DUP_WIDTH), BF16),
                   jax.ShapeDtypeStruct((bsz, length, KV_DUP_WIDTH), BF16)),
        grid=(bsz, length // tm),
        in_specs=[
            tok(D_MODEL),
            pl.BlockSpec((None, N_MOD, D_MODEL), lambda bi, t: (bi, 0, 0)),
            _resident((D_MODEL, QKV_WIDTH)),
            _resident((1, QKV_WIDTH)),
            table, table, table,
        ],
        out_specs=(tok(Q_WIDTH), tok(KV_DUP_WIDTH), tok(KV_DUP_WIDTH)),
        compiler_params=_params(2),
        name="qkv",
    )(x, mod, w, b.reshape(1, QKV_WIDTH), cos_t, sa_t, sb_t)


def _kv_ctx_kernel(x_ref, mod_ref, w_ref, b_ref, k_ref, v_ref):
    h = _modulate(x_ref[...], mod_ref, 1).astype(BF16)
    kv = jnp.dot(h, w_ref[...], preferred_element_type=F32) + b_ref[...]
    k_ref[...] = _dup_heads(kv[:, :KV_WIDTH]).astype(BF16)
    v_ref[...] = _dup_heads(kv[:, KV_WIDTH:]).astype(BF16)


def _kv_ctx_call(x, mod, w, b, tm):
    bsz, length, _ = x.shape
    tok = lambda width: pl.BlockSpec((None, tm, width), lambda bi, t: (bi, t, 0))
    return pl.pallas_call(
        _kv_ctx_kernel,
        out_shape=(jax.ShapeDtypeStruct((bsz, length, KV_DUP_WIDTH), BF16),
                   jax.ShapeDtypeStruct((bsz, length, KV_DUP_WIDTH), BF16)),
        grid=(bsz, length // tm),
        in_specs=[
            tok(D_MODEL),
            pl.BlockSpec((None, N_MOD, D_MODEL), lambda bi, t: (bi, 0, 0)),
            _resident((D_MODEL, 2 * KV_WIDTH)),
            _resident((1, 2 * KV_WIDTH)),
        ],
        out_specs=(tok(KV_DUP_WIDTH), tok(KV_DUP_WIDTH)),
        compiler_params=_params(2),
        name="kv_ctx",
    )(x, mod, w, b.reshape(1, 2 * KV_WIDTH))


def _attn_kernel(n_blocks, sink_ref, q_ref, kp_ref, vp_ref, kc_ref, vc_ref, o_ref, s_ref, p_ref, inv_ref):
    n_sub = q_ref.shape[0] // BLOCK
    qb0 = pl.program_id(1) * n_sub
    rows2 = 2 * BLOCK
    n_keys = SPAN + kc_ref.shape[0]
    diff = (lax.broadcasted_iota(jnp.int32, (ATTN_ROWS, BLOCK), 1)
            - lax.broadcasted_iota(jnp.int32, (ATTN_ROWS, BLOCK), 0))
    lo_q = lax.broadcasted_iota(jnp.int32, (rows2, LANES), 1) < HEAD_DIM
    lo_kv = lax.broadcasted_iota(jnp.int32, (n_keys, LANES), 1) < HEAD_DIM

    def keys_values(sub):
        starts = [pl.multiple_of(jnp.clip(qb0 + sub + d, 0, n_blocks - 1) * BLOCK, BLOCK) for d in (-1, 0, 1)]
        k_all = jnp.concatenate([kp_ref[pl.ds(s0, BLOCK), :] for s0 in starts] + [kc_ref[...]], axis=0)
        v_all = jnp.concatenate([vp_ref[pl.ds(s0, BLOCK), :] for s0 in starts] + [vc_ref[...]], axis=0)
        return k_all, v_all

    kv = [keys_values(sub) for sub in range(n_sub)]
    slots_per_block = 2 * N_KV_HEADS

    def scores(slot):
        sub, rest = divmod(slot, slots_per_block)
        h, half = divmod(rest, 2)
        c0 = 2 * h * LANES
        q_rows = slice(sub * BLOCK, (sub + 1) * BLOCK)
        q_cat = jnp.concatenate([q_ref[q_rows, c0:c0 + LANES], q_ref[q_rows, c0 + LANES:c0 + 2 * LANES]], axis=0)
        keep = lo_q if half == 0 else jnp.logical_not(lo_q)
        q_sel = jnp.where(keep, q_cat, jnp.zeros_like(q_cat))
        k_h = kv[sub][0][:, h * LANES:(h + 1) * LANES]
        s_ref[slot] = lax.dot_general(q_sel, k_h, (((1,), (1,)), ((), ())), preferred_element_type=F32)

    def softmax(slot):
        sub, rest = divmod(slot, slots_per_block)
        h, half = divmod(rest, 2)
        has_left, has_right = qb0 + sub > 0, qb0 + sub < n_blocks - 1
        for r0 in range(0, rows2, ATTN_ROWS):
            rows = slice(r0, r0 + ATTN_ROWS)
            in_block = r0 % BLOCK
            sink = sink_ref[4 * h + half + 2 * (r0 // BLOCK)]
            left_ok = diff >= jnp.where(has_left, in_block, 2 * BLOCK)
            right_ok = diff <= jnp.where(has_right, in_block, -2 * BLOCK)
            parts = [jnp.where(left_ok, s_ref[slot, rows, :BLOCK], NEG_INF),
                     s_ref[slot, rows, BLOCK:2 * BLOCK],
                     jnp.where(right_ok, s_ref[slot, rows, 2 * BLOCK:SPAN], NEG_INF)]
            parts += [s_ref[slot, rows, c:c + BLOCK] for c in range(SPAN, n_keys, BLOCK)]
            m = functools.reduce(jnp.maximum, parts)
            m = jnp.maximum(jnp.max(m, axis=-1, keepdims=True), sink)
            p = [jnp.exp(t - m) for t in parts]
            denom = jnp.sum(functools.reduce(jnp.add, p), axis=-1, keepdims=True) + jnp.exp(sink - m)
            inv_ref[slot, rows, :] = jnp.broadcast_to(1.0 / denom, (ATTN_ROWS, LANES))
            p_ref[slot, rows, :] = jnp.concatenate(p, axis=1).astype(BF16)

    def weighted_values(pair):
        sub, h = divmod(pair, N_KV_HEADS)
        v_h = kv[sub][1][:, h * LANES:(h + 1) * LANES]
        v_lo = jnp.where(lo_kv, v_h, jnp.zeros_like(v_h))
        v_hi = jnp.where(lo_kv, jnp.zeros_like(v_h), v_h)
        acc = (jnp.dot(p_ref[2 * pair], v_lo, preferred_element_type=F32)
               + jnp.dot(p_ref[2 * pair + 1], v_hi, preferred_element_type=F32))
        out = (acc * jnp.where(lo_q, inv_ref[2 * pair], inv_ref[2 * pair + 1])).astype(BF16)
        c0 = 2 * h * LANES
        q_rows = slice(sub * BLOCK, (sub + 1) * BLOCK)
        o_ref[q_rows, c0:c0 + LANES] = out[:BLOCK]
        o_ref[q_rows, c0 + LANES:c0 + 2 * LANES] = out[BLOCK:]

    n_slots = n_sub * slots_per_block
    scores(0)
    for slot in range(n_slots):
        if slot + 1 < n_slots:
            scores(slot + 1)
        softmax(slot)
        if slot % 2 == 1:
            weighted_values(slot // 2)


def _attn_call(q, kp, vp, kc, vc, sink, tq):
    bsz, length, _ = q.shape
    n_blocks = length // BLOCK
    n_ctx = kc.shape[1]
    n_slots = (tq // BLOCK) * 2 * N_KV_HEADS
    per_batch = lambda rows: pl.BlockSpec((None, rows, KV_DUP_WIDTH), lambda bi, t, *_: (bi, 0, 0))
    return pl.pallas_call(
        functools.partial(_attn_kernel, n_blocks),
        out_shape=jax.ShapeDtypeStruct((bsz, length, Q_WIDTH), BF16),
        grid_spec=pltpu.PrefetchScalarGridSpec(
            num_scalar_prefetch=1,
            grid=(bsz, length // tq),
            in_specs=[
                pl.BlockSpec((None, tq, Q_WIDTH), lambda bi, t, *_: (bi, t, 0)),
                per_batch(length),
                per_batch(length),
                per_batch(n_ctx),
                per_batch(n_ctx),
            ],
            out_specs=pl.BlockSpec((None, tq, Q_WIDTH), lambda bi, t, *_: (bi, t, 0)),
            scratch_shapes=[
                pltpu.VMEM((n_slots, 2 * BLOCK, SPAN + n_ctx), F32),
                pltpu.VMEM((n_slots, 2 * BLOCK, SPAN + n_ctx), BF16),
                pltpu.VMEM((n_slots, 2 * BLOCK, LANES), F32),
            ],
        ),
        compiler_params=_params(2),
        name="attn",
    )(sink, q, kp, vp, kc, vc)


def _proj_norm_kernel(o_ref, x_ref, mod_ref, w_ref, g_ref, b_ref, out_ref):
    y = jnp.dot(o_ref[...], w_ref[...], preferred_element_type=F32)
    out_ref[...] = _post_norm(x_ref[...], y, mod_ref, 1, g_ref, b_ref)


def _proj_norm_call(o, x, mod, w, g, b, tm):
    bsz, length, _ = x.shape
    tok = pl.BlockSpec((None, tm, D_MODEL), lambda bi, t: (bi, t, 0))
    return pl.pallas_call(
        _proj_norm_kernel,
        out_shape=jax.ShapeDtypeStruct(x.shape, F32),
        grid=(bsz, length // tm),
        in_specs=[
            tok, tok,
            pl.BlockSpec((None, N_MOD, D_MODEL), lambda bi, t: (bi, 0, 0)),
            _resident((D_MODEL, D_MODEL)),
            _resident((1, D_MODEL)),
            _resident((1, D_MODEL)),
        ],
        out_specs=tok,
        compiler_params=_params(2),
        name="proj_norm",
    )(o, x, mod, w, g.reshape(1, D_MODEL), b.reshape(1, D_MODEL))


def _dft_ch_kernel(x_ref, mod_ref, cc_ref, sc_ref, a_ref, b_ref):
    h = _modulate(x_ref[...], mod_ref, 1).astype(BF16)
    for grp in range(FOURIER_GROUPS):
        cols = slice(grp * FOURIER_GROUP_CH, (grp + 1) * FOURIER_GROUP_CH)
        a_ref[:, cols] = jnp.dot(h[:, cols], cc_ref[...], preferred_element_type=F32).astype(BF16)
        b_ref[:, cols] = jnp.dot(h[:, cols], sc_ref[...], preferred_element_type=F32).astype(BF16)


def _dft_ch_call(x, mod, cc, sc, tm):
    bsz, length, _ = x.shape
    ch = FOURIER_GROUP_CH
    tok = pl.BlockSpec((None, tm, D_MODEL), lambda bi, t: (bi, t, 0))
    out = jax.ShapeDtypeStruct((bsz, length, D_MODEL), BF16)
    return pl.pallas_call(
        _dft_ch_kernel,
        out_shape=(out, out),
        grid=(bsz, length // tm),
        in_specs=[
            tok,
            pl.BlockSpec((None, N_MOD, D_MODEL), lambda bi, t: (bi, 0, 0)),
            _resident((ch, ch)),
            _resident((ch, ch)),
        ],
        out_specs=(tok, tok),
        compiler_params=_params(2),
        name="dft_ch",
    )(x, mod, cc, sc)


def _dft_seq_kernel(norm, wc_ref, ws_ref, a_ref, b_ref, x_ref, mod_ref, wo_ref, g_ref, beta_ref, out_ref):
    f = (jnp.dot(wc_ref[...], a_ref[...], preferred_element_type=F32)
         + jnp.dot(ws_ref[...], b_ref[...], preferred_element_type=F32)) * norm
    y = jnp.dot(f.astype(BF16), wo_ref[...], preferred_element_type=F32)
    out_ref[...] = _post_norm(x_ref[...], y, mod_ref, 1, g_ref, beta_ref)


def _dft_seq_call(w_cos, w_nsin, a, b, x, mod, wo, g, beta, tm):
    bsz, length, _ = x.shape
    norm = float(1.0 / np.sqrt(length * FOURIER_GROUP_CH))
    tok = pl.BlockSpec((None, tm, D_MODEL), lambda bi, t: (bi, t, 0))
    twiddle = pl.BlockSpec((tm, length), lambda bi, t: (t, 0))
    per_batch = pl.BlockSpec((None, length, D_MODEL), lambda bi, t: (bi, 0, 0))
    return pl.pallas_call(
        functools.partial(_dft_seq_kernel, norm),
        out_shape=jax.ShapeDtypeStruct(x.shape, F32),
        grid=(bsz, length // tm),
        in_specs=[
            twiddle, twiddle, per_batch, per_batch,
            tok,
            pl.BlockSpec((None, N_MOD, D_MODEL), lambda bi, t: (bi, 0, 0)),
            _resident((D_MODEL, D_MODEL)),
            _resident((1, D_MODEL)),
            _resident((1, D_MODEL)),
        ],
        out_specs=tok,
        compiler_params=_params(2),
        name="dft_seq",
    )(w_cos, w_nsin, a, b, x, mod, wo, g.reshape(1, D_MODEL), beta.reshape(1, D_MODEL))


def _rope_tables(n_tokens):
    rows = n_tokens // GRID_W
    row = jnp.repeat(jnp.arange(rows), GRID_W).astype(F32)
    col = jnp.tile(jnp.arange(GRID_W), rows).astype(F32)
    inv = ROPE_BASE ** (-jnp.arange(0, AXIS_DIM, 2, dtype=F32) / AXIS_DIM)
    ang_r, ang_c = row[:, None] * inv, col[:, None] * inv
    cos_r, sin_r, cos_c, sin_c = jnp.cos(ang_r), jnp.sin(ang_r), jnp.cos(ang_c), jnp.sin(ang_c)
    zero = jnp.zeros_like(sin_r)
    cos_h = jnp.concatenate([cos_r, cos_r, cos_c, cos_c], axis=-1)
    sa_h = jnp.concatenate([-sin_r, zero, -sin_c, zero], axis=-1)
    sb_h = jnp.concatenate([zero, sin_r, zero, sin_c], axis=-1)
    rep = LANES // HEAD_DIM
    return tuple(jnp.tile(t, (1, rep)) for t in (cos_h, sa_h, sb_h))


def _dft_tables(n):
    idx = jnp.arange(n, dtype=jnp.int32)
    ang = ((idx[:, None] * idx[None, :]) % n).astype(F32) * (2.0 * np.pi / n)
    return jnp.cos(ang), jnp.sin(ang)


def kernel(x, c, ctx, c_ctx, mod_w, mod_b, ln_g, ln_b, ffn_wi, ffn_wo,
           attn_wqkv, attn_bqkv, attn_wo, attn_sink, fourier_wo):
    bsz, seq, _ = x.shape
    n_ctx = ctx.shape[1]
    assert DEPTH == 2 and seq % BLOCK == 0 and seq % GRID_W == 0

    cv = jnp.concatenate([c, c_ctx[None, :], jnp.zeros((MOD_ROWS - bsz - 1, D_MODEL), F32)], axis=0)
    mod = _mod_call(cv, mod_w, mod_b).reshape(DEPTH, MOD_ROWS, N_MOD, D_MODEL)
    wi = ffn_wi.astype(BF16)
    wo = ffn_wo.astype(BF16)

    mod_lat, mod_ctx = mod[0, :bsz], mod[0, bsz:bsz + 1]
    g, b = ln_g[0], ln_b[0]
    x = _ffn_call(x, mod_lat, 0, wi, wo, 0, 0, g[0], b[0], tm=512)
    ctx_s = _ffn_call(ctx.reshape(1, bsz * n_ctx, D_MODEL), mod_ctx, 0, wi, wo, 0, 0, g[0], b[0], tm=512)

    w_qkv = attn_wqkv[0].astype(BF16)
    cos_t, sa_t, sb_t = _rope_tables(seq)
    q, k, v = _qkv_call(x, mod_lat, w_qkv, attn_bqkv[0], cos_t, sa_t, sb_t, tm=512)
    kc, vc = _kv_ctx_call(ctx_s, mod_ctx, w_qkv[:, Q_WIDTH:], attn_bqkv[0, Q_WIDTH:], tm=512)
    kc = kc.reshape(bsz, n_ctx, KV_DUP_WIDTH)
    vc = vc.reshape(bsz, n_ctx, KV_DUP_WIDTH)
    o = _attn_call(q, k, v, kc, vc, attn_sink[0], tq=2 * BLOCK)
    x = _proj_norm_call(o, x, mod_lat, attn_wo[0].astype(BF16), g[1], b[1], tm=512)
    x = _ffn_call(x, mod_lat, 2, wi, wo, 0, 1, g[2], b[2], tm=512)

    mod_lat = mod[1, :bsz]
    g, b = ln_g[1], ln_b[1]
    x = _ffn_call(x, mod_lat, 0, wi, wo, 1, 0, g[0], b[0], tm=512)
    cos_c, sin_c = _dft_tables(FOURIER_GROUP_CH)
    cos_s, sin_s = _dft_tables(seq)
    fa, fb = _dft_ch_call(x, mod_lat, cos_c.astype(BF16), sin_c.astype(BF16), tm=512)
    x = _dft_seq_call(cos_s.astype(BF16), (-sin_s).astype(BF16), fa, fb, x, mod_lat,
                      fourier_wo[0].astype(BF16), g[1], b[1], tm=512)
    x = _ffn_call(x, mod_lat, 2, wi, wo, 1, 1, g[2], b[2], tm=512)
    return x
```

```python
import functools

import jax
import jax.numpy as jnp
import numpy as np
from jax import lax
from jax.experimental import pallas as pl
from jax.experimental.pallas import tpu as pltpu

D_MODEL = 1024
DEPTH = 2
GRID_W = 64
N_HEADS = 16
N_KV_HEADS = 4
HEAD_DIM = 64
GROUP = N_HEADS // N_KV_HEADS
Q_WIDTH = N_HEADS * HEAD_DIM
KV_WIDTH = N_KV_HEADS * HEAD_DIM
QKV_WIDTH = Q_WIDTH + 2 * KV_WIDTH
KV_DUP_WIDTH = 2 * KV_WIDTH
WINDOW = 128
BLOCK = 128
SPAN = BLOCK + 2 * WINDOW
ROPE_BASE = 10000.0
AXIS_DIM = HEAD_DIM // 2
FOURIER_GROUPS = 4
FOURIER_GROUP_CH = D_MODEL // FOURIER_GROUPS
D_FF = 2816
N_MOD = 9
LN_EPS = 1e-5
ALPHA = (2.0 * DEPTH) ** 0.25
NEG_INF = -1e30

LANES = 128
MOD_ROWS = 24
VMEM_LIMIT = 56 * 1024 * 1024
FFN_CHUNKS = (1024, 1024, 768)
ATTN_ROWS = 32
FLIP = 256

BF16 = jnp.bfloat16
F32 = jnp.float32


def _params(n_axes):
    return pltpu.CompilerParams(dimension_semantics=("parallel",) * n_axes,
                                vmem_limit_bytes=VMEM_LIMIT)


def _resident(shape):
    return pl.BlockSpec(shape, lambda *_: (0,) * len(shape), pipeline_mode=pl.Buffered(1))


def _layer_norm(z, g, b):
    mu = jnp.mean(z, axis=-1, keepdims=True)
    d = z - mu
    var = jnp.mean(d * d, axis=-1, keepdims=True)
    return d * lax.rsqrt(var + LN_EPS) * g + b


def _modulate(x, mod_ref, j):
    shift = mod_ref[3 * j:3 * j + 1, :]
    scale = mod_ref[3 * j + 1:3 * j + 2, :]
    return x * (1.0 + scale) + shift


def _post_norm(x, y, mod_ref, j, g_ref, b_ref):
    gate = mod_ref[3 * j + 2:3 * j + 3, :]
    return _layer_norm(ALPHA * x + gate * y, g_ref[...], b_ref[...])


def _mod_kernel(cv_ref, w_ref, b_ref, o_ref):
    cv = cv_ref[...]
    s = (cv * jax.nn.sigmoid(cv)).astype(BF16)
    o_ref[...] = jnp.dot(s, w_ref[...].astype(BF16), preferred_element_type=F32) + b_ref[...]


def _mod_call(cv, mod_w, mod_b):
    tn = D_MODEL
    n_out = N_MOD * D_MODEL
    return pl.pallas_call(
        _mod_kernel,
        out_shape=jax.ShapeDtypeStruct((DEPTH, MOD_ROWS, n_out), F32),
        grid=(DEPTH, n_out // tn),
        in_specs=[
            pl.BlockSpec((MOD_ROWS, D_MODEL), lambda i, n: (0, 0)),
            pl.BlockSpec((None, D_MODEL, tn), lambda i, n: (i, 0, n)),
            pl.BlockSpec((None, 1, tn), lambda i, n: (i, 0, n)),
        ],
        out_specs=pl.BlockSpec((None, MOD_ROWS, tn), lambda i, n: (i, 0, n)),
        compiler_params=_params(2),
        name="mod",
    )(cv, mod_w, mod_b.reshape(DEPTH, 1, n_out))


def _ffn_kernel(j, n_tiles, x_ref, xp_ref, mod_ref, modp_ref, wi_ref, wo_ref, g_ref, b_ref, o_ref, u_ref, y_ref):
    i = pl.program_id(0)

    @pl.when(i == 0)
    def _():
        y_ref[...] = jnp.zeros_like(y_ref)

    def post_norm_previous():
        o_ref[...] = _post_norm(xp_ref[...], y_ref[...], modp_ref, j, g_ref, b_ref)

    @pl.when(i < n_tiles)
    def _():
        post_norm_previous()
        h = _modulate(x_ref[...], mod_ref, j).astype(BF16)
        start = 0
        for width in FFN_CHUNKS:
            a = jnp.dot(h, wi_ref[:, start:start + width], preferred_element_type=F32)
            g = jnp.dot(h, wi_ref[:, D_FF + start:D_FF + start + width], preferred_element_type=F32)
            u_ref[:, start:start + width] = (a * (g * jax.nn.sigmoid(g))).astype(BF16)
            start += width
        y_ref[...] = 0.5 * jnp.dot(u_ref[...], wo_ref[...], preferred_element_type=F32)

    @pl.when(i == n_tiles)
    def _():
        post_norm_previous()


def _ffn_call(x, mod, j, wi_all, wo_all, layer, slot, g, b, tm):
    bsz, length, _ = x.shape
    per_batch = length // tm
    n_tiles = bsz * per_batch
    cur = lambda i: jnp.minimum(i, n_tiles - 1)
    prev = lambda i: jnp.maximum(i - 1, 0)
    tile = lambda pick: pl.BlockSpec((None, tm, D_MODEL), lambda i: (pick(i) // per_batch, pick(i) % per_batch, 0))
    mod_rows = lambda pick: pl.BlockSpec((None, N_MOD, D_MODEL), lambda i: (pick(i) // per_batch, 0, 0))
    weight = lambda rows, cols: pl.BlockSpec((None, None, rows, cols), lambda i: (layer, slot, 0, 0),
                                             pipeline_mode=pl.Buffered(1))
    return pl.pallas_call(
        functools.partial(_ffn_kernel, j, n_tiles),
        out_shape=jax.ShapeDtypeStruct(x.shape, F32),
        grid=(n_tiles + 1,),
        in_specs=[
            tile(cur), tile(prev), mod_rows(cur), mod_rows(prev),
            weight(D_MODEL, 2 * D_FF),
            weight(D_FF, D_MODEL),
            _resident((1, D_MODEL)),
            _resident((1, D_MODEL)),
        ],
        out_specs=tile(prev),
        scratch_shapes=[pltpu.VMEM((tm, D_FF), BF16), pltpu.VMEM((tm, D_MODEL), F32)],
        compiler_params=pltpu.CompilerParams(dimension_semantics=("arbitrary",), vmem_limit_bytes=VMEM_LIMIT),
        name="ffn",
    )(x, x, mod, mod, wi_all, wo_all, g.reshape(1, D_MODEL), b.reshape(1, D_MODEL))


def _rope(t, cos_ref, sa_ref, sb_ref):
    half = AXIS_DIM // 2
    outs = []
    for c in range(t.shape[1] // LANES):
        tc = t[:, c * LANES:(c + 1) * LANES]
        up = pltpu.roll(tc, LANES - half, 1)
        down = pltpu.roll(tc, half, 1)
        outs.append(tc * cos_ref[...] + up * sa_ref[...] + down * sb_ref[...])
    return jnp.concatenate(outs, axis=1)


def _dup_heads(t):
    pieces = []
    for h in range(t.shape[1] // HEAD_DIM):
        head = t[:, h * HEAD_DIM:(h + 1) * HEAD_DIM]
        pieces += [head, head]
    return jnp.concatenate(pieces, axis=1)


def _qkv_kernel(x_ref, mod_ref, w_ref, b_ref, cos_ref, sa_ref, sb_ref, q_ref, k_ref, v_ref):
    h = _modulate(x_ref[...], mod_ref, 1).astype(BF16)
    qkv = jnp.dot(h, w_ref[...], preferred_element_type=F32) + b_ref[...]
    q = qkv[:, :Q_WIDTH] * (HEAD_DIM ** -0.5)
    q_ref[...] = _rope(q, cos_ref, sa_ref, sb_ref).astype(BF16)
    k = _rope(qkv[:, Q_WIDTH:Q_WIDTH + KV_WIDTH], cos_ref, sa_ref, sb_ref)
    k_ref[...] = _dup_heads(k).astype(BF16)
    v_ref[...] = _dup_heads(qkv[:, Q_WIDTH + KV_WIDTH:]).astype(BF16)


def _qkv_call(x, mod, w, b, cos_t, sa_t, sb_t, tm):
    bsz, length, _ = x.shape
    tok = lambda width: pl.BlockSpec((None, tm, width), lambda bi, t: (bi, t, 0))
    table = pl.BlockSpec((tm, LANES), lambda bi, t: (t, 0))
    return pl.pallas_call(
        _qkv_kernel,
        out_shape=(jax.ShapeDtypeStruct((bsz, length, Q_WIDTH), BF16),
                   jax.ShapeDtypeStruct((bsz, length, KV_DUP_WIDTH), BF16),
                   jax.ShapeDtypeStruct((bsz, length, KV_DUP_WIDTH), BF16)),
        grid=(bsz, length // tm),
        in_specs=[
            tok(D_MODEL),
            pl.BlockSpec((None, N_MOD, D_MODEL), lambda bi, t: (bi, 0, 0)),
            _resident((D_MODEL, QKV_WIDTH)),
            _resident((1, QKV_WIDTH)),
            table, table, table,
        ],
        out_specs=(tok(Q_WIDTH), tok(KV_DUP_WIDTH), tok(KV_DUP_WIDTH)),
        compiler_params=_params(2),
        name="qkv",
    )(x, mod, w, b.reshape(1, QKV_WIDTH), cos_t, sa_t, sb_t)


def _kv_ctx_kernel(x_ref, mod_ref, w_ref, b_ref, k_ref, v_ref):
    h = _modulate(x_ref[...], mod_ref, 1).astype(BF16)
    kv = jnp.dot(h, w_ref[...], preferred_element_type=F32) + b_ref[...]
    k_ref[...] = _dup_heads(kv[:, :KV_WIDTH]).astype(BF16)
    v_ref[...] = _dup_heads(kv[:, KV_WIDTH:]).astype(BF16)


def _kv_ctx_call(x, mod, w, b, tm):
    bsz, length, _ = x.shape
    tok = lambda width: pl.BlockSpec((None, tm, width), lambda bi, t: (bi, t, 0))
    return pl.pallas_call(
        _kv_ctx_kernel,
        out_shape=(jax.ShapeDtypeStruct((bsz, length, KV_DUP_WIDTH), BF16),
                   jax.ShapeDtypeStruct((bsz, length, KV_DUP_WIDTH), BF16)),
        grid=(bsz, length // tm),
        in_specs=[
            tok(D_MODEL),
            pl.BlockSpec((None, N_MOD, D_MODEL), lambda bi, t: (bi, 0, 0)),
            _resident((D_MODEL, 2 * KV_WIDTH)),
            _resident((1, 2 * KV_WIDTH)),
        ],
        out_specs=(tok(KV_DUP_WIDTH), tok(KV_DUP_WIDTH)),
        compiler_params=_params(2),
        name="kv_ctx",
    )(x, mod, w, b.reshape(1, 2 * KV_WIDTH))


def _attn_kernel(n_blocks, sink_ref, q_ref, kp_ref, vp_ref, kc_ref, vc_ref, o_ref, s_ref, p_ref, inv_ref):
    n_sub = q_ref.shape[0] // BLOCK
    qb0 = pl.program_id(1) * n_sub
    rows2 = 2 * BLOCK
    n_keys = SPAN + kc_ref.shape[0]
    diff = (lax.broadcasted_iota(jnp.int32, (ATTN_ROWS, BLOCK), 1)
            - lax.broadcasted_iota(jnp.int32, (ATTN_ROWS, BLOCK), 0))
    lo_q = lax.broadcasted_iota(jnp.int32, (rows2, LANES), 1) < HEAD_DIM
    lo_kv = lax.broadcasted_iota(jnp.int32, (n_keys, LANES), 1) < HEAD_DIM

    def keys_values(sub):
        starts = [pl.multiple_of(jnp.clip(qb0 + sub + d, 0, n_blocks - 1) * BLOCK, BLOCK) for d in (-1, 0, 1)]
        k_all = jnp.concatenate([kp_ref[pl.ds(s0, BLOCK), :] for s0 in starts] + [kc_ref[...]], axis=0)
        v_all = jnp.concatenate([vp_ref[pl.ds(s0, BLOCK), :] for s0 in starts] + [vc_ref[...]], axis=0)
        return k_all, v_all

    kv = [keys_values(sub) for sub in range(n_sub)]
    slots_per_block = 2 * N_KV_HEADS

    def scores(slot):
        sub, rest = divmod(slot, slots_per_block)
        h, half = divmod(rest, 2)
        c0 = 2 * h * LANES
        q_rows = slice(sub * BLOCK, (sub + 1) * BLOCK)
        q_cat = jnp.concatenate([q_ref[q_rows, c0:c0 + LANES], q_ref[q_rows, c0 + LANES:c0 + 2 * LANES]], axis=0)
        keep = lo_q if half == 0 else jnp.logical_not(lo_q)
        q_sel = jnp.where(keep, q_cat, jnp.zeros_like(q_cat))
        k_h = kv[sub][0][:, h * LANES:(h + 1) * LANES]
        s_ref[slot] = lax.dot_general(q_sel, k_h, (((1,), (1,)), ((), ())), preferred_element_type=F32)

    def softmax(slot):
        sub, rest = divmod(slot, slots_per_block)
        h, half = divmod(rest, 2)
        has_left, has_right = qb0 + sub > 0, qb0 + sub < n_blocks - 1
        for r0 in range(0, rows2, ATTN_ROWS):
            rows = slice(r0, r0 + ATTN_ROWS)
            in_block = r0 % BLOCK
            sink = sink_ref[4 * h + half + 2 * (r0 // BLOCK)]
            left_ok = diff >= jnp.where(has_left, in_block, 2 * BLOCK)
            right_ok = diff <= jnp.where(has_right, in_block, -2 * BLOCK)
            parts = [jnp.where(left_ok, s_ref[slot, rows, :BLOCK], NEG_INF),
                     s_ref[slot, rows, BLOCK:2 * BLOCK],
                     jnp.where(right_ok, s_ref[slot, rows, 2 * BLOCK:SPAN], NEG_INF)]
            parts += [s_ref[slot, rows, c:c + BLOCK] for c in range(SPAN, n_keys, BLOCK)]
            m = functools.reduce(jnp.maximum, parts)
            m = jnp.maximum(jnp.max(m, axis=-1, keepdims=True), sink)
            p = [jnp.exp(t - m) for t in parts]
            denom = jnp.sum(functools.reduce(jnp.add, p), axis=-1, keepdims=True) + jnp.exp(sink - m)
            inv_ref[slot, rows, :] = jnp.broadcast_to(1.0 / denom, (ATTN_ROWS, LANES))
            p_ref[slot, rows, :] = jnp.concatenate(p, axis=1).astype(BF16)

    def weighted_values(pair):
        sub, h = divmod(pair, N_KV_HEADS)
        v_h = kv[sub][1][:, h * LANES:(h + 1) * LANES]
        v_lo = jnp.where(lo_kv, v_h, jnp.zeros_like(v_h))
        v_hi = jnp.where(lo_kv, jnp.zeros_like(v_h), v_h)
        acc = (jnp.dot(p_ref[2 * pair], v_lo, preferred_element_type=F32)
               + jnp.dot(p_ref[2 * pair + 1], v_hi, preferred_element_type=F32))
        out = (acc * jnp.where(lo_q, inv_ref[2 * pair], inv_ref[2 * pair + 1])).astype(BF16)
        c0 = 2 * h * LANES
        q_rows = slice(sub * BLOCK, (sub + 1) * BLOCK)
        o_ref[q_rows, c0:c0 + LANES] = out[:BLOCK]
        o_ref[q_rows, c0 + LANES:c0 + 2 * LANES] = out[BLOCK:]

    n_slots = n_sub * slots_per_block
    scores(0)
    for slot in range(n_slots):
        if slot + 1 < n_slots:
            scores(slot + 1)
        softmax(slot)
        if slot % 2 == 1:
            weighted_values(slot // 2)


def _attn_call(q, kp, vp, kc, vc, sink, tq):
    bsz, length, _ = q.shape
    n_blocks = length // BLOCK
    n_ctx = kc.shape[1]
    n_slots = (tq // BLOCK) * 2 * N_KV_HEADS
    per_batch = lambda rows: pl.BlockSpec((None, rows, KV_DUP_WIDTH), lambda bi, t, *_: (bi, 0, 0))
    return pl.pallas_call(
        functools.partial(_attn_kernel, n_blocks),
        out_shape=jax.ShapeDtypeStruct((bsz, length, Q_WIDTH), BF16),
        grid_spec=pltpu.PrefetchScalarGridSpec(
            num_scalar_prefetch=1,
            grid=(bsz, length // tq),
            in_specs=[
                pl.BlockSpec((None, tq, Q_WIDTH), lambda bi, t, *_: (bi, t, 0)),
                per_batch(length),
                per_batch(length),
                per_batch(n_ctx),
                per_batch(n_ctx),
            ],
            out_specs=pl.BlockSpec((None, tq, Q_WIDTH), lambda bi, t, *_: (bi, t, 0)),
            scratch_shapes=[
                pltpu.VMEM((n_slots, 2 * BLOCK, SPAN + n_ctx), F32),
                pltpu.VMEM((n_slots, 2 * BLOCK, SPAN + n_ctx), BF16),
                pltpu.VMEM((n_slots, 2 * BLOCK, LANES), F32),
            ],
        ),
        compiler_params=_params(2),
        name="attn",
    )(sink, q, kp, vp, kc, vc)


def _proj_norm_kernel(o_ref, x_ref, mod_ref, w_ref, g_ref, b_ref, out_ref):
    y = jnp.dot(o_ref[...], w_ref[...], preferred_element_type=F32)
    out_ref[...] = _post_norm(x_ref[...], y, mod_ref, 1, g_ref, b_ref)


def _proj_norm_call(o, x, mod, w, g, b, tm):
    bsz, length, _ = x.shape
    tok = pl.BlockSpec((None, tm, D_MODEL), lambda bi, t: (bi, t, 0))
    return pl.pallas_call(
        _proj_norm_kernel,
        out_shape=jax.ShapeDtypeStruct(x.shape, F32),
        grid=(bsz, length // tm),
        in_specs=[
            tok, tok,
            pl.BlockSpec((None, N_MOD, D_MODEL), lambda bi, t: (bi, 0, 0)),
            _resident((D_MODEL, D_MODEL)),
            _resident((1, D_MODEL)),
            _resident((1, D_MODEL)),
        ],
        out_specs=tok,
        compiler_params=_params(2),
        name="proj_norm",
    )(o, x, mod, w, g.reshape(1, D_MODEL), b.reshape(1, D_MODEL))


def _reverse_shift(t, first_row, flip_ref):
    rows = t.shape[0]
    blocks = [jnp.dot(flip_ref[...], t[r0:r0 + FLIP, :], preferred_element_type=F32)
              for r0 in range(rows - FLIP, -FLIP, -FLIP)]
    shifted = pltpu.roll(jnp.concatenate(blocks, axis=0), 1, 0)
    is_first = lax.broadcasted_iota(jnp.int32, shifted.shape, 0) == 0
    return jnp.where(is_first, first_row, shifted)


def _channel_dft(t, w_ref):
    return jnp.concatenate(
        [jnp.dot(t[:, g0:g0 + FOURIER_GROUP_CH], w_ref[...], preferred_element_type=F32)
         for g0 in range(0, D_MODEL, FOURIER_GROUP_CH)], axis=1)


def _dft_fold_kernel(lo_ref, up_ref, mod_ref, cc_ref, sc_ref, flip_ref, ab_ref, an_ref, carry_ref):
    @pl.when(pl.program_id(1) == 0)
    def _():
        carry_ref[...] = jnp.zeros_like(carry_ref)

    h_lo = _modulate(lo_ref[...], mod_ref, 1)
    h_up = _modulate(up_ref[...], mod_ref, 1).astype(BF16)
    partner = _reverse_shift(h_up, carry_ref[0:1, :], flip_ref)
    carry_ref[...] = h_up[0:8, :].astype(F32)
    ab_ref[0] = _channel_dft((h_lo + partner).astype(BF16), cc_ref).astype(BF16)
    ab_ref[1] = _channel_dft((h_lo - partner).astype(BF16), sc_ref).astype(BF16)
    an_ref[...] = _channel_dft(h_up[0:8, :], cc_ref)


def _dft_fold_call(x4, mod, cc, sc, flip, tm):
    bsz, _, half, _ = x4.shape
    n_t = half // tm
    ch = FOURIER_GROUP_CH
    return pl.pallas_call(
        _dft_fold_kernel,
        out_shape=(jax.ShapeDtypeStruct((bsz, 2, half, D_MODEL), BF16),
                   jax.ShapeDtypeStruct((bsz, 8, D_MODEL), F32)),
        grid=(bsz, n_t),
        in_specs=[
            pl.BlockSpec((None, None, tm, D_MODEL), lambda bi, t: (bi, 0, t, 0)),
            pl.BlockSpec((None, None, tm, D_MODEL), lambda bi, t: (bi, 1, n_t - 1 - t, 0)),
            pl.BlockSpec((None, N_MOD, D_MODEL), lambda bi, t: (bi, 0, 0)),
            _resident((ch, ch)),
            _resident((ch, ch)),
            _resident((FLIP, FLIP)),
        ],
        out_specs=(pl.BlockSpec((None, 2, tm, D_MODEL), lambda bi, t: (bi, 0, t, 0)),
                   pl.BlockSpec((None, 8, D_MODEL), lambda bi, t: (bi, 0, 0))),
        scratch_shapes=[pltpu.VMEM((8, D_MODEL), F32)],
        compiler_params=pltpu.CompilerParams(dimension_semantics=("parallel", "arbitrary"),
                                             vmem_limit_bytes=VMEM_LIMIT),
        name="dft_fold",
    )(x4, x4, mod, cc, sc, flip)


def _dft_seq_kernel(norm, n_t, wlo_ref, wup_ref, ab_ref, an_ref, alt_ref, flip_ref, x_ref, mod_ref, wo_ref,
                    g_ref, beta_ref, out_ref, carry_ref):
    t = pl.program_id(1)
    tk = wlo_ref.shape[0]
    ab = ab_ref[...]
    a_n = an_ref[0:1, :]
    row = lax.broadcasted_iota(jnp.int32, (tk, 1), 0)

    def alternating(k0):
        return (1 - 2 * ((k0 + row) & 1)).astype(F32)

    f_lo = (jnp.dot(wlo_ref[...], ab, preferred_element_type=F32) + alternating(t * tk) * a_n) * norm
    f_mir = (jnp.dot(wup_ref[...], ab, preferred_element_type=F32) + alternating((n_t - 1 - t) * tk) * a_n) * norm
    f_mir = f_mir.astype(BF16)

    @pl.when(t == 0)
    def _():
        f_n = (jnp.dot(alt_ref[...], ab, preferred_element_type=F32)[0:1, :] + a_n) * norm
        carry_ref[0:1, :] = f_n.astype(BF16).astype(F32)

    f_up = _reverse_shift(f_mir, carry_ref[0:1, :], flip_ref)
    carry_ref[...] = f_mir[0:8, :].astype(F32)
    y_lo = jnp.dot(f_lo.astype(BF16), wo_ref[...], preferred_element_type=F32)
    y_up = jnp.dot(f_up.astype(BF16), wo_ref[...], preferred_element_type=F32)
    out_ref[0] = _post_norm(x_ref[0], y_lo, mod_ref, 1, g_ref, beta_ref)
    out_ref[1] = _post_norm(x_ref[1], y_up, mod_ref, 1, g_ref, beta_ref)


def _dft_seq_call(w_lo, w_up, ab, a_n, alt, flip, x4, mod, wo, g, beta, tk):
    bsz, _, half, _ = x4.shape
    n_t = half // tk
    norm = float(1.0 / np.sqrt(2 * half * FOURIER_GROUP_CH))
    tok = pl.BlockSpec((None, 2, tk, D_MODEL), lambda bi, t: (bi, 0, t, 0))
    return pl.pallas_call(
        functools.partial(_dft_seq_kernel, norm, n_t),
        out_shape=jax.ShapeDtypeStruct(x4.shape, F32),
        grid=(bsz, n_t),
        in_specs=[
            pl.BlockSpec((tk, 2 * half), lambda bi, t: (t, 0)),
            pl.BlockSpec((tk, 2 * half), lambda bi, t: (n_t - 1 - t, 0)),
            pl.BlockSpec((None, 2 * half, D_MODEL), lambda bi, t: (bi, 0, 0)),
            pl.BlockSpec((None, 8, D_MODEL), lambda bi, t: (bi, 0, 0)),
            _resident((8, 2 * half)),
            _resident((FLIP, FLIP)),
            tok,
            pl.BlockSpec((None, N_MOD, D_MODEL), lambda bi, t: (bi, 0, 0)),
            _resident((D_MODEL, D_MODEL)),
            _resident((1, D_MODEL)),
            _resident((1, D_MODEL)),
        ],
        out_specs=tok,
        scratch_shapes=[pltpu.VMEM((8, D_MODEL), F32)],
        compiler_params=pltpu.CompilerParams(dimension_semantics=("parallel", "arbitrary"),
                                             vmem_limit_bytes=VMEM_LIMIT),
        name="dft_seq",
    )(w_lo, w_up, ab, a_n, alt, flip, x4, mod, wo, g.reshape(1, D_MODEL), beta.reshape(1, D_MODEL))


def _rope_tables(n_tokens):
    rows = n_tokens // GRID_W
    row = jnp.repeat(jnp.arange(rows), GRID_W).astype(F32)
    col = jnp.tile(jnp.arange(GRID_W), rows).astype(F32)
    inv = ROPE_BASE ** (-jnp.arange(0, AXIS_DIM, 2, dtype=F32) / AXIS_DIM)
    ang_r, ang_c = row[:, None] * inv, col[:, None] * inv
    cos_r, sin_r, cos_c, sin_c = jnp.cos(ang_r), jnp.sin(ang_r), jnp.cos(ang_c), jnp.sin(ang_c)
    zero = jnp.zeros_like(sin_r)
    cos_h = jnp.concatenate([cos_r, cos_r, cos_c, cos_c], axis=-1)
    sa_h = jnp.concatenate([-sin_r, zero, -sin_c, zero], axis=-1)
    sb_h = jnp.concatenate([zero, sin_r, zero, sin_c], axis=-1)
    rep = LANES // HEAD_DIM
    return tuple(jnp.tile(t, (1, rep)) for t in (cos_h, sa_h, sb_h))


def _dft_tables(n, rows):
    idx = jnp.arange(rows, dtype=jnp.int32)
    ang = ((idx[:, None] * idx[None, :]) % n).astype(F32) * (2.0 * np.pi / n)
    return jnp.cos(ang), jnp.sin(ang)


def kernel(x, c, ctx, c_ctx, mod_w, mod_b, ln_g, ln_b, ffn_wi, ffn_wo,
           attn_wqkv, attn_bqkv, attn_wo, attn_sink, fourier_wo):
    bsz, seq, _ = x.shape
    n_ctx = ctx.shape[1]
    assert DEPTH == 2 and seq % BLOCK == 0 and seq % GRID_W == 0

    cv = jnp.concatenate([c, c_ctx[None, :], jnp.zeros((MOD_ROWS - bsz - 1, D_MODEL), F32)], axis=0)
    mod = _mod_call(cv, mod_w, mod_b).reshape(DEPTH, MOD_ROWS, N_MOD, D_MODEL)
    wi = ffn_wi.astype(BF16)
    wo = ffn_wo.astype(BF16)

    mod_lat, mod_ctx = mod[0, :bsz], mod[0, bsz:bsz + 1]
    g, b = ln_g[0], ln_b[0]
    x = _ffn_call(x, mod_lat, 0, wi, wo, 0, 0, g[0], b[0], tm=512)
    ctx_s = _ffn_call(ctx.reshape(1, bsz * n_ctx, D_MODEL), mod_ctx, 0, wi, wo, 0, 0, g[0], b[0], tm=512)

    w_qkv = attn_wqkv[0].astype(BF16)
    cos_t, sa_t, sb_t = _rope_tables(seq)
    q, k, v = _qkv_call(x, mod_lat, w_qkv, attn_bqkv[0], cos_t, sa_t, sb_t, tm=512)
    kc, vc = _kv_ctx_call(ctx_s, mod_ctx, w_qkv[:, Q_WIDTH:], attn_bqkv[0, Q_WIDTH:], tm=512)
    kc = kc.reshape(bsz, n_ctx, KV_DUP_WIDTH)
    vc = vc.reshape(bsz, n_ctx, KV_DUP_WIDTH)
    o = _attn_call(q, k, v, kc, vc, attn_sink[0], tq=2 * BLOCK)
    x = _proj_norm_call(o, x, mod_lat, attn_wo[0].astype(BF16), g[1], b[1], tm=512)
    x = _ffn_call(x, mod_lat, 2, wi, wo, 0, 1, g[2], b[2], tm=512)

    mod_lat = mod[1, :bsz]
    g, b = ln_g[1], ln_b[1]
    x = _ffn_call(x, mod_lat, 0, wi, wo, 1, 0, g[0], b[0], tm=512)
    half = seq // 2
    cos_c, sin_c = _dft_tables(FOURIER_GROUP_CH, FOURIER_GROUP_CH)
    cos_s, sin_s = _dft_tables(seq, half)
    w_lo = jnp.concatenate([cos_s, -sin_s], axis=1).astype(BF16)
    w_up = jnp.concatenate([cos_s, sin_s], axis=1).astype(BF16)
    alt = jnp.zeros((8, seq), F32).at[0, :half].set(1.0 - 2.0 * (jnp.arange(half) % 2)).astype(BF16)
    anti = jnp.arange(FLIP)[:, None] + jnp.arange(FLIP)[None, :] == FLIP - 1
    flip = anti.astype(BF16)
    x4 = x.reshape(bsz, 2, half, D_MODEL)
    ab, a_n = _dft_fold_call(x4, mod_lat, cos_c.astype(BF16), sin_c.astype(BF16), flip, tm=512)
    x4 = _dft_seq_call(w_lo, w_up, ab.reshape(bsz, seq, D_MODEL), a_n, alt, flip, x4, mod_lat,
                       fourier_wo[0].astype(BF16), g[1], b[1], tk=512)
    x = x4.reshape(bsz, seq, D_MODEL)
    x = _ffn_call(x, mod_lat, 2, wi, wo, 1, 1, g[2], b[2], tm=512)
    return x
```

```python
import functools

import jax
import jax.numpy as jnp
import numpy as np
from jax import lax
from jax.experimental import pallas as pl
from jax.experimental.pallas import tpu as pltpu

D_MODEL = 1024
DEPTH = 2
GRID_W = 64
N_HEADS = 16
N_KV_HEADS = 4
HEAD_DIM = 64
GROUP = N_HEADS // N_KV_HEADS
Q_WIDTH = N_HEADS * HEAD_DIM
KV_WIDTH = N_KV_HEADS * HEAD_DIM
QKV_WIDTH = Q_WIDTH + 2 * KV_WIDTH
KV_DUP_WIDTH = 2 * KV_WIDTH
WINDOW = 128
BLOCK = 128
SPAN = BLOCK + 2 * WINDOW
ROPE_BASE = 10000.0
AXIS_DIM = HEAD_DIM // 2
FOURIER_GROUPS = 4
FOURIER_GROUP_CH = D_MODEL // FOURIER_GROUPS
D_FF = 2816
N_MOD = 9
LN_EPS = 1e-5
ALPHA = (2.0 * DEPTH) ** 0.25
NEG_INF = -1e30

LANES = 128
MOD_ROWS = 24
VMEM_LIMIT = 60 * 1024 * 1024
FFN_TILE = 1024
FFN_CHUNKS = (512,) * 5 + (256,)
ATTN_ROWS = 32
FLIP = 256

BF16 = jnp.bfloat16
F32 = jnp.float32


def _params(n_axes):
    return pltpu.CompilerParams(dimension_semantics=("parallel",) * n_axes,
                                vmem_limit_bytes=VMEM_LIMIT)


def _resident(shape):
    return pl.BlockSpec(shape, lambda *_: (0,) * len(shape), pipeline_mode=pl.Buffered(1))


def _layer_norm(z, g, b):
    mu = jnp.mean(z, axis=-1, keepdims=True)
    d = z - mu
    var = jnp.mean(d * d, axis=-1, keepdims=True)
    return d * lax.rsqrt(var + LN_EPS) * g + b


def _modulate(x, mod_ref, j):
    shift = mod_ref[3 * j:3 * j + 1, :]
    scale = mod_ref[3 * j + 1:3 * j + 2, :]
    return x * (1.0 + scale) + shift


def _post_norm(x, y, mod_ref, j, g_ref, b_ref):
    gate = mod_ref[3 * j + 2:3 * j + 3, :]
    return _layer_norm(ALPHA * x + gate * y, g_ref[...], b_ref[...])


def _mod_kernel(cv_ref, w_ref, b_ref, o_ref):
    cv = cv_ref[...]
    s = (cv * jax.nn.sigmoid(cv)).astype(BF16)
    o_ref[...] = jnp.dot(s, w_ref[...].astype(BF16), preferred_element_type=F32) + b_ref[...]


def _mod_call(cv, mod_w, mod_b):
    tn = D_MODEL
    n_out = N_MOD * D_MODEL
    return pl.pallas_call(
        _mod_kernel,
        out_shape=jax.ShapeDtypeStruct((DEPTH, MOD_ROWS, n_out), F32),
        grid=(DEPTH, n_out // tn),
        in_specs=[
            pl.BlockSpec((MOD_ROWS, D_MODEL), lambda i, n: (0, 0)),
            pl.BlockSpec((None, D_MODEL, tn), lambda i, n: (i, 0, n)),
            pl.BlockSpec((None, 1, tn), lambda i, n: (i, 0, n)),
        ],
        out_specs=pl.BlockSpec((None, MOD_ROWS, tn), lambda i, n: (i, 0, n)),
        compiler_params=_params(2),
        name="mod",
    )(cv, mod_w, mod_b.reshape(DEPTH, 1, n_out))


def _ffn_kernel(j, n_tiles, x_ref, mod_ref, wi_ref, wo_ref, g_ref, b_ref, o_ref, u_ref, z_ref):
    i = pl.program_id(0)

    @pl.when(i == 0)
    def _():
        z_ref[...] = jnp.zeros_like(z_ref)

    def norm_previous():
        o_ref[...] = _layer_norm(z_ref[...], g_ref[...], b_ref[...])

    @pl.when(i < n_tiles)
    def _():
        norm_previous()
        h = _modulate(x_ref[...], mod_ref, j).astype(BF16)
        start = 0
        for width in FFN_CHUNKS:
            a = jnp.dot(h, wi_ref[:, start:start + width], preferred_element_type=F32)
            g = jnp.dot(h, wi_ref[:, D_FF + start:D_FF + start + width], preferred_element_type=F32)
            u_ref[:, start:start + width] = (a * (g * jax.nn.sigmoid(g))).astype(BF16)
            start += width
        y = 0.5 * jnp.dot(u_ref[...], wo_ref[...], preferred_element_type=F32)
        z_ref[...] = ALPHA * x_ref[...] + mod_ref[3 * j + 2:3 * j + 3, :] * y

    @pl.when(i == n_tiles)
    def _():
        norm_previous()


def _ffn_call(x, mod, j, wi_all, wo_all, layer, slot, g, b, tm):
    bsz, length, _ = x.shape
    per_batch = length // tm
    n_tiles = bsz * per_batch
    cur = lambda i: jnp.minimum(i, n_tiles - 1)
    prev = lambda i: jnp.maximum(i - 1, 0)
    tile = lambda pick: pl.BlockSpec((None, tm, D_MODEL), lambda i: (pick(i) // per_batch, pick(i) % per_batch, 0))
    weight = lambda rows, cols: pl.BlockSpec((None, None, rows, cols), lambda i: (layer, slot, 0, 0),
                                             pipeline_mode=pl.Buffered(1))
    return pl.pallas_call(
        functools.partial(_ffn_kernel, j, n_tiles),
        out_shape=jax.ShapeDtypeStruct(x.shape, F32),
        grid=(n_tiles + 1,),
        in_specs=[
            tile(cur),
            pl.BlockSpec((None, N_MOD, D_MODEL), lambda i: (cur(i) // per_batch, 0, 0)),
            weight(D_MODEL, 2 * D_FF),
            weight(D_FF, D_MODEL),
            _resident((1, D_MODEL)),
            _resident((1, D_MODEL)),
        ],
        out_specs=tile(prev),
        scratch_shapes=[pltpu.VMEM((tm, D_FF), BF16), pltpu.VMEM((tm, D_MODEL), F32)],
        compiler_params=pltpu.CompilerParams(dimension_semantics=("arbitrary",), vmem_limit_bytes=VMEM_LIMIT),
        name="ffn",
    )(x, mod, wi_all, wo_all, g.reshape(1, D_MODEL), b.reshape(1, D_MODEL))


def _rope(t, cos_ref, sa_ref, sb_ref):
    half = AXIS_DIM // 2
    outs = []
    for c in range(t.shape[1] // LANES):
        tc = t[:, c * LANES:(c + 1) * LANES]
        up = pltpu.roll(tc, LANES - half, 1)
        down = pltpu.roll(tc, half, 1)
        outs.append(tc * cos_ref[...] + up * sa_ref[...] + down * sb_ref[...])
    return jnp.concatenate(outs, axis=1)


def _dup_heads(t):
    pieces = []
    for h in range(t.shape[1] // HEAD_DIM):
        head = t[:, h * HEAD_DIM:(h + 1) * HEAD_DIM]
        pieces += [head, head]
    return jnp.concatenate(pieces, axis=1)


def _qkv_kernel(x_ref, mod_ref, w_ref, b_ref, cos_ref, sa_ref, sb_ref, q_ref, k_ref, v_ref):
    h = _modulate(x_ref[...], mod_ref, 1).astype(BF16)
    qkv = jnp.dot(h, w_ref[...], preferred_element_type=F32) + b_ref[...]
    q = qkv[:, :Q_WIDTH] * (HEAD_DIM ** -0.5)
    q_ref[...] = _rope(q, cos_ref, sa_ref, sb_ref).astype(BF16)
    k = _rope(qkv[:, Q_WIDTH:Q_WIDTH + KV_WIDTH], cos_ref, sa_ref, sb_ref)
    k_ref[...] = _dup_heads(k).astype(BF16)
    v_ref[...] = _dup_heads(qkv[:, Q_WIDTH + KV_WIDTH:]).astype(BF16)


def _qkv_call(x, mod, w, b, cos_t, sa_t, sb_t, tm):
    bsz, length, _ = x.shape
    tok = lambda width: pl.BlockSpec((None, tm, width), lambda bi, t: (bi, t, 0))
    table = pl.BlockSpec((tm, LANES), lambda bi, t: (t, 0))
    return pl.pallas_call(
        _qkv_kernel,
        out_shape=(jax.ShapeDtypeStruct((bsz, length, Q_WIDTH), BF16),
                   jax.ShapeDtypeStruct((bsz, length, KV_DUP_WIDTH), BF16),
                   jax.ShapeDtypeStruct((bsz, length, KV_DUP_WIDTH), BF16)),
        grid=(bsz, length // tm),
        in_specs=[
            tok(D_MODEL),
            pl.BlockSpec((None, N_MOD, D_MODEL), lambda bi, t: (bi, 0, 0)),
            _resident((D_MODEL, QKV_WIDTH)),
            _resident((1, QKV_WIDTH)),
            table, table, table,
        ],
        out_specs=(tok(Q_WIDTH), tok(KV_DUP_WIDTH), tok(KV_DUP_WIDTH)),
        compiler_params=_params(2),
        name="qkv",
    )(x, mod, w, b.reshape(1, QKV_WIDTH), cos_t, sa_t, sb_t)


def _kv_ctx_kernel(x_ref, mod_ref, w_ref, b_ref, k_ref, v_ref):
    h = _modulate(x_ref[...], mod_ref, 1).astype(BF16)
    kv = jnp.dot(h, w_ref[...], preferred_element_type=F32) + b_ref[...]
    k_ref[...] = _dup_heads(kv[:, :KV_WIDTH]).astype(BF16)
    v_ref[...] = _dup_heads(kv[:, KV_WIDTH:]).astype(BF16)


def _kv_ctx_call(x, mod, w, b, tm):
    bsz, length, _ = x.shape
    tok = lambda width: pl.BlockSpec((None, tm, width), lambda bi, t: (bi, t, 0))
    return pl.pallas_call(
        _kv_ctx_kernel,
        out_shape=(jax.ShapeDtypeStruct((bsz, length, KV_DUP_WIDTH), BF16),
                   jax.ShapeDtypeStruct((bsz, length, KV_DUP_WIDTH), BF16)),
        grid=(bsz, length // tm),
        in_specs=[
            tok(D_MODEL),
            pl.BlockSpec((None, N_MOD, D_MODEL), lambda bi, t: (bi, 0, 0)),
            _resident((D_MODEL, 2 * KV_WIDTH)),
            _resident((1, 2 * KV_WIDTH)),
        ],
        out_specs=(tok(KV_DUP_WIDTH), tok(KV_DUP_WIDTH)),
        compiler_params=_params(2),
        name="kv_ctx",
    )(x, mod, w, b.reshape(1, 2 * KV_WIDTH))


def _attn_kernel(n_blocks, n_steps, sink_ref, q_ref, kp_ref, vp_ref, kc_ref, vc_ref, xp_ref, modp_ref, wo_ref,
                 g_ref, b_ref, out_ref, o_ref, s_ref, p_ref, inv_ref):
    i = pl.program_id(0)

    @pl.when(i == 0)
    def _():
        o_ref[...] = jnp.zeros_like(o_ref)

    def project(o_prev):
        return jnp.dot(o_prev, wo_ref[...], preferred_element_type=F32)

    def norm(y):
        out_ref[...] = _post_norm(xp_ref[...], y, modp_ref, 1, g_ref, b_ref)

    @pl.when(i < n_steps)
    def _():
        o_prev = o_ref[...]
        y = []
        tile_in_batch = i % (n_blocks * BLOCK // q_ref.shape[0])
        _attend(n_blocks, tile_in_batch, sink_ref, q_ref, kp_ref, vp_ref, kc_ref, vc_ref, o_ref, s_ref, p_ref, inv_ref,
                after_slot={3: lambda: y.append(project(o_prev)), 9: lambda: norm(y[0])})

    @pl.when(i == n_steps)
    def _():
        norm(project(o_ref[...]))


def _attend(n_blocks, tile_in_batch, sink_ref, q_ref, kp_ref, vp_ref, kc_ref, vc_ref, o_ref, s_ref, p_ref, inv_ref,
            after_slot):
    n_sub = q_ref.shape[0] // BLOCK
    qb0 = tile_in_batch * n_sub
    rows2 = 2 * BLOCK
    n_keys = SPAN + kc_ref.shape[0]
    diff = (lax.broadcasted_iota(jnp.int32, (ATTN_ROWS, BLOCK), 1)
            - lax.broadcasted_iota(jnp.int32, (ATTN_ROWS, BLOCK), 0))
    lo_q = lax.broadcasted_iota(jnp.int32, (rows2, LANES), 1) < HEAD_DIM
    lo_kv = lax.broadcasted_iota(jnp.int32, (n_keys, LANES), 1) < HEAD_DIM

    def keys_values(sub):
        starts = [pl.multiple_of(jnp.clip(qb0 + sub + d, 0, n_blocks - 1) * BLOCK, BLOCK) for d in (-1, 0, 1)]
        k_all = jnp.concatenate([kp_ref[pl.ds(s0, BLOCK), :] for s0 in starts] + [kc_ref[...]], axis=0)
        v_all = jnp.concatenate([vp_ref[pl.ds(s0, BLOCK), :] for s0 in starts] + [vc_ref[...]], axis=0)
        return k_all, v_all

    kv = [keys_values(sub) for sub in range(n_sub)]
    slots_per_block = 2 * N_KV_HEADS

    def scores(slot):
        sub, rest = divmod(slot, slots_per_block)
        h, half = divmod(rest, 2)
        c0 = 2 * h * LANES
        q_rows = slice(sub * BLOCK, (sub + 1) * BLOCK)
        q_cat = jnp.concatenate([q_ref[q_rows, c0:c0 + LANES], q_ref[q_rows, c0 + LANES:c0 + 2 * LANES]], axis=0)
        keep = lo_q if half == 0 else jnp.logical_not(lo_q)
        q_sel = jnp.where(keep, q_cat, jnp.zeros_like(q_cat))
        k_h = kv[sub][0][:, h * LANES:(h + 1) * LANES]
        s_ref[slot] = lax.dot_general(q_sel, k_h, (((1,), (1,)), ((), ())), preferred_element_type=F32)

    def softmax(slot):
        sub, rest = divmod(slot, slots_per_block)
        h, half = divmod(rest, 2)
        has_left, has_right = qb0 + sub > 0, qb0 + sub < n_blocks - 1
        for r0 in range(0, rows2, ATTN_ROWS):
            rows = slice(r0, r0 + ATTN_ROWS)
            in_block = r0 % BLOCK
            sink = sink_ref[4 * h + half + 2 * (r0 // BLOCK)]
            left_ok = diff >= jnp.where(has_left, in_block, 2 * BLOCK)
            right_ok = diff <= jnp.where(has_right, in_block, -2 * BLOCK)
            parts = [jnp.where(left_ok, s_ref[slot, rows, :BLOCK], NEG_INF),
                     s_ref[slot, rows, BLOCK:2 * BLOCK],
                     jnp.where(right_ok, s_ref[slot, rows, 2 * BLOCK:SPAN], NEG_INF)]
            parts += [s_ref[slot, rows, c:c + BLOCK] for c in range(SPAN, n_keys, BLOCK)]
            m = functools.reduce(jnp.maximum, parts)
            m = jnp.maximum(jnp.max(m, axis=-1, keepdims=True), sink)
            p = [jnp.exp(t - m) for t in parts]
            denom = jnp.sum(functools.reduce(jnp.add, p), axis=-1, keepdims=True) + jnp.exp(sink - m)
            inv_ref[slot, rows, :] = jnp.broadcast_to(1.0 / denom, (ATTN_ROWS, LANES))
            p_ref[slot, rows, :] = jnp.concatenate(p, axis=1).astype(BF16)

    def weighted_values(pair):
        sub, h = divmod(pair, N_KV_HEADS)
        v_h = kv[sub][1][:, h * LANES:(h + 1) * LANES]
        v_lo = jnp.where(lo_kv, v_h, jnp.zeros_like(v_h))
        v_hi = jnp.where(lo_kv, jnp.zeros_like(v_h), v_h)
        acc = (jnp.dot(p_ref[2 * pair], v_lo, preferred_element_type=F32)
               + jnp.dot(p_ref[2 * pair + 1], v_hi, preferred_element_type=F32))
        out = (acc * jnp.where(lo_q, inv_ref[2 * pair], inv_ref[2 * pair + 1])).astype(BF16)
        c0 = 2 * h * LANES
        q_rows = slice(sub * BLOCK, (sub + 1) * BLOCK)
        o_ref[q_rows, c0:c0 + LANES] = out[:BLOCK]
        o_ref[q_rows, c0 + LANES:c0 + 2 * LANES] = out[BLOCK:]

    n_slots = n_sub * slots_per_block
    scores(0)
    for slot in range(n_slots):
        if slot + 1 < n_slots:
            scores(slot + 1)
        softmax(slot)
        if slot % 2 == 1:
            weighted_values(slot // 2)
        if slot in after_slot:
            after_slot[slot]()


def _attn_call(q, kp, vp, kc, vc, sink, x, mod, wo, g, b, tq):
    bsz, length, _ = q.shape
    n_blocks = length // BLOCK
    n_ctx = kc.shape[1]
    per_batch_tiles = length // tq
    n_steps = bsz * per_batch_tiles
    n_slots = (tq // BLOCK) * 2 * N_KV_HEADS
    cur = lambda i: jnp.minimum(i, n_steps - 1)
    prev = lambda i: jnp.maximum(i - 1, 0)
    tile = lambda pick: pl.BlockSpec(
        (None, tq, D_MODEL), lambda i, *_: (pick(i) // per_batch_tiles, pick(i) % per_batch_tiles, 0))
    per_batch = lambda rows: pl.BlockSpec(
        (None, rows, KV_DUP_WIDTH), lambda i, *_: (cur(i) // per_batch_tiles, 0, 0))
    return pl.pallas_call(
        functools.partial(_attn_kernel, n_blocks, n_steps),
        out_shape=jax.ShapeDtypeStruct(x.shape, F32),
        grid_spec=pltpu.PrefetchScalarGridSpec(
            num_scalar_prefetch=1,
            grid=(n_steps + 1,),
            in_specs=[
                tile(cur),
                per_batch(length),
                per_batch(length),
                per_batch(n_ctx),
                per_batch(n_ctx),
                tile(prev),
                pl.BlockSpec((None, N_MOD, D_MODEL), lambda i, *_: (prev(i) // per_batch_tiles, 0, 0)),
                _resident((D_MODEL, D_MODEL)),
                _resident((1, D_MODEL)),
                _resident((1, D_MODEL)),
            ],
            out_specs=tile(prev),
            scratch_shapes=[
                pltpu.VMEM((tq, Q_WIDTH), BF16),
                pltpu.VMEM((n_slots, 2 * BLOCK, SPAN + n_ctx), F32),
                pltpu.VMEM((n_slots, 2 * BLOCK, SPAN + n_ctx), BF16),
                pltpu.VMEM((n_slots, 2 * BLOCK, LANES), F32),
            ],
        ),
        compiler_params=pltpu.CompilerParams(dimension_semantics=("arbitrary",), vmem_limit_bytes=VMEM_LIMIT),
        name="attn",
    )(sink, q, kp, vp, kc, vc, x, mod, wo, g.reshape(1, D_MODEL), b.reshape(1, D_MODEL))


def _reverse_shift(t, first_row, flip_ref):
    rows = t.shape[0]
    blocks = [jnp.dot(flip_ref[...], t[r0:r0 + FLIP, :], preferred_element_type=F32)
              for r0 in range(rows - FLIP, -FLIP, -FLIP)]
    shifted = pltpu.roll(jnp.concatenate(blocks, axis=0), 1, 0)
    is_first = lax.broadcasted_iota(jnp.int32, shifted.shape, 0) == 0
    return jnp.where(is_first, first_row, shifted)


def _channel_dft(t, w_ref):
    return jnp.concatenate(
        [jnp.dot(t[:, g0:g0 + FOURIER_GROUP_CH], w_ref[...], preferred_element_type=F32)
         for g0 in range(0, D_MODEL, FOURIER_GROUP_CH)], axis=1)


def _dft_fold_kernel(lo_ref, up_ref, mod_ref, cc_ref, sc_ref, flip_ref, ab_ref, an_ref, carry_ref):
    @pl.when(pl.program_id(1) == 0)
    def _():
        carry_ref[...] = jnp.zeros_like(carry_ref)

    h_lo = _modulate(lo_ref[...], mod_ref, 1)
    h_up = _modulate(up_ref[...], mod_ref, 1).astype(BF16)
    partner = _reverse_shift(h_up, carry_ref[0:1, :], flip_ref)
    carry_ref[...] = h_up[0:8, :].astype(F32)
    ab_ref[0] = _channel_dft((h_lo + partner).astype(BF16), cc_ref).astype(BF16)
    ab_ref[1] = _channel_dft((h_lo - partner).astype(BF16), sc_ref).astype(BF16)
    an_ref[...] = _channel_dft(h_up[0:8, :], cc_ref)


def _dft_fold_call(x4, mod, cc, sc, flip, tm):
    bsz, _, half, _ = x4.shape
    n_t = half // tm
    ch = FOURIER_GROUP_CH
    return pl.pallas_call(
        _dft_fold_kernel,
        out_shape=(jax.ShapeDtypeStruct((bsz, 2, half, D_MODEL), BF16),
                   jax.ShapeDtypeStruct((bsz, 8, D_MODEL), F32)),
        grid=(bsz, n_t),
        in_specs=[
            pl.BlockSpec((None, None, tm, D_MODEL), lambda bi, t: (bi, 0, t, 0)),
            pl.BlockSpec((None, None, tm, D_MODEL), lambda bi, t: (bi, 1, n_t - 1 - t, 0)),
            pl.BlockSpec((None, N_MOD, D_MODEL), lambda bi, t: (bi, 0, 0)),
            _resident((ch, ch)),
            _resident((ch, ch)),
            _resident((FLIP, FLIP)),
        ],
        out_specs=(pl.BlockSpec((None, 2, tm, D_MODEL), lambda bi, t: (bi, 0, t, 0)),
                   pl.BlockSpec((None, 8, D_MODEL), lambda bi, t: (bi, 0, 0))),
        scratch_shapes=[pltpu.VMEM((8, D_MODEL), F32)],
        compiler_params=pltpu.CompilerParams(dimension_semantics=("parallel", "arbitrary"),
                                             vmem_limit_bytes=VMEM_LIMIT),
        name="dft_fold",
    )(x4, x4, mod, cc, sc, flip)


def _dft_seq_kernel(norm, n_t, wlo_ref, wup_ref, ab_ref, an_ref, alt_ref, flip_ref, x_ref, mod_ref, wo_ref,
                    g_ref, beta_ref, out_ref, carry_ref):
    t = pl.program_id(1)
    tk = wlo_ref.shape[0]
    ab = ab_ref[...]
    a_n = an_ref[0:1, :]
    row = lax.broadcasted_iota(jnp.int32, (tk, 1), 0)

    def alternating(k0):
        return (1 - 2 * ((k0 + row) & 1)).astype(F32)

    f_lo = (jnp.dot(wlo_ref[...], ab, preferred_element_type=F32) + alternating(t * tk) * a_n) * norm
    f_mir = (jnp.dot(wup_ref[...], ab, preferred_element_type=F32) + alternating((n_t - 1 - t) * tk) * a_n) * norm
    f_mir = f_mir.astype(BF16)

    @pl.when(t == 0)
    def _():
        f_n = (jnp.dot(alt_ref[...], ab, preferred_element_type=F32)[0:1, :] + a_n) * norm
        carry_ref[0:1, :] = f_n.astype(BF16).astype(F32)

    f_up = _reverse_shift(f_mir, carry_ref[0:1, :], flip_ref)
    carry_ref[...] = f_mir[0:8, :].astype(F32)
    y_lo = jnp.dot(f_lo.astype(BF16), wo_ref[...], preferred_element_type=F32)
    y_up = jnp.dot(f_up.astype(BF16), wo_ref[...], preferred_element_type=F32)
    out_ref[0] = _post_norm(x_ref[0], y_lo, mod_ref, 1, g_ref, beta_ref)
    out_ref[1] = _post_norm(x_ref[1], y_up, mod_ref, 1, g_ref, beta_ref)


def _dft_seq_call(w_lo, w_up, ab, a_n, alt, flip, x4, mod, wo, g, beta, tk):
    bsz, _, half, _ = x4.shape
    n_t = half // tk
    norm = float(1.0 / np.sqrt(2 * half * FOURIER_GROUP_CH))
    tok = pl.BlockSpec((None, 2, tk, D_MODEL), lambda bi, t: (bi, 0, t, 0))
    return pl.pallas_call(
        functools.partial(_dft_seq_kernel, norm, n_t),
        out_shape=jax.ShapeDtypeStruct(x4.shape, F32),
        grid=(bsz, n_t),
        in_specs=[
            pl.BlockSpec((tk, 2 * half), lambda bi, t: (t, 0)),
            pl.BlockSpec((tk, 2 * half), lambda bi, t: (n_t - 1 - t, 0)),
            pl.BlockSpec((None, 2 * half, D_MODEL), lambda bi, t: (bi, 0, 0)),
            pl.BlockSpec((None, 8, D_MODEL), lambda bi, t: (bi, 0, 0)),
            _resident((8, 2 * half)),
            _resident((FLIP, FLIP)),
            tok,
            pl.BlockSpec((None, N_MOD, D_MODEL), lambda bi, t: (bi, 0, 0)),
            _resident((D_MODEL, D_MODEL)),
            _resident((1, D_MODEL)),
            _resident((1, D_MODEL)),
        ],
        out_specs=tok,
        scratch_shapes=[pltpu.VMEM((8, D_MODEL), F32)],
        compiler_params=pltpu.CompilerParams(dimension_semantics=("parallel", "arbitrary"),
                                             vmem_limit_bytes=VMEM_LIMIT),
        name="dft_seq",
    )(w_lo, w_up, ab, a_n, alt, flip, x4, mod, wo, g.reshape(1, D_MODEL), beta.reshape(1, D_MODEL))


def _rope_tables(n_tokens):
    rows = n_tokens // GRID_W
    row = jnp.repeat(jnp.arange(rows), GRID_W).astype(F32)
    col = jnp.tile(jnp.arange(GRID_W), rows).astype(F32)
    inv = ROPE_BASE ** (-jnp.arange(0, AXIS_DIM, 2, dtype=F32) / AXIS_DIM)
    ang_r, ang_c = row[:, None] * inv, col[:, None] * inv
    cos_r, sin_r, cos_c, sin_c = jnp.cos(ang_r), jnp.sin(ang_r), jnp.cos(ang_c), jnp.sin(ang_c)
    zero = jnp.zeros_like(sin_r)
    cos_h = jnp.concatenate([cos_r, cos_r, cos_c, cos_c], axis=-1)
    sa_h = jnp.concatenate([-sin_r, zero, -sin_c, zero], axis=-1)
    sb_h = jnp.concatenate([zero, sin_r, zero, sin_c], axis=-1)
    rep = LANES // HEAD_DIM
    return tuple(jnp.tile(t, (1, rep)) for t in (cos_h, sa_h, sb_h))


def _dft_tables(n, rows):
    idx = jnp.arange(rows, dtype=jnp.int32)
    ang = ((idx[:, None] * idx[None, :]) % n).astype(F32) * (2.0 * np.pi / n)
    return jnp.cos(ang), jnp.sin(ang)


def kernel(x, c, ctx, c_ctx, mod_w, mod_b, ln_g, ln_b, ffn_wi, ffn_wo,
           attn_wqkv, attn_bqkv, attn_wo, attn_sink, fourier_wo):
    bsz, seq, _ = x.shape
    n_ctx = ctx.shape[1]
    assert DEPTH == 2 and seq % BLOCK == 0 and seq % GRID_W == 0

    cv = jnp.concatenate([c, c_ctx[None, :], jnp.zeros((MOD_ROWS - bsz - 1, D_MODEL), F32)], axis=0)
    mod = _mod_call(cv, mod_w, mod_b).reshape(DEPTH, MOD_ROWS, N_MOD, D_MODEL)
    wi = ffn_wi.astype(BF16)
    wo = ffn_wo.astype(BF16)

    mod_lat, mod_ctx = mod[0, :bsz], mod[0, bsz:bsz + 1]
    g, b = ln_g[0], ln_b[0]
    x = _ffn_call(x, mod_lat, 0, wi, wo, 0, 0, g[0], b[0], tm=FFN_TILE)
    ctx_s = _ffn_call(ctx.reshape(1, bsz * n_ctx, D_MODEL), mod_ctx, 0, wi, wo, 0, 0, g[0], b[0], tm=FFN_TILE)

    w_qkv = attn_wqkv[0].astype(BF16)
    cos_t, sa_t, sb_t = _rope_tables(seq)
    q, k, v = _qkv_call(x, mod_lat, w_qkv, attn_bqkv[0], cos_t, sa_t, sb_t, tm=512)
    kc, vc = _kv_ctx_call(ctx_s, mod_ctx, w_qkv[:, Q_WIDTH:], attn_bqkv[0, Q_WIDTH:], tm=512)
    kc = kc.reshape(bsz, n_ctx, KV_DUP_WIDTH)
    vc = vc.reshape(bsz, n_ctx, KV_DUP_WIDTH)
    x = _attn_call(q, k, v, kc, vc, attn_sink[0], x, mod_lat, attn_wo[0].astype(BF16), g[1], b[1], tq=2 * BLOCK)
    x = _ffn_call(x, mod_lat, 2, wi, wo, 0, 1, g[2], b[2], tm=FFN_TILE)

    mod_lat = mod[1, :bsz]
    g, b = ln_g[1], ln_b[1]
    x = _ffn_call(x, mod_lat, 0, wi, wo, 1, 0, g[0], b[0], tm=FFN_TILE)
    half = seq // 2
    cos_c, sin_c = _dft_tables(FOURIER_GROUP_CH, FOURIER_GROUP_CH)
    cos_s, sin_s = _dft_tables(seq, half)
    w_lo = jnp.concatenate([cos_s, -sin_s], axis=1).astype(BF16)
    w_up = jnp.concatenate([cos_s, sin_s], axis=1).astype(BF16)
    alt = jnp.zeros((8, seq), F32).at[0, :half].set(1.0 - 2.0 * (jnp.arange(half) % 2)).astype(BF16)
    anti = jnp.arange(FLIP)[:, None] + jnp.arange(FLIP)[None, :] == FLIP - 1
    flip = anti.astype(BF16)
    x4 = x.reshape(bsz, 2, half, D_MODEL)
    ab, a_n = _dft_fold_call(x4, mod_lat, cos_c.astype(BF16), sin_c.astype(BF16), flip, tm=512)
    x4 = _dft_seq_call(w_lo, w_up, ab.reshape(bsz, seq, D_MODEL), a_n, alt, flip, x4, mod_lat,
                       fourier_wo[0].astype(BF16), g[1], b[1], tk=512)
    x = x4.reshape(bsz, seq, D_MODEL)
    x = _ffn_call(x, mod_lat, 2, wi, wo, 1, 1, g[2], b[2], tm=FFN_TILE)
    return x
```

```python
import functools

import jax
import jax.numpy as jnp
import numpy as np
from jax import lax
from jax.experimental import pallas as pl
from jax.experimental.pallas import tpu as pltpu

D_MODEL = 1024
DEPTH = 2
GRID_W = 64
N_HEADS = 16
N_KV_HEADS = 4
HEAD_DIM = 64
GROUP = N_HEADS // N_KV_HEADS
Q_WIDTH = N_HEADS * HEAD_DIM
KV_WIDTH = N_KV_HEADS * HEAD_DIM
QKV_WIDTH = Q_WIDTH + 2 * KV_WIDTH
KV_DUP_WIDTH = 2 * KV_WIDTH
WINDOW = 128
BLOCK = 128
SPAN = BLOCK + 2 * WINDOW
ROPE_BASE = 10000.0
AXIS_DIM = HEAD_DIM // 2
FOURIER_GROUPS = 4
FOURIER_GROUP_CH = D_MODEL // FOURIER_GROUPS
D_FF = 2816
N_MOD = 9
LN_EPS = 1e-5
ALPHA = (2.0 * DEPTH) ** 0.25
NEG_INF = -1e30
LOG2E = float(np.log2(np.e))

LANES = 128
MOD_ROWS = 24
VMEM_LIMIT = 60 * 1024 * 1024
FFN_TILE = 1024
FFN_CHUNKS = (512,) * 5 + (256,)
FLIP = 256

BF16 = jnp.bfloat16
F32 = jnp.float32


def _params(n_axes):
    return pltpu.CompilerParams(dimension_semantics=("parallel",) * n_axes,
                                vmem_limit_bytes=VMEM_LIMIT)


def _resident(shape):
    return pl.BlockSpec(shape, lambda *_: (0,) * len(shape), pipeline_mode=pl.Buffered(1))


def _layer_norm(z, g, b):
    mu = jnp.mean(z, axis=-1, keepdims=True)
    d = z - mu
    var = jnp.mean(d * d, axis=-1, keepdims=True)
    return d * lax.rsqrt(var + LN_EPS) * g + b


def _modulate(x, mod_ref, j):
    shift = mod_ref[3 * j:3 * j + 1, :]
    scale = mod_ref[3 * j + 1:3 * j + 2, :]
    return x * (1.0 + scale) + shift


def _post_norm(x, y, mod_ref, j, g_ref, b_ref):
    gate = mod_ref[3 * j + 2:3 * j + 3, :]
    return _layer_norm(ALPHA * x + gate * y, g_ref[...], b_ref[...])


def _mod_kernel(cv_ref, w_ref, b_ref, o_ref):
    cv = cv_ref[...]
    s = (cv * jax.nn.sigmoid(cv)).astype(BF16)
    o_ref[...] = jnp.dot(s, w_ref[...].astype(BF16), preferred_element_type=F32) + b_ref[...]


def _mod_call(cv, mod_w, mod_b):
    tn = D_MODEL
    n_out = N_MOD * D_MODEL
    return pl.pallas_call(
        _mod_kernel,
        out_shape=jax.ShapeDtypeStruct((DEPTH, MOD_ROWS, n_out), F32),
        grid=(DEPTH, n_out // tn),
        in_specs=[
            pl.BlockSpec((MOD_ROWS, D_MODEL), lambda i, n: (0, 0)),
            pl.BlockSpec((None, D_MODEL, tn), lambda i, n: (i, 0, n)),
            pl.BlockSpec((None, 1, tn), lambda i, n: (i, 0, n)),
        ],
        out_specs=pl.BlockSpec((None, MOD_ROWS, tn), lambda i, n: (i, 0, n)),
        compiler_params=_params(2),
        name="mod",
    )(cv, mod_w, mod_b.reshape(DEPTH, 1, n_out))


def _ffn_kernel(j, n_tiles, x_ref, mod_ref, wi_ref, wo_ref, g_ref, b_ref, o_ref, u_ref, z_ref):
    i = pl.program_id(0)

    @pl.when(i == 0)
    def _():
        z_ref[...] = jnp.zeros_like(z_ref)

    def norm_previous():
        o_ref[...] = _layer_norm(z_ref[...], g_ref[...], b_ref[...])

    @pl.when(i < n_tiles)
    def _():
        norm_previous()
        h = _modulate(x_ref[...], mod_ref, j).astype(BF16)
        start = 0
        for width in FFN_CHUNKS:
            a = jnp.dot(h, wi_ref[:, start:start + width], preferred_element_type=F32)
            g = jnp.dot(h, wi_ref[:, D_FF + start:D_FF + start + width], preferred_element_type=F32)
            u_ref[:, start:start + width] = (a * (g * jax.nn.sigmoid(g))).astype(BF16)
            start += width
        y = 0.5 * jnp.dot(u_ref[...], wo_ref[...], preferred_element_type=F32)
        z_ref[...] = ALPHA * x_ref[...] + mod_ref[3 * j + 2:3 * j + 3, :] * y

    @pl.when(i == n_tiles)
    def _():
        norm_previous()


def _ffn_call(x, mod, j, wi_all, wo_all, layer, slot, g, b, tm):
    bsz, length, _ = x.shape
    per_batch = length // tm
    n_tiles = bsz * per_batch
    cur = lambda i: jnp.minimum(i, n_tiles - 1)
    prev = lambda i: jnp.maximum(i - 1, 0)
    tile = lambda pick: pl.BlockSpec((None, tm, D_MODEL), lambda i: (pick(i) // per_batch, pick(i) % per_batch, 0))
    weight = lambda rows, cols: pl.BlockSpec((None, None, rows, cols), lambda i: (layer, slot, 0, 0),
                                             pipeline_mode=pl.Buffered(1))
    return pl.pallas_call(
        functools.partial(_ffn_kernel, j, n_tiles),
        out_shape=jax.ShapeDtypeStruct(x.shape, F32),
        grid=(n_tiles + 1,),
        in_specs=[
            tile(cur),
            pl.BlockSpec((None, N_MOD, D_MODEL), lambda i: (cur(i) // per_batch, 0, 0)),
            weight(D_MODEL, 2 * D_FF),
            weight(D_FF, D_MODEL),
            _resident((1, D_MODEL)),
            _resident((1, D_MODEL)),
        ],
        out_specs=tile(prev),
        scratch_shapes=[pltpu.VMEM((tm, D_FF), BF16), pltpu.VMEM((tm, D_MODEL), F32)],
        compiler_params=pltpu.CompilerParams(dimension_semantics=("arbitrary",), vmem_limit_bytes=VMEM_LIMIT),
        name="ffn",
    )(x, mod, wi_all, wo_all, g.reshape(1, D_MODEL), b.reshape(1, D_MODEL))


def _rope(t, cos_ref, sa_ref, sb_ref):
    half = AXIS_DIM // 2
    outs = []
    for c in range(t.shape[1] // LANES):
        tc = t[:, c * LANES:(c + 1) * LANES]
        up = pltpu.roll(tc, LANES - half, 1)
        down = pltpu.roll(tc, half, 1)
        outs.append(tc * cos_ref[...] + up * sa_ref[...] + down * sb_ref[...])
    return jnp.concatenate(outs, axis=1)


def _dup_heads(t):
    pieces = []
    for h in range(t.shape[1] // HEAD_DIM):
        head = t[:, h * HEAD_DIM:(h + 1) * HEAD_DIM]
        pieces += [head, head]
    return jnp.concatenate(pieces, axis=1)


def _qkv_kernel(x_ref, mod_ref, w_ref, b_ref, cos_ref, sa_ref, sb_ref, q_ref, k_ref, vt_ref):
    h = _modulate(x_ref[...], mod_ref, 1).astype(BF16)
    qkv = jnp.dot(h, w_ref[...], preferred_element_type=F32) + b_ref[...]
    q = qkv[:, :Q_WIDTH] * (HEAD_DIM ** -0.5 * LOG2E)
    q_ref[...] = _rope(q, cos_ref, sa_ref, sb_ref).astype(BF16)
    k = _rope(qkv[:, Q_WIDTH:Q_WIDTH + KV_WIDTH], cos_ref, sa_ref, sb_ref)
    k_ref[...] = _dup_heads(k).astype(BF16)
    v = _dup_heads(qkv[:, Q_WIDTH + KV_WIDTH:])
    for blk in range(vt_ref.shape[0]):
        vt_ref[blk] = v[blk * BLOCK:(blk + 1) * BLOCK, :].T.astype(BF16)


def _qkv_call(x, mod, w, b, cos_t, sa_t, sb_t, tm):
    bsz, length, _ = x.shape
    tok = lambda width: pl.BlockSpec((None, tm, width), lambda bi, t: (bi, t, 0))
    table = pl.BlockSpec((tm, LANES), lambda bi, t: (t, 0))
    return pl.pallas_call(
        _qkv_kernel,
        out_shape=(jax.ShapeDtypeStruct((bsz, length, Q_WIDTH), BF16),
                   jax.ShapeDtypeStruct((bsz, length, KV_DUP_WIDTH), BF16),
                   jax.ShapeDtypeStruct((bsz, length // BLOCK, KV_DUP_WIDTH, BLOCK), BF16)),
        grid=(bsz, length // tm),
        in_specs=[
            tok(D_MODEL),
            pl.BlockSpec((None, N_MOD, D_MODEL), lambda bi, t: (bi, 0, 0)),
            _resident((D_MODEL, QKV_WIDTH)),
            _resident((1, QKV_WIDTH)),
            table, table, table,
        ],
        out_specs=(tok(Q_WIDTH), tok(KV_DUP_WIDTH),
                   pl.BlockSpec((None, tm // BLOCK, KV_DUP_WIDTH, BLOCK), lambda bi, t: (bi, t, 0, 0))),
        compiler_params=_params(2),
        name="qkv",
    )(x, mod, w, b.reshape(1, QKV_WIDTH), cos_t, sa_t, sb_t)


def _kv_ctx_kernel(x_ref, mod_ref, w_ref, b_ref, k_ref, vt_ref):
    h = _modulate(x_ref[...], mod_ref, 1).astype(BF16)
    kv = jnp.dot(h, w_ref[...], preferred_element_type=F32) + b_ref[...]
    k_ref[...] = _dup_heads(kv[:, :KV_WIDTH]).astype(BF16)
    vt_ref[...] = _dup_heads(kv[:, KV_WIDTH:]).T.astype(BF16)


def _kv_ctx_call(x, mod, w, b):
    bsz, n_ctx, _ = x.shape
    return pl.pallas_call(
        _kv_ctx_kernel,
        out_shape=(jax.ShapeDtypeStruct((bsz, n_ctx, KV_DUP_WIDTH), BF16),
                   jax.ShapeDtypeStruct((bsz, KV_DUP_WIDTH, n_ctx), BF16)),
        grid=(bsz,),
        in_specs=[
            pl.BlockSpec((None, n_ctx, D_MODEL), lambda bi: (bi, 0, 0)),
            _resident((None, N_MOD, D_MODEL)),
            _resident((D_MODEL, 2 * KV_WIDTH)),
            _resident((1, 2 * KV_WIDTH)),
        ],
        out_specs=(pl.BlockSpec((None, n_ctx, KV_DUP_WIDTH), lambda bi: (bi, 0, 0)),
                   pl.BlockSpec((None, KV_DUP_WIDTH, n_ctx), lambda bi: (bi, 0, 0))),
        compiler_params=_params(1),
        name="kv_ctx",
    )(x, mod, w, b.reshape(1, 2 * KV_WIDTH))


def _attn_kernel(n_blocks, n_steps, sink_ref, q_ref, k_ref, vt_ref, kc_ref, vct_ref, xp_ref, modp_ref, wo_ref,
                 g_ref, b_ref, out_ref, o_ref):
    i = pl.program_id(0)

    @pl.when(i == 0)
    def _():
        o_ref[...] = jnp.zeros_like(o_ref)

    def project(o_prev):
        return jnp.dot(o_prev, wo_ref[...], preferred_element_type=F32)

    def norm(y):
        out_ref[...] = _post_norm(xp_ref[...], y, modp_ref, 1, g_ref, b_ref)

    @pl.when(i < n_steps)
    def _():
        o_prev = o_ref[...]
        y = []
        tile_in_batch = i % (n_blocks * BLOCK // q_ref.shape[0])
        _attend(n_blocks, tile_in_batch, sink_ref, q_ref, k_ref, vt_ref, kc_ref, vct_ref, o_ref,
                after_slot={3: lambda: y.append(project(o_prev)), 9: lambda: norm(y[0])})

    @pl.when(i == n_steps)
    def _():
        norm(project(o_ref[...]))


def _attend(n_blocks, tile_in_batch, sink_ref, q_ref, k_ref, vt_ref, kc_ref, vct_ref, o_ref, after_slot):
    n_sub = q_ref.shape[0] // BLOCK
    qb0 = tile_in_batch * n_sub
    cols2 = 2 * BLOCK
    key_i = lax.broadcasted_iota(jnp.int32, (BLOCK, cols2), 0)
    qry_i = lax.broadcasted_iota(jnp.int32, (BLOCK, cols2), 1) & (BLOCK - 1)
    first_head = lax.broadcasted_iota(jnp.int32, (1, cols2), 1) < BLOCK
    lo_q = lax.broadcasted_iota(jnp.int32, (cols2, LANES), 1) < HEAD_DIM
    lo_v = lax.broadcasted_iota(jnp.int32, (LANES, 1), 0) < HEAD_DIM

    def block_ids(sub):
        return [jnp.clip(qb0 + sub + d, 0, n_blocks - 1) for d in (-1, 0, 1)]

    slots_per_block = 2 * N_KV_HEADS
    p_t, inv = {}, {}

    def scores(slot):
        sub, rest = divmod(slot, slots_per_block)
        h, half = divmod(rest, 2)
        c0 = 2 * h * LANES
        q_rows = slice(sub * BLOCK, (sub + 1) * BLOCK)
        q_cat = jnp.concatenate([q_ref[q_rows, c0:c0 + LANES], q_ref[q_rows, c0 + LANES:c0 + 2 * LANES]], axis=0)
        keep = lo_q if half == 0 else jnp.logical_not(lo_q)
        q_sel = jnp.where(keep, q_cat, jnp.zeros_like(q_cat))
        lanes = slice(h * LANES, (h + 1) * LANES)
        k_h = jnp.concatenate([k_ref[pl.ds(pl.multiple_of(blk * BLOCK, BLOCK), BLOCK), lanes]
                               for blk in block_ids(sub)] + [kc_ref[:, lanes]], axis=0)
        return lax.dot_general(k_h, q_sel, (((1,), (1,)), ((), ())), preferred_element_type=F32)

    def softmax(slot, s_t):
        sub, rest = divmod(slot, slots_per_block)
        h, half = divmod(rest, 2)
        left_ok = (qry_i <= key_i) & (qb0 + sub > 0)
        right_ok = (key_i <= qry_i) & (qb0 + sub < n_blocks - 1)
        sink = jnp.where(first_head, sink_ref[4 * h + half], sink_ref[4 * h + 2 + half]) * LOG2E
        parts = [jnp.where(left_ok, s_t[:BLOCK], NEG_INF),
                 s_t[BLOCK:2 * BLOCK],
                 jnp.where(right_ok, s_t[2 * BLOCK:SPAN], NEG_INF)]
        parts += [s_t[c:c + BLOCK] for c in range(SPAN, s_t.shape[0], BLOCK)]
        m = jnp.max(functools.reduce(jnp.maximum, parts), axis=0, keepdims=True)
        m = jnp.maximum(m, sink)
        p = [jnp.exp2(t - m) for t in parts]
        denom = jnp.sum(functools.reduce(jnp.add, p), axis=0, keepdims=True) + jnp.exp2(sink - m)
        inv[slot] = 1.0 / denom
        p_t[slot] = jnp.concatenate(p, axis=0).astype(BF16)

    def weighted_values(pair):
        sub, h = divmod(pair, N_KV_HEADS)
        rows = slice(h * LANES, (h + 1) * LANES)
        v_t = jnp.concatenate([vt_ref[blk, rows, :] for blk in block_ids(sub)] + [vct_ref[rows, :]], axis=1)
        v_lo = jnp.where(lo_v, v_t, jnp.zeros_like(v_t))
        v_hi = jnp.where(lo_v, jnp.zeros_like(v_t), v_t)
        acc_t = (jnp.dot(v_lo, p_t.pop(2 * pair), preferred_element_type=F32)
                 + jnp.dot(v_hi, p_t.pop(2 * pair + 1), preferred_element_type=F32))
        out = (acc_t * jnp.where(lo_v, inv.pop(2 * pair), inv.pop(2 * pair + 1))).T.astype(BF16)
        c0 = 2 * h * LANES
        q_rows = slice(sub * BLOCK, (sub + 1) * BLOCK)
        o_ref[q_rows, c0:c0 + LANES] = out[:BLOCK]
        o_ref[q_rows, c0 + LANES:c0 + 2 * LANES] = out[BLOCK:]

    n_slots = n_sub * slots_per_block
    s_next = scores(0)
    for slot in range(n_slots):
        s_cur = s_next
        if slot + 1 < n_slots:
            s_next = scores(slot + 1)
        softmax(slot, s_cur)
        if slot % 2 == 1:
            weighted_values(slot // 2)
        if slot in after_slot:
            after_slot[slot]()


def _attn_call(q, k, vt, kc, vct, sink, x, mod, wo, g, b, tq):
    bsz, length, _ = q.shape
    n_blocks = length // BLOCK
    per_batch_tiles = length // tq
    n_steps = bsz * per_batch_tiles
    cur = lambda i: jnp.minimum(i, n_steps - 1)
    prev = lambda i: jnp.maximum(i - 1, 0)
    tile = lambda pick: pl.BlockSpec(
        (None, tq, D_MODEL), lambda i, *_: (pick(i) // per_batch_tiles, pick(i) % per_batch_tiles, 0))
    per_batch = lambda arr: pl.BlockSpec(
        (None,) + arr.shape[1:], lambda i, *_: (cur(i) // per_batch_tiles,) + (0,) * (arr.ndim - 1))
    return pl.pallas_call(
        functools.partial(_attn_kernel, n_blocks, n_steps),
        out_shape=jax.ShapeDtypeStruct(x.shape, F32),
        grid_spec=pltpu.PrefetchScalarGridSpec(
            num_scalar_prefetch=1,
            grid=(n_steps + 1,),
            in_specs=[
                tile(cur),
                per_batch(k), per_batch(vt), per_batch(kc), per_batch(vct),
                tile(prev),
                pl.BlockSpec((None, N_MOD, D_MODEL), lambda i, *_: (prev(i) // per_batch_tiles, 0, 0)),
                _resident((D_MODEL, D_MODEL)),
                _resident((1, D_MODEL)),
                _resident((1, D_MODEL)),
            ],
            out_specs=tile(prev),
            scratch_shapes=[pltpu.VMEM((tq, Q_WIDTH), BF16)],
        ),
        compiler_params=pltpu.CompilerParams(dimension_semantics=("arbitrary",), vmem_limit_bytes=VMEM_LIMIT),
        name="attn",
    )(sink, q, k, vt, kc, vct, x, mod, wo, g.reshape(1, D_MODEL), b.reshape(1, D_MODEL))


def _reverse_shift(t, first_row, flip_ref):
    rows = t.shape[0]
    blocks = [jnp.dot(flip_ref[...], t[r0:r0 + FLIP, :], preferred_element_type=F32)
              for r0 in range(rows - FLIP, -FLIP, -FLIP)]
    shifted = pltpu.roll(jnp.concatenate(blocks, axis=0), 1, 0)
    is_first = lax.broadcasted_iota(jnp.int32, shifted.shape, 0) == 0
    return jnp.where(is_first, first_row, shifted)


def _channel_dft(t, w_ref):
    return jnp.concatenate(
        [jnp.dot(t[:, g0:g0 + FOURIER_GROUP_CH], w_ref[...], preferred_element_type=F32)
         for g0 in range(0, D_MODEL, FOURIER_GROUP_CH)], axis=1)


def _dft_fold_kernel(lo_ref, up_ref, mod_ref, cc_ref, sc_ref, flip_ref, ab_ref, an_ref, carry_ref):
    @pl.when(pl.program_id(1) == 0)
    def _():
        carry_ref[...] = jnp.zeros_like(carry_ref)

    h_lo = _modulate(lo_ref[...], mod_ref, 1)
    h_up = _modulate(up_ref[...], mod_ref, 1).astype(BF16)
    partner = _reverse_shift(h_up, carry_ref[0:1, :], flip_ref)
    carry_ref[...] = h_up[0:8, :].astype(F32)
    ab_ref[0] = _channel_dft((h_lo + partner).astype(BF16), cc_ref).astype(BF16)
    ab_ref[1] = _channel_dft((h_lo - partner).astype(BF16), sc_ref).astype(BF16)
    an_ref[...] = _channel_dft(h_up[0:8, :], cc_ref)


def _dft_fold_call(x4, mod, cc, sc, flip, tm):
    bsz, _, half, _ = x4.shape
    n_t = half // tm
    ch = FOURIER_GROUP_CH
    return pl.pallas_call(
        _dft_fold_kernel,
        out_shape=(jax.ShapeDtypeStruct((bsz, 2, half, D_MODEL), BF16),
                   jax.ShapeDtypeStruct((bsz, 8, D_MODEL), F32)),
        grid=(bsz, n_t),
        in_specs=[
            pl.BlockSpec((None, None, tm, D_MODEL), lambda bi, t: (bi, 0, t, 0)),
            pl.BlockSpec((None, None, tm, D_MODEL), lambda bi, t: (bi, 1, n_t - 1 - t, 0)),
            pl.BlockSpec((None, N_MOD, D_MODEL), lambda bi, t: (bi, 0, 0)),
            _resident((ch, ch)),
            _resident((ch, ch)),
            _resident((FLIP, FLIP)),
        ],
        out_specs=(pl.BlockSpec((None, 2, tm, D_MODEL), lambda bi, t: (bi, 0, t, 0)),
                   pl.BlockSpec((None, 8, D_MODEL), lambda bi, t: (bi, 0, 0))),
        scratch_shapes=[pltpu.VMEM((8, D_MODEL), F32)],
        compiler_params=pltpu.CompilerParams(dimension_semantics=("parallel", "arbitrary"),
                                             vmem_limit_bytes=VMEM_LIMIT),
        name="dft_fold",
    )(x4, x4, mod, cc, sc, flip)


def _dft_seq_kernel(norm, n_t, wlo_ref, wup_ref, ab_ref, an_ref, alt_ref, flip_ref, x_ref, mod_ref, wo_ref,
                    g_ref, beta_ref, out_ref, carry_ref):
    t = pl.program_id(1)
    tk = wlo_ref.shape[0]
    ab = ab_ref[...]
    a_n = an_ref[0:1, :]
    row = lax.broadcasted_iota(jnp.int32, (tk, 1), 0)

    def alternating(k0):
        return (1 - 2 * ((k0 + row) & 1)).astype(F32)

    f_lo = (jnp.dot(wlo_ref[...], ab, preferred_element_type=F32) + alternating(t * tk) * a_n) * norm
    f_mir = (jnp.dot(wup_ref[...], ab, preferred_element_type=F32) + alternating((n_t - 1 - t) * tk) * a_n) * norm
    f_mir = f_mir.astype(BF16)

    @pl.when(t == 0)
    def _():
        f_n = (jnp.dot(alt_ref[...], ab, preferred_element_type=F32)[0:1, :] + a_n) * norm
        carry_ref[0:1, :] = f_n.astype(BF16).astype(F32)

    f_up = _reverse_shift(f_mir, carry_ref[0:1, :], flip_ref)
    carry_ref[...] = f_mir[0:8, :].astype(F32)
    y_lo = jnp.dot(f_lo.astype(BF16), wo_ref[...], preferred_element_type=F32)
    y_up = jnp.dot(f_up.astype(BF16), wo_ref[...], preferred_element_type=F32)
    out_ref[0] = _post_norm(x_ref[0], y_lo, mod_ref, 1, g_ref, beta_ref)
    out_ref[1] = _post_norm(x_ref[1], y_up, mod_ref, 1, g_ref, beta_ref)


def _dft_seq_call(w_lo, w_up, ab, a_n, alt, flip, x4, mod, wo, g, beta, tk):
    bsz, _, half, _ = x4.shape
    n_t = half // tk
    norm = float(1.0 / np.sqrt(2 * half * FOURIER_GROUP_CH))
    tok = pl.BlockSpec((None, 2, tk, D_MODEL), lambda bi, t: (bi, 0, t, 0))
    return pl.pallas_call(
        functools.partial(_dft_seq_kernel, norm, n_t),
        out_shape=jax.ShapeDtypeStruct(x4.shape, F32),
        grid=(bsz, n_t),
        in_specs=[
            pl.BlockSpec((tk, 2 * half), lambda bi, t: (t, 0)),
            pl.BlockSpec((tk, 2 * half), lambda bi, t: (n_t - 1 - t, 0)),
            pl.BlockSpec((None, 2 * half, D_MODEL), lambda bi, t: (bi, 0, 0)),
            pl.BlockSpec((None, 8, D_MODEL), lambda bi, t: (bi, 0, 0)),
            _resident((8, 2 * half)),
            _resident((FLIP, FLIP)),
            tok,
            pl.BlockSpec((None, N_MOD, D_MODEL), lambda bi, t: (bi, 0, 0)),
            _resident((D_MODEL, D_MODEL)),
            _resident((1, D_MODEL)),
            _resident((1, D_MODEL)),
        ],
        out_specs=tok,
        scratch_shapes=[pltpu.VMEM((8, D_MODEL), F32)],
        compiler_params=pltpu.CompilerParams(dimension_semantics=("parallel", "arbitrary"),
                                             vmem_limit_bytes=VMEM_LIMIT),
        name="dft_seq",
    )(w_lo, w_up, ab, a_n, alt, flip, x4, mod, wo, g.reshape(1, D_MODEL), beta.reshape(1, D_MODEL))


def _rope_tables(n_tokens):
    rows = n_tokens // GRID_W
    row = jnp.repeat(jnp.arange(rows), GRID_W).astype(F32)
    col = jnp.tile(jnp.arange(GRID_W), rows).astype(F32)
    inv = ROPE_BASE ** (-jnp.arange(0, AXIS_DIM, 2, dtype=F32) / AXIS_DIM)
    ang_r, ang_c = row[:, None] * inv, col[:, None] * inv
    cos_r, sin_r, cos_c, sin_c = jnp.cos(ang_r), jnp.sin(ang_r), jnp.cos(ang_c), jnp.sin(ang_c)
    zero = jnp.zeros_like(sin_r)
    cos_h = jnp.concatenate([cos_r, cos_r, cos_c, cos_c], axis=-1)
    sa_h = jnp.concatenate([-sin_r, zero, -sin_c, zero], axis=-1)
    sb_h = jnp.concatenate([zero, sin_r, zero, sin_c], axis=-1)
    rep = LANES // HEAD_DIM
    return tuple(jnp.tile(t, (1, rep)) for t in (cos_h, sa_h, sb_h))


def _dft_tables(n, rows):
    idx = jnp.arange(rows, dtype=jnp.int32)
    ang = ((idx[:, None] * idx[None, :]) % n).astype(F32) * (2.0 * np.pi / n)
    return jnp.cos(ang), jnp.sin(ang)


def kernel(x, c, ctx, c_ctx, mod_w, mod_b, ln_g, ln_b, ffn_wi, ffn_wo,
           attn_wqkv, attn_bqkv, attn_wo, attn_sink, fourier_wo):
    bsz, seq, _ = x.shape
    n_ctx = ctx.shape[1]
    assert DEPTH == 2 and seq % BLOCK == 0 and seq % GRID_W == 0

    cv = jnp.concatenate([c, c_ctx[None, :], jnp.zeros((MOD_ROWS - bsz - 1, D_MODEL), F32)], axis=0)
    mod = _mod_call(cv, mod_w, mod_b).reshape(DEPTH, MOD_ROWS, N_MOD, D_MODEL)
    wi = ffn_wi.astype(BF16)
    wo = ffn_wo.astype(BF16)

    mod_lat, mod_ctx = mod[0, :bsz], mod[0, bsz:bsz + 1]
    g, b = ln_g[0], ln_b[0]
    x = _ffn_call(x, mod_lat, 0, wi, wo, 0, 0, g[0], b[0], tm=FFN_TILE)
    ctx_s = _ffn_call(ctx.reshape(1, bsz * n_ctx, D_MODEL), mod_ctx, 0, wi, wo, 0, 0, g[0], b[0], tm=FFN_TILE)

    w_qkv = attn_wqkv[0].astype(BF16)
    cos_t, sa_t, sb_t = _rope_tables(seq)
    q, k, vt = _qkv_call(x, mod_lat, w_qkv, attn_bqkv[0], cos_t, sa_t, sb_t, tm=512)
    kc, vct = _kv_ctx_call(ctx_s.reshape(bsz, n_ctx, D_MODEL), mod_ctx, w_qkv[:, Q_WIDTH:], attn_bqkv[0, Q_WIDTH:])
    x = _attn_call(q, k, vt, kc, vct, attn_sink[0], x, mod_lat, attn_wo[0].astype(BF16), g[1], b[1], tq=2 * BLOCK)
    x = _ffn_call(x, mod_lat, 2, wi, wo, 0, 1, g[2], b[2], tm=FFN_TILE)

    mod_lat = mod[1, :bsz]
    g, b = ln_g[1], ln_b[1]
    x = _ffn_call(x, mod_lat, 0, wi, wo, 1, 0, g[0], b[0], tm=FFN_TILE)
    half = seq // 2
    cos_c, sin_c = _dft_tables(FOURIER_GROUP_CH, FOURIER_GROUP_CH)
    cos_s, sin_s = _dft_tables(seq, half)
    w_lo = jnp.concatenate([cos_s, -sin_s], axis=1).astype(BF16)
    w_up = jnp.concatenate([cos_s, sin_s], axis=1).astype(BF16)
    alt = jnp.zeros((8, seq), F32).at[0, :half].set(1.0 - 2.0 * (jnp.arange(half) % 2)).astype(BF16)
    anti = jnp.arange(FLIP)[:, None] + jnp.arange(FLIP)[None, :] == FLIP - 1
    flip = anti.astype(BF16)
    x4 = x.reshape(bsz, 2, half, D_MODEL)
    ab, a_n = _dft_fold_call(x4, mod_lat, cos_c.astype(BF16), sin_c.astype(BF16), flip, tm=512)
    x4 = _dft_seq_call(w_lo, w_up, ab.reshape(bsz, seq, D_MODEL), a_n, alt, flip, x4, mod_lat,
                       fourier_wo[0].astype(BF16), g[1], b[1], tk=512)
    x = x4.reshape(bsz, seq, D_MODEL)
    x = _ffn_call(x, mod_lat, 2, wi, wo, 1, 1, g[2], b[2], tm=FFN_TILE)
    return x
```

```python
import functools

import jax
import jax.numpy as jnp
import numpy as np
from jax import lax
from jax.experimental import pallas as pl
from jax.experimental.pallas import tpu as pltpu

D_MODEL = 1024
DEPTH = 2
GRID_W = 64
N_HEADS = 16
N_KV_HEADS = 4
HEAD_DIM = 64
GROUP = N_HEADS // N_KV_HEADS
Q_WIDTH = N_HEADS * HEAD_DIM
KV_WIDTH = N_KV_HEADS * HEAD_DIM
QKV_WIDTH = Q_WIDTH + 2 * KV_WIDTH
KV_DUP_WIDTH = 2 * KV_WIDTH
WINDOW = 128
BLOCK = 128
SPAN = BLOCK + 2 * WINDOW
ROPE_BASE = 10000.0
AXIS_DIM = HEAD_DIM // 2
FOURIER_GROUPS = 4
FOURIER_GROUP_CH = D_MODEL // FOURIER_GROUPS
D_FF = 2816
N_MOD = 9
LN_EPS = 1e-5
ALPHA = (2.0 * DEPTH) ** 0.25
NEG_INF = -1e30
LOG2E = float(np.log2(np.e))

LANES = 128
MOD_ROWS = 24
VMEM_LIMIT = 60 * 1024 * 1024
FFN_TILE = 1024
FFN_CHUNKS = (1024, 1024, 768)
FLIP = 256

BF16 = jnp.bfloat16
F32 = jnp.float32


def _params(n_axes):
    return pltpu.CompilerParams(dimension_semantics=("parallel",) * n_axes,
                                vmem_limit_bytes=VMEM_LIMIT)


def _resident(shape):
    return pl.BlockSpec(shape, lambda *_: (0,) * len(shape), pipeline_mode=pl.Buffered(1))


def _layer_norm(z, g, b):
    mu = jnp.mean(z, axis=-1, keepdims=True)
    d = z - mu
    var = jnp.mean(d * d, axis=-1, keepdims=True)
    return d * lax.rsqrt(var + LN_EPS) * g + b


def _modulate(x, mod_ref, j):
    shift = mod_ref[3 * j:3 * j + 1, :]
    scale = mod_ref[3 * j + 1:3 * j + 2, :]
    return x * (1.0 + scale) + shift


def _post_norm(x, y, mod_ref, j, g_ref, b_ref):
    gate = mod_ref[3 * j + 2:3 * j + 3, :]
    return _layer_norm(ALPHA * x + gate * y, g_ref[...], b_ref[...])


def _mod_kernel(cv_ref, w_ref, b_ref, o_ref):
    cv = cv_ref[...]
    s = (cv * jax.nn.sigmoid(cv)).astype(BF16)
    o_ref[...] = jnp.dot(s, w_ref[...].astype(BF16), preferred_element_type=F32) + b_ref[...]


def _mod_call(cv, mod_w, mod_b):
    tn = D_MODEL
    n_out = N_MOD * D_MODEL
    return pl.pallas_call(
        _mod_kernel,
        out_shape=jax.ShapeDtypeStruct((DEPTH, MOD_ROWS, n_out), F32),
        grid=(DEPTH, n_out // tn),
        in_specs=[
            pl.BlockSpec((MOD_ROWS, D_MODEL), lambda i, n: (0, 0)),
            pl.BlockSpec((None, D_MODEL, tn), lambda i, n: (i, 0, n)),
            pl.BlockSpec((None, 1, tn), lambda i, n: (i, 0, n)),
        ],
        out_specs=pl.BlockSpec((None, MOD_ROWS, tn), lambda i, n: (i, 0, n)),
        compiler_params=_params(2),
        name="mod",
    )(cv, mod_w, mod_b.reshape(DEPTH, 1, n_out))


def _ffn_kernel(j, n_tiles, x_ref, mod_ref, wi_ref, wo_ref, g_ref, b_ref, o_ref, u_ref, z_ref):
    i = pl.program_id(0)

    @pl.when(i == 0)
    def _():
        z_ref[...] = jnp.zeros_like(z_ref)

    def norm_previous():
        o_ref[...] = _layer_norm(z_ref[...], g_ref[...], b_ref[...])

    @pl.when(i < n_tiles)
    def _():
        norm_previous()
        h = _modulate(x_ref[...], mod_ref, j).astype(BF16)
        start = 0
        for width in FFN_CHUNKS:
            a = jnp.dot(h, wi_ref[:, start:start + width], preferred_element_type=F32)
            g = jnp.dot(h, wi_ref[:, D_FF + start:D_FF + start + width], preferred_element_type=F32)
            u_ref[:, start:start + width] = (a * (g * jax.nn.sigmoid(g))).astype(BF16)
            start += width
        y = 0.5 * jnp.dot(u_ref[...], wo_ref[...], preferred_element_type=F32)
        z_ref[...] = ALPHA * x_ref[...] + mod_ref[3 * j + 2:3 * j + 3, :] * y

    @pl.when(i == n_tiles)
    def _():
        norm_previous()


def _ffn_call(x, mod, j, wi_all, wo_all, layer, slot, g, b, tm):
    bsz, length, _ = x.shape
    per_batch = length // tm
    n_tiles = bsz * per_batch
    cur = lambda i: jnp.minimum(i, n_tiles - 1)
    prev = lambda i: jnp.maximum(i - 1, 0)
    tile = lambda pick: pl.BlockSpec((None, tm, D_MODEL), lambda i: (pick(i) // per_batch, pick(i) % per_batch, 0))
    weight = lambda rows, cols: pl.BlockSpec((None, None, rows, cols), lambda i: (layer, slot, 0, 0),
                                             pipeline_mode=pl.Buffered(1))
    return pl.pallas_call(
        functools.partial(_ffn_kernel, j, n_tiles),
        out_shape=jax.ShapeDtypeStruct(x.shape, F32),
        grid=(n_tiles + 1,),
        in_specs=[
            tile(cur),
            pl.BlockSpec((None, N_MOD, D_MODEL), lambda i: (cur(i) // per_batch, 0, 0)),
            weight(D_MODEL, 2 * D_FF),
            weight(D_FF, D_MODEL),
            _resident((1, D_MODEL)),
            _resident((1, D_MODEL)),
        ],
        out_specs=tile(prev),
        scratch_shapes=[pltpu.VMEM((tm, D_FF), BF16), pltpu.VMEM((tm, D_MODEL), F32)],
        compiler_params=pltpu.CompilerParams(dimension_semantics=("arbitrary",), vmem_limit_bytes=VMEM_LIMIT),
        name="ffn",
    )(x, mod, wi_all, wo_all, g.reshape(1, D_MODEL), b.reshape(1, D_MODEL))


def _rope(t, cos_ref, sa_ref, sb_ref):
    half = AXIS_DIM // 2
    outs = []
    for c in range(t.shape[1] // LANES):
        tc = t[:, c * LANES:(c + 1) * LANES]
        up = pltpu.roll(tc, LANES - half, 1)
        down = pltpu.roll(tc, half, 1)
        outs.append(tc * cos_ref[...] + up * sa_ref[...] + down * sb_ref[...])
    return jnp.concatenate(outs, axis=1)


def _dup_heads(t):
    pieces = []
    for h in range(t.shape[1] // HEAD_DIM):
        head = t[:, h * HEAD_DIM:(h + 1) * HEAD_DIM]
        pieces += [head, head]
    return jnp.concatenate(pieces, axis=1)


def _qkv_kernel(x_ref, mod_ref, w_ref, b_ref, cos_ref, sa_ref, sb_ref, q_ref, k_ref, vt_ref):
    h = _modulate(x_ref[...], mod_ref, 1).astype(BF16)
    qkv = jnp.dot(h, w_ref[...], preferred_element_type=F32) + b_ref[...]
    q = qkv[:, :Q_WIDTH] * (HEAD_DIM ** -0.5 * LOG2E)
    q_ref[...] = _rope(q, cos_ref, sa_ref, sb_ref).astype(BF16)
    k = _rope(qkv[:, Q_WIDTH:Q_WIDTH + KV_WIDTH], cos_ref, sa_ref, sb_ref)
    k_ref[...] = _dup_heads(k).astype(BF16)
    v = _dup_heads(qkv[:, Q_WIDTH + KV_WIDTH:])
    for blk in range(vt_ref.shape[0]):
        vt_ref[blk] = v[blk * BLOCK:(blk + 1) * BLOCK, :].T.astype(BF16)


def _qkv_call(x, mod, w, b, cos_t, sa_t, sb_t, tm):
    bsz, length, _ = x.shape
    tok = lambda width: pl.BlockSpec((None, tm, width), lambda bi, t: (bi, t, 0))
    table = pl.BlockSpec((tm, LANES), lambda bi, t: (t, 0))
    return pl.pallas_call(
        _qkv_kernel,
        out_shape=(jax.ShapeDtypeStruct((bsz, length, Q_WIDTH), BF16),
                   jax.ShapeDtypeStruct((bsz, length, KV_DUP_WIDTH), BF16),
                   jax.ShapeDtypeStruct((bsz, length // BLOCK, KV_DUP_WIDTH, BLOCK), BF16)),
        grid=(bsz, length // tm),
        in_specs=[
            tok(D_MODEL),
            pl.BlockSpec((None, N_MOD, D_MODEL), lambda bi, t: (bi, 0, 0)),
            _resident((D_MODEL, QKV_WIDTH)),
            _resident((1, QKV_WIDTH)),
            table, table, table,
        ],
        out_specs=(tok(Q_WIDTH), tok(KV_DUP_WIDTH),
                   pl.BlockSpec((None, tm // BLOCK, KV_DUP_WIDTH, BLOCK), lambda bi, t: (bi, t, 0, 0))),
        compiler_params=_params(2),
        name="qkv",
    )(x, mod, w, b.reshape(1, QKV_WIDTH), cos_t, sa_t, sb_t)


def _kv_ctx_kernel(x_ref, mod_ref, w_ref, b_ref, k_ref, vt_ref):
    h = _modulate(x_ref[...], mod_ref, 1).astype(BF16)
    kv = jnp.dot(h, w_ref[...], preferred_element_type=F32) + b_ref[...]
    k_ref[...] = _dup_heads(kv[:, :KV_WIDTH]).astype(BF16)
    vt_ref[...] = _dup_heads(kv[:, KV_WIDTH:]).T.astype(BF16)


def _kv_ctx_call(x, mod, w, b):
    bsz, n_ctx, _ = x.shape
    return pl.pallas_call(
        _kv_ctx_kernel,
        out_shape=(jax.ShapeDtypeStruct((bsz, n_ctx, KV_DUP_WIDTH), BF16),
                   jax.ShapeDtypeStruct((bsz, KV_DUP_WIDTH, n_ctx), BF16)),
        grid=(bsz,),
        in_specs=[
            pl.BlockSpec((None, n_ctx, D_MODEL), lambda bi: (bi, 0, 0)),
            _resident((None, N_MOD, D_MODEL)),
            _resident((D_MODEL, 2 * KV_WIDTH)),
            _resident((1, 2 * KV_WIDTH)),
        ],
        out_specs=(pl.BlockSpec((None, n_ctx, KV_DUP_WIDTH), lambda bi: (bi, 0, 0)),
                   pl.BlockSpec((None, KV_DUP_WIDTH, n_ctx), lambda bi: (bi, 0, 0))),
        compiler_params=_params(1),
        name="kv_ctx",
    )(x, mod, w, b.reshape(1, 2 * KV_WIDTH))


def _attn_kernel(n_blocks, n_steps, sink_ref, q_ref, k_ref, vt_ref, kc_ref, vct_ref, xp_ref, modp_ref, wo_ref,
                 g_ref, b_ref, out_ref, o_ref):
    i = pl.program_id(0)

    @pl.when(i == 0)
    def _():
        o_ref[...] = jnp.zeros_like(o_ref)

    def project(o_prev):
        return jnp.dot(o_prev, wo_ref[...], preferred_element_type=F32)

    def norm(y):
        out_ref[...] = _post_norm(xp_ref[...], y, modp_ref, 1, g_ref, b_ref)

    @pl.when(i < n_steps)
    def _():
        o_prev = o_ref[...]
        y = []
        tile_in_batch = i % (n_blocks * BLOCK // q_ref.shape[0])
        _attend(n_blocks, tile_in_batch, sink_ref, q_ref, k_ref, vt_ref, kc_ref, vct_ref, o_ref,
                after_slot={9: lambda: y.append(project(o_prev)), 21: lambda: norm(y[0])})

    @pl.when(i == n_steps)
    def _():
        norm(project(o_ref[...]))


def _attend(n_blocks, tile_in_batch, sink_ref, q_ref, k_ref, vt_ref, kc_ref, vct_ref, o_ref, after_slot):
    n_sub = q_ref.shape[0] // BLOCK
    qb0 = tile_in_batch * n_sub
    cols2 = 2 * BLOCK
    key_i = lax.broadcasted_iota(jnp.int32, (BLOCK, cols2), 0)
    qry_i = lax.broadcasted_iota(jnp.int32, (BLOCK, cols2), 1) & (BLOCK - 1)
    first_head = lax.broadcasted_iota(jnp.int32, (1, cols2), 1) < BLOCK
    lo_q = lax.broadcasted_iota(jnp.int32, (cols2, LANES), 1) < HEAD_DIM
    lo_v = lax.broadcasted_iota(jnp.int32, (LANES, 1), 0) < HEAD_DIM

    def block_ids(sub):
        return [jnp.clip(qb0 + sub + d, 0, n_blocks - 1) for d in (-1, 0, 1)]

    slots_per_block = 2 * N_KV_HEADS
    p_t, inv = {}, {}

    def scores(slot):
        sub, rest = divmod(slot, slots_per_block)
        h, half = divmod(rest, 2)
        c0 = 2 * h * LANES
        q_rows = slice(sub * BLOCK, (sub + 1) * BLOCK)
        q_cat = jnp.concatenate([q_ref[q_rows, c0:c0 + LANES], q_ref[q_rows, c0 + LANES:c0 + 2 * LANES]], axis=0)
        keep = lo_q if half == 0 else jnp.logical_not(lo_q)
        q_sel = jnp.where(keep, q_cat, jnp.zeros_like(q_cat))
        lanes = slice(h * LANES, (h + 1) * LANES)
        k_h = jnp.concatenate([k_ref[pl.ds(pl.multiple_of(blk * BLOCK, BLOCK), BLOCK), lanes]
                               for blk in block_ids(sub)] + [kc_ref[:, lanes]], axis=0)
        return lax.dot_general(k_h, q_sel, (((1,), (1,)), ((), ())), preferred_element_type=F32)

    def softmax(slot, s_t):
        sub, rest = divmod(slot, slots_per_block)
        h, half = divmod(rest, 2)
        left_ok = (qry_i <= key_i) & (qb0 + sub > 0)
        right_ok = (key_i <= qry_i) & (qb0 + sub < n_blocks - 1)
        sink = jnp.where(first_head, sink_ref[4 * h + half], sink_ref[4 * h + 2 + half]) * LOG2E
        parts = [jnp.where(left_ok, s_t[:BLOCK], NEG_INF),
                 s_t[BLOCK:2 * BLOCK],
                 jnp.where(right_ok, s_t[2 * BLOCK:SPAN], NEG_INF)]
        parts += [s_t[c:c + BLOCK] for c in range(SPAN, s_t.shape[0], BLOCK)]
        m = jnp.max(functools.reduce(jnp.maximum, parts), axis=0, keepdims=True)
        m = jnp.maximum(m, sink)
        p = [jnp.exp2(t - m) for t in parts]
        denom = jnp.sum(functools.reduce(jnp.add, p), axis=0, keepdims=True) + jnp.exp2(sink - m)
        inv[slot] = 1.0 / denom
        p_t[slot] = jnp.concatenate(p, axis=0).astype(BF16)

    def weighted_values(pair):
        sub, h = divmod(pair, N_KV_HEADS)
        rows = slice(h * LANES, (h + 1) * LANES)
        v_t = jnp.concatenate([vt_ref[blk, rows, :] for blk in block_ids(sub)] + [vct_ref[rows, :]], axis=1)
        v_lo = jnp.where(lo_v, v_t, jnp.zeros_like(v_t))
        v_hi = jnp.where(lo_v, jnp.zeros_like(v_t), v_t)
        acc_t = (jnp.dot(v_lo, p_t.pop(2 * pair), preferred_element_type=F32)
                 + jnp.dot(v_hi, p_t.pop(2 * pair + 1), preferred_element_type=F32))
        out = (acc_t * jnp.where(lo_v, inv.pop(2 * pair), inv.pop(2 * pair + 1))).T.astype(BF16)
        c0 = 2 * h * LANES
        q_rows = slice(sub * BLOCK, (sub + 1) * BLOCK)
        o_ref[q_rows, c0:c0 + LANES] = out[:BLOCK]
        o_ref[q_rows, c0 + LANES:c0 + 2 * LANES] = out[BLOCK:]

    n_slots = n_sub * slots_per_block
    ahead = 1
    s_t = {slot: scores(slot) for slot in range(ahead)}
    for slot in range(n_slots):
        if slot + ahead < n_slots:
            s_t[slot + ahead] = scores(slot + ahead)
        softmax(slot, s_t.pop(slot))
        if slot % 2 == 0 and slot > 0:
            weighted_values(slot // 2 - 1)
        if slot in after_slot:
            after_slot[slot]()
    weighted_values(n_slots // 2 - 1)


def _attn_call(q, k, vt, kc, vct, sink, x, mod, wo, g, b, tq):
    bsz, length, _ = q.shape
    n_blocks = length // BLOCK
    per_batch_tiles = length // tq
    n_steps = bsz * per_batch_tiles
    cur = lambda i: jnp.minimum(i, n_steps - 1)
    prev = lambda i: jnp.maximum(i - 1, 0)
    tile = lambda pick: pl.BlockSpec(
        (None, tq, D_MODEL), lambda i, *_: (pick(i) // per_batch_tiles, pick(i) % per_batch_tiles, 0))
    per_batch = lambda arr: pl.BlockSpec(
        (None,) + arr.shape[1:], lambda i, *_: (cur(i) // per_batch_tiles,) + (0,) * (arr.ndim - 1))
    return pl.pallas_call(
        functools.partial(_attn_kernel, n_blocks, n_steps),
        out_shape=jax.ShapeDtypeStruct(x.shape, F32),
        grid_spec=pltpu.PrefetchScalarGridSpec(
            num_scalar_prefetch=1,
            grid=(n_steps + 1,),
            in_specs=[
                tile(cur),
                per_batch(k), per_batch(vt), per_batch(kc), per_batch(vct),
                tile(prev),
                pl.BlockSpec((None, N_MOD, D_MODEL), lambda i, *_: (prev(i) // per_batch_tiles, 0, 0)),
                _resident((D_MODEL, D_MODEL)),
                _resident((1, D_MODEL)),
                _resident((1, D_MODEL)),
            ],
            out_specs=tile(prev),
            scratch_shapes=[pltpu.VMEM((tq, Q_WIDTH), BF16)],
        ),
        compiler_params=pltpu.CompilerParams(dimension_semantics=("arbitrary",), vmem_limit_bytes=VMEM_LIMIT),
        name="attn",
    )(sink, q, k, vt, kc, vct, x, mod, wo, g.reshape(1, D_MODEL), b.reshape(1, D_MODEL))


def _reverse_shift(t, first_row, flip_ref):
    rows = t.shape[0]
    blocks = [jnp.dot(flip_ref[...], t[r0:r0 + FLIP, :], preferred_element_type=F32)
              for r0 in range(rows - FLIP, -FLIP, -FLIP)]
    shifted = pltpu.roll(jnp.concatenate(blocks, axis=0), 1, 0)
    is_first = lax.broadcasted_iota(jnp.int32, shifted.shape, 0) == 0
    return jnp.where(is_first, first_row, shifted)


def _channel_dft(t, w_ref):
    return jnp.concatenate(
        [jnp.dot(t[:, g0:g0 + FOURIER_GROUP_CH], w_ref[...], preferred_element_type=F32)
         for g0 in range(0, D_MODEL, FOURIER_GROUP_CH)], axis=1)


def _dft_fold_kernel(lo_ref, up_ref, mod_ref, cc_ref, sc_ref, flip_ref, ab_ref, an_ref, carry_ref):
    @pl.when(pl.program_id(1) == 0)
    def _():
        carry_ref[...] = jnp.zeros_like(carry_ref)

    h_lo = _modulate(lo_ref[...], mod_ref, 1)
    h_up = _modulate(up_ref[...], mod_ref, 1).astype(BF16)
    partner = _reverse_shift(h_up, carry_ref[0:1, :], flip_ref)
    carry_ref[...] = h_up[0:8, :].astype(F32)
    ab_ref[0] = _channel_dft((h_lo + partner).astype(BF16), cc_ref).astype(BF16)
    ab_ref[1] = _channel_dft((h_lo - partner).astype(BF16), sc_ref).astype(BF16)
    an_ref[...] = _channel_dft(h_up[0:8, :], cc_ref)


def _dft_fold_call(x4, mod, cc, sc, flip, tm):
    bsz, _, half, _ = x4.shape
    n_t = half // tm
    ch = FOURIER_GROUP_CH
    return pl.pallas_call(
        _dft_fold_kernel,
        out_shape=(jax.ShapeDtypeStruct((bsz, 2, half, D_MODEL), BF16),
                   jax.ShapeDtypeStruct((bsz, 8, D_MODEL), F32)),
        grid=(bsz, n_t),
        in_specs=[
            pl.BlockSpec((None, None, tm, D_MODEL), lambda bi, t: (bi, 0, t, 0)),
            pl.BlockSpec((None, None, tm, D_MODEL), lambda bi, t: (bi, 1, n_t - 1 - t, 0)),
            pl.BlockSpec((None, N_MOD, D_MODEL), lambda bi, t: (bi, 0, 0)),
            _resident((ch, ch)),
            _resident((ch, ch)),
            _resident((FLIP, FLIP)),
        ],
        out_specs=(pl.BlockSpec((None, 2, tm, D_MODEL), lambda bi, t: (bi, 0, t, 0)),
                   pl.BlockSpec((None, 8, D_MODEL), lambda bi, t: (bi, 0, 0))),
        scratch_shapes=[pltpu.VMEM((8, D_MODEL), F32)],
        compiler_params=pltpu.CompilerParams(dimension_semantics=("parallel", "arbitrary"),
                                             vmem_limit_bytes=VMEM_LIMIT),
        name="dft_fold",
    )(x4, x4, mod, cc, sc, flip)


def _dft_seq_kernel(norm, n_t, wlo_ref, wup_ref, ab_ref, an_ref, alt_ref, flip_ref, x_ref, mod_ref, wo_ref,
                    g_ref, beta_ref, out_ref, carry_ref):
    t = pl.program_id(1)
    tk = wlo_ref.shape[0]
    ab = ab_ref[...]
    a_n = an_ref[0:1, :]
    row = lax.broadcasted_iota(jnp.int32, (tk, 1), 0)

    def alternating(k0):
        return (1 - 2 * ((k0 + row) & 1)).astype(F32)

    f_lo = (jnp.dot(wlo_ref[...], ab, preferred_element_type=F32) + alternating(t * tk) * a_n) * norm
    f_mir = (jnp.dot(wup_ref[...], ab, preferred_element_type=F32) + alternating((n_t - 1 - t) * tk) * a_n) * norm
    f_mir = f_mir.astype(BF16)

    @pl.when(t == 0)
    def _():
        f_n = (jnp.dot(alt_ref[...], ab, preferred_element_type=F32)[0:1, :] + a_n) * norm
        carry_ref[0:1, :] = f_n.astype(BF16).astype(F32)

    f_up = _reverse_shift(f_mir, carry_ref[0:1, :], flip_ref)
    carry_ref[...] = f_mir[0:8, :].astype(F32)
    y_lo = jnp.dot(f_lo.astype(BF16), wo_ref[...], preferred_element_type=F32)
    y_up = jnp.dot(f_up.astype(BF16), wo_ref[...], preferred_element_type=F32)
    out_ref[0] = _post_norm(x_ref[0], y_lo, mod_ref, 1, g_ref, beta_ref)
    out_ref[1] = _post_norm(x_ref[1], y_up, mod_ref, 1, g_ref, beta_ref)


def _dft_seq_call(w_lo, w_up, ab, a_n, alt, flip, x4, mod, wo, g, beta, tk):
    bsz, _, half, _ = x4.shape
    n_t = half // tk
    norm = float(1.0 / np.sqrt(2 * half * FOURIER_GROUP_CH))
    tok = pl.BlockSpec((None, 2, tk, D_MODEL), lambda bi, t: (bi, 0, t, 0))
    return pl.pallas_call(
        functools.partial(_dft_seq_kernel, norm, n_t),
        out_shape=jax.ShapeDtypeStruct(x4.shape, F32),
        grid=(bsz, n_t),
        in_specs=[
            pl.BlockSpec((tk, 2 * half), lambda bi, t: (t, 0)),
            pl.BlockSpec((tk, 2 * half), lambda bi, t: (n_t - 1 - t, 0)),
            pl.BlockSpec((None, 2 * half, D_MODEL), lambda bi, t: (bi, 0, 0)),
            pl.BlockSpec((None, 8, D_MODEL), lambda bi, t: (bi, 0, 0)),
            _resident((8, 2 * half)),
            _resident((FLIP, FLIP)),
            tok,
            pl.BlockSpec((None, N_MOD, D_MODEL), lambda bi, t: (bi, 0, 0)),
            _resident((D_MODEL, D_MODEL)),
            _resident((1, D_MODEL)),
            _resident((1, D_MODEL)),
        ],
        out_specs=tok,
        scratch_shapes=[pltpu.VMEM((8, D_MODEL), F32)],
        compiler_params=pltpu.CompilerParams(dimension_semantics=("parallel", "arbitrary"),
                                             vmem_limit_bytes=VMEM_LIMIT),
        name="dft_seq",
    )(w_lo, w_up, ab, a_n, alt, flip, x4, mod, wo, g.reshape(1, D_MODEL), beta.reshape(1, D_MODEL))


def _rope_tables(n_tokens):
    rows = n_tokens // GRID_W
    row = jnp.repeat(jnp.arange(rows), GRID_W).astype(F32)
    col = jnp.tile(jnp.arange(GRID_W), rows).astype(F32)
    inv = ROPE_BASE ** (-jnp.arange(0, AXIS_DIM, 2, dtype=F32) / AXIS_DIM)
    ang_r, ang_c = row[:, None] * inv, col[:, None] * inv
    cos_r, sin_r, cos_c, sin_c = jnp.cos(ang_r), jnp.sin(ang_r), jnp.cos(ang_c), jnp.sin(ang_c)
    zero = jnp.zeros_like(sin_r)
    cos_h = jnp.concatenate([cos_r, cos_r, cos_c, cos_c], axis=-1)
    sa_h = jnp.concatenate([-sin_r, zero, -sin_c, zero], axis=-1)
    sb_h = jnp.concatenate([zero, sin_r, zero, sin_c], axis=-1)
    rep = LANES // HEAD_DIM
    return tuple(jnp.tile(t, (1, rep)) for t in (cos_h, sa_h, sb_h))


def _dft_tables(n, rows):
    idx = jnp.arange(rows, dtype=jnp.int32)
    ang = ((idx[:, None] * idx[None, :]) % n).astype(F32) * (2.0 * np.pi / n)
    return jnp.cos(ang), jnp.sin(ang)


def kernel(x, c, ctx, c_ctx, mod_w, mod_b, ln_g, ln_b, ffn_wi, ffn_wo,
           attn_wqkv, attn_bqkv, attn_wo, attn_sink, fourier_wo):
    bsz, seq, _ = x.shape
    n_ctx = ctx.shape[1]
    assert DEPTH == 2 and seq % BLOCK == 0 and seq % GRID_W == 0

    cv = jnp.concatenate([c, c_ctx[None, :], jnp.zeros((MOD_ROWS - bsz - 1, D_MODEL), F32)], axis=0)
    mod = _mod_call(cv, mod_w, mod_b).reshape(DEPTH, MOD_ROWS, N_MOD, D_MODEL)
    wi = ffn_wi.astype(BF16)
    wo = ffn_wo.astype(BF16)

    mod_lat, mod_ctx = mod[0, :bsz], mod[0, bsz:bsz + 1]
    g, b = ln_g[0], ln_b[0]
    x = _ffn_call(x, mod_lat, 0, wi, wo, 0, 0, g[0], b[0], tm=FFN_TILE)
    ctx_s = _ffn_call(ctx.reshape(1, bsz * n_ctx, D_MODEL), mod_ctx, 0, wi, wo, 0, 0, g[0], b[0], tm=FFN_TILE)

    w_qkv = attn_wqkv[0].astype(BF16)
    cos_t, sa_t, sb_t = _rope_tables(seq)
    q, k, vt = _qkv_call(x, mod_lat, w_qkv, attn_bqkv[0], cos_t, sa_t, sb_t, tm=512)
    kc, vct = _kv_ctx_call(ctx_s.reshape(bsz, n_ctx, D_MODEL), mod_ctx, w_qkv[:, Q_WIDTH:], attn_bqkv[0, Q_WIDTH:])
    x = _attn_call(q, k, vt, kc, vct, attn_sink[0], x, mod_lat, attn_wo[0].astype(BF16), g[1], b[1], tq=4 * BLOCK)
    x = _ffn_call(x, mod_lat, 2, wi, wo, 0, 1, g[2], b[2], tm=FFN_TILE)

    mod_lat = mod[1, :bsz]
    g, b = ln_g[1], ln_b[1]
    x = _ffn_call(x, mod_lat, 0, wi, wo, 1, 0, g[0], b[0], tm=FFN_TILE)
    half = seq // 2
    cos_c, sin_c = _dft_tables(FOURIER_GROUP_CH, FOURIER_GROUP_CH)
    cos_s, sin_s = _dft_tables(seq, half)
    w_lo = jnp.concatenate([cos_s, -sin_s], axis=1).astype(BF16)
    w_up = jnp.concatenate([cos_s, sin_s], axis=1).astype(BF16)
    alt = jnp.zeros((8, seq), F32).at[0, :half].set(1.0 - 2.0 * (jnp.arange(half) % 2)).astype(BF16)
    anti = jnp.arange(FLIP)[:, None] + jnp.arange(FLIP)[None, :] == FLIP - 1
    flip = anti.astype(BF16)
    x4 = x.reshape(bsz, 2, half, D_MODEL)
    ab, a_n = _dft_fold_call(x4, mod_lat, cos_c.astype(BF16), sin_c.astype(BF16), flip, tm=512)
    x4 = _dft_seq_call(w_lo, w_up, ab.reshape(bsz, seq, D_MODEL), a_n, alt, flip, x4, mod_lat,
                       fourier_wo[0].astype(BF16), g[1], b[1], tk=512)
    x = x4.reshape(bsz, seq, D_MODEL)
    x = _ffn_call(x, mod_lat, 2, wi, wo, 1, 1, g[2], b[2], tm=FFN_TILE)
    return x
```

```python
import functools

import jax
import jax.numpy as jnp
import numpy as np
from jax import lax
from jax.experimental import pallas as pl
from jax.experimental.pallas import tpu as pltpu

D_MODEL = 1024
DEPTH = 2
GRID_W = 64
N_HEADS = 16
N_KV_HEADS = 4
HEAD_DIM = 64
GROUP = N_HEADS // N_KV_HEADS
Q_WIDTH = N_HEADS * HEAD_DIM
KV_WIDTH = N_KV_HEADS * HEAD_DIM
QKV_WIDTH = Q_WIDTH + 2 * KV_WIDTH
KV_DUP_WIDTH = 2 * KV_WIDTH
WINDOW = 128
BLOCK = 128
SPAN = BLOCK + 2 * WINDOW
ROPE_BASE = 10000.0
AXIS_DIM = HEAD_DIM // 2
FOURIER_GROUPS = 4
FOURIER_GROUP_CH = D_MODEL // FOURIER_GROUPS
D_FF = 2816
N_MOD = 9
LN_EPS = 1e-5
ALPHA = (2.0 * DEPTH) ** 0.25
NEG_INF = -1e30
LOG2E = float(np.log2(np.e))

LANES = 128
MOD_ROWS = 24
VMEM_LIMIT = 60 * 1024 * 1024
FFN_TILE = 1024
FFN_CHUNKS = (1024, 1024, 768)
FLIP = 256

BF16 = jnp.bfloat16
F32 = jnp.float32


def _params(n_axes):
    return pltpu.CompilerParams(dimension_semantics=("parallel",) * n_axes,
                                vmem_limit_bytes=VMEM_LIMIT)


def _resident(shape):
    return pl.BlockSpec(shape, lambda *_: (0,) * len(shape), pipeline_mode=pl.Buffered(1))


def _layer_norm(z, g, b):
    mu = jnp.mean(z, axis=-1, keepdims=True)
    d = z - mu
    var = jnp.mean(d * d, axis=-1, keepdims=True)
    return d * lax.rsqrt(var + LN_EPS) * g + b


def _modulate(x, mod_ref, j):
    shift = mod_ref[3 * j:3 * j + 1, :]
    scale = mod_ref[3 * j + 1:3 * j + 2, :]
    return x * (1.0 + scale) + shift


def _post_norm(x, y, mod_ref, j, g_ref, b_ref):
    gate = mod_ref[3 * j + 2:3 * j + 3, :]
    return _layer_norm(ALPHA * x + gate * y, g_ref[...], b_ref[...])


def _mod_kernel(cv_ref, w_ref, b_ref, o_ref):
    cv = cv_ref[...]
    s = (cv * jax.nn.sigmoid(cv)).astype(BF16)
    o_ref[...] = jnp.dot(s, w_ref[...].astype(BF16), preferred_element_type=F32) + b_ref[...]


def _mod_call(cv, mod_w, mod_b):
    tn = D_MODEL
    n_out = N_MOD * D_MODEL
    return pl.pallas_call(
        _mod_kernel,
        out_shape=jax.ShapeDtypeStruct((DEPTH, MOD_ROWS, n_out), F32),
        grid=(DEPTH, n_out // tn),
        in_specs=[
            pl.BlockSpec((MOD_ROWS, D_MODEL), lambda i, n: (0, 0)),
            pl.BlockSpec((None, D_MODEL, tn), lambda i, n: (i, 0, n)),
            pl.BlockSpec((None, 1, tn), lambda i, n: (i, 0, n)),
        ],
        out_specs=pl.BlockSpec((None, MOD_ROWS, tn), lambda i, n: (i, 0, n)),
        compiler_params=_params(2),
        name="mod",
    )(cv, mod_w, mod_b.reshape(DEPTH, 1, n_out))


def _ffn_kernel(j, n_tiles, x_ref, mod_ref, wi_ref, wo_ref, g_ref, b_ref, o_ref, u_ref, z_ref):
    i = pl.program_id(0)

    @pl.when(i == 0)
    def _():
        z_ref[...] = jnp.zeros_like(z_ref)

    def norm_previous():
        o_ref[...] = _layer_norm(z_ref[...], g_ref[...], b_ref[...])

    @pl.when(i < n_tiles)
    def _():
        norm_previous()
        h = _modulate(x_ref[...], mod_ref, j).astype(BF16)
        start = 0
        for width in FFN_CHUNKS:
            a = jnp.dot(h, wi_ref[:, start:start + width], preferred_element_type=F32)
            g = jnp.dot(h, wi_ref[:, D_FF + start:D_FF + start + width], preferred_element_type=F32)
            u_ref[:, start:start + width] = (a * (g * jax.nn.sigmoid(g))).astype(BF16)
            start += width
        y = 0.5 * jnp.dot(u_ref[...], wo_ref[...], preferred_element_type=F32)
        z_ref[...] = ALPHA * x_ref[...] + mod_ref[3 * j + 2:3 * j + 3, :] * y

    @pl.when(i == n_tiles)
    def _():
        norm_previous()


def _ffn_call(x, mod, j, wi_all, wo_all, layer, slot, g, b, tm):
    bsz, length, _ = x.shape
    per_batch = length // tm
    n_tiles = bsz * per_batch
    cur = lambda i: jnp.minimum(i, n_tiles - 1)
    prev = lambda i: jnp.maximum(i - 1, 0)
    tile = lambda pick: pl.BlockSpec((None, tm, D_MODEL), lambda i: (pick(i) // per_batch, pick(i) % per_batch, 0))
    weight = lambda rows, cols: pl.BlockSpec((None, None, rows, cols), lambda i: (layer, slot, 0, 0),
                                             pipeline_mode=pl.Buffered(1))
    return pl.pallas_call(
        functools.partial(_ffn_kernel, j, n_tiles),
        out_shape=jax.ShapeDtypeStruct(x.shape, F32),
        grid=(n_tiles + 1,),
        in_specs=[
            tile(cur),
            pl.BlockSpec((None, N_MOD, D_MODEL), lambda i: (cur(i) // per_batch, 0, 0)),
            weight(D_MODEL, 2 * D_FF),
            weight(D_FF, D_MODEL),
            _resident((1, D_MODEL)),
            _resident((1, D_MODEL)),
        ],
        out_specs=tile(prev),
        scratch_shapes=[pltpu.VMEM((tm, D_FF), BF16), pltpu.VMEM((tm, D_MODEL), F32)],
        compiler_params=pltpu.CompilerParams(dimension_semantics=("arbitrary",), vmem_limit_bytes=VMEM_LIMIT),
        name="ffn",
    )(x, mod, wi_all, wo_all, g.reshape(1, D_MODEL), b.reshape(1, D_MODEL))


def _rope(t, cos_ref, sa_ref, sb_ref):
    half = AXIS_DIM // 2
    outs = []
    for c in range(t.shape[1] // LANES):
        tc = t[:, c * LANES:(c + 1) * LANES]
        up = pltpu.roll(tc, LANES - half, 1)
        down = pltpu.roll(tc, half, 1)
        outs.append(tc * cos_ref[...] + up * sa_ref[...] + down * sb_ref[...])
    return jnp.concatenate(outs, axis=1)


def _dup_heads(t):
    pieces = []
    for h in range(t.shape[1] // HEAD_DIM):
        head = t[:, h * HEAD_DIM:(h + 1) * HEAD_DIM]
        pieces += [head, head]
    return jnp.concatenate(pieces, axis=1)


def _qkv_kernel(x_ref, mod_ref, w_ref, b_ref, cos_ref, sa_ref, sb_ref, q_ref, k_ref, vt_ref):
    h = _modulate(x_ref[...], mod_ref, 1).astype(BF16)
    qkv = jnp.dot(h, w_ref[...], preferred_element_type=F32) + b_ref[...]
    q = qkv[:, :Q_WIDTH] * (HEAD_DIM ** -0.5 * LOG2E)
    q_ref[...] = _rope(q, cos_ref, sa_ref, sb_ref).astype(BF16)
    k = _rope(qkv[:, Q_WIDTH:Q_WIDTH + KV_WIDTH], cos_ref, sa_ref, sb_ref)
    k_ref[...] = _dup_heads(k).astype(BF16)
    v = _dup_heads(qkv[:, Q_WIDTH + KV_WIDTH:])
    for blk in range(vt_ref.shape[0]):
        vt_ref[blk] = v[blk * BLOCK:(blk + 1) * BLOCK, :].T.astype(BF16)


def _qkv_call(x, mod, w, b, cos_t, sa_t, sb_t, tm):
    bsz, length, _ = x.shape
    tok = lambda width: pl.BlockSpec((None, tm, width), lambda bi, t: (bi, t, 0))
    table = pl.BlockSpec((tm, LANES), lambda bi, t: (t, 0))
    return pl.pallas_call(
        _qkv_kernel,
        out_shape=(jax.ShapeDtypeStruct((bsz, length, Q_WIDTH), BF16),
                   jax.ShapeDtypeStruct((bsz, length, KV_DUP_WIDTH), BF16),
                   jax.ShapeDtypeStruct((bsz, length // BLOCK, KV_DUP_WIDTH, BLOCK), BF16)),
        grid=(bsz, length // tm),
        in_specs=[
            tok(D_MODEL),
            pl.BlockSpec((None, N_MOD, D_MODEL), lambda bi, t: (bi, 0, 0)),
            _resident((D_MODEL, QKV_WIDTH)),
            _resident((1, QKV_WIDTH)),
            table, table, table,
        ],
        out_specs=(tok(Q_WIDTH), tok(KV_DUP_WIDTH),
                   pl.BlockSpec((None, tm // BLOCK, KV_DUP_WIDTH, BLOCK), lambda bi, t: (bi, t, 0, 0))),
        compiler_params=_params(2),
        name="qkv",
    )(x, mod, w, b.reshape(1, QKV_WIDTH), cos_t, sa_t, sb_t)


def _kv_ctx_kernel(x_ref, mod_ref, w_ref, b_ref, k_ref, vt_ref):
    h = _modulate(x_ref[...], mod_ref, 1).astype(BF16)
    kv = jnp.dot(h, w_ref[...], preferred_element_type=F32) + b_ref[...]
    k_ref[...] = _dup_heads(kv[:, :KV_WIDTH]).astype(BF16)
    vt_ref[...] = _dup_heads(kv[:, KV_WIDTH:]).T.astype(BF16)


def _kv_ctx_call(x, mod, w, b):
    bsz, n_ctx, _ = x.shape
    return pl.pallas_call(
        _kv_ctx_kernel,
        out_shape=(jax.ShapeDtypeStruct((bsz, n_ctx, KV_DUP_WIDTH), BF16),
                   jax.ShapeDtypeStruct((bsz, KV_DUP_WIDTH, n_ctx), BF16)),
        grid=(bsz,),
        in_specs=[
            pl.BlockSpec((None, n_ctx, D_MODEL), lambda bi: (bi, 0, 0)),
            _resident((None, N_MOD, D_MODEL)),
            _resident((D_MODEL, 2 * KV_WIDTH)),
            _resident((1, 2 * KV_WIDTH)),
        ],
        out_specs=(pl.BlockSpec((None, n_ctx, KV_DUP_WIDTH), lambda bi: (bi, 0, 0)),
                   pl.BlockSpec((None, KV_DUP_WIDTH, n_ctx), lambda bi: (bi, 0, 0))),
        compiler_params=_params(1),
        name="kv_ctx",
    )(x, mod, w, b.reshape(1, 2 * KV_WIDTH))


def _attn_kernel(n_blocks, n_steps, sink_ref, q_ref, k_ref, vt_ref, kc_ref, vct_ref, xp_ref, modp_ref, wo_ref,
                 g_ref, b_ref, out_ref, o_ref):
    i = pl.program_id(0)

    @pl.when(i == 0)
    def _():
        o_ref[...] = jnp.zeros_like(o_ref)

    def project(o_prev):
        return jnp.dot(o_prev, wo_ref[...], preferred_element_type=F32)

    def norm(y):
        out_ref[...] = _post_norm(xp_ref[...], y, modp_ref, 1, g_ref, b_ref)

    @pl.when(i < n_steps)
    def _():
        o_prev = o_ref[...]
        y = []
        tile_in_batch = i % (n_blocks * BLOCK // q_ref.shape[0])
        _attend(n_blocks, tile_in_batch, sink_ref, q_ref, k_ref, vt_ref, kc_ref, vct_ref, o_ref,
                after_slot={9: lambda: y.append(project(o_prev)), 21: lambda: norm(y[0])})

    @pl.when(i == n_steps)
    def _():
        norm(project(o_ref[...]))


def _attend(n_blocks, tile_in_batch, sink_ref, q_ref, k_ref, vt_ref, kc_ref, vct_ref, o_ref, after_slot):
    n_sub = q_ref.shape[0] // BLOCK
    qb0 = tile_in_batch * n_sub
    cols2 = 2 * BLOCK
    key_i = lax.broadcasted_iota(jnp.int32, (BLOCK, cols2), 0)
    qry_i = lax.broadcasted_iota(jnp.int32, (BLOCK, cols2), 1) & (BLOCK - 1)
    first_head = lax.broadcasted_iota(jnp.int32, (1, cols2), 1) < BLOCK
    lo_q = lax.broadcasted_iota(jnp.int32, (cols2, LANES), 1) < HEAD_DIM
    lo_v = lax.broadcasted_iota(jnp.int32, (LANES, 1), 0) < HEAD_DIM

    def block_ids(sub):
        return [jnp.clip(qb0 + sub + d, 0, n_blocks - 1) for d in (-1, 0, 1)]

    slots_per_block = 2 * N_KV_HEADS
    p_t, inv = {}, {}

    def scores(slot):
        sub, rest = divmod(slot, slots_per_block)
        h, half = divmod(rest, 2)
        c0 = 2 * h * LANES
        q_rows = slice(sub * BLOCK, (sub + 1) * BLOCK)
        q_cat = jnp.concatenate([q_ref[q_rows, c0:c0 + LANES], q_ref[q_rows, c0 + LANES:c0 + 2 * LANES]], axis=0)
        keep = lo_q if half == 0 else jnp.logical_not(lo_q)
        q_sel = jnp.where(keep, q_cat, jnp.zeros_like(q_cat))
        lanes = slice(h * LANES, (h + 1) * LANES)
        k_h = jnp.concatenate([k_ref[pl.ds(pl.multiple_of(blk * BLOCK, BLOCK), BLOCK), lanes]
                               for blk in block_ids(sub)] + [kc_ref[:, lanes]], axis=0)
        return lax.dot_general(k_h, q_sel, (((1,), (1,)), ((), ())), preferred_element_type=F32)

    def softmax(slot, s_t):
        sub, rest = divmod(slot, slots_per_block)
        h, half = divmod(rest, 2)
        left_ok = (qry_i <= key_i) & (qb0 + sub > 0)
        right_ok = (key_i <= qry_i) & (qb0 + sub < n_blocks - 1)
        sink = jnp.where(first_head, sink_ref[4 * h + half], sink_ref[4 * h + 2 + half]) * LOG2E
        parts = [jnp.where(left_ok, s_t[:BLOCK], NEG_INF),
                 s_t[BLOCK:2 * BLOCK],
                 jnp.where(right_ok, s_t[2 * BLOCK:SPAN], NEG_INF)]
        parts += [s_t[c:c + BLOCK] for c in range(SPAN, s_t.shape[0], BLOCK)]
        m = jnp.max(functools.reduce(jnp.maximum, parts), axis=0, keepdims=True)
        m = jnp.maximum(m, sink)
        p = [jnp.exp2(t - m) for t in parts]
        denom = jnp.sum(functools.reduce(jnp.add, p), axis=0, keepdims=True) + jnp.exp2(sink - m)
        inv[slot] = 1.0 / denom
        p_t[slot] = jnp.concatenate(p, axis=0).astype(BF16)

    def weighted_values(pair):
        sub, h = divmod(pair, N_KV_HEADS)
        rows = slice(h * LANES, (h + 1) * LANES)
        v_t = jnp.concatenate([vt_ref[blk, rows, :] for blk in block_ids(sub)] + [vct_ref[rows, :]], axis=1)
        v_lo = jnp.where(lo_v, v_t, jnp.zeros_like(v_t))
        v_hi = jnp.where(lo_v, jnp.zeros_like(v_t), v_t)
        acc_t = (jnp.dot(v_lo, p_t.pop(2 * pair), preferred_element_type=F32)
                 + jnp.dot(v_hi, p_t.pop(2 * pair + 1), preferred_element_type=F32))
        out = (acc_t * jnp.where(lo_v, inv.pop(2 * pair), inv.pop(2 * pair + 1))).T.astype(BF16)
        c0 = 2 * h * LANES
        q_rows = slice(sub * BLOCK, (sub + 1) * BLOCK)
        o_ref[q_rows, c0:c0 + LANES] = out[:BLOCK]
        o_ref[q_rows, c0 + LANES:c0 + 2 * LANES] = out[BLOCK:]

    n_slots = n_sub * slots_per_block
    ahead = 1
    s_t = {slot: scores(slot) for slot in range(ahead)}
    for slot in range(n_slots):
        if slot + ahead < n_slots:
            s_t[slot + ahead] = scores(slot + ahead)
        softmax(slot, s_t.pop(slot))
        if slot % 2 == 0 and slot > 0:
            weighted_values(slot // 2 - 1)
        if slot in after_slot:
            after_slot[slot]()
    weighted_values(n_slots // 2 - 1)


def _attn_call(q, k, vt, kc, vct, sink, x, mod, wo, g, b, tq):
    bsz, length, _ = q.shape
    n_blocks = length // BLOCK
    per_batch_tiles = length // tq
    n_steps = bsz * per_batch_tiles
    cur = lambda i: jnp.minimum(i, n_steps - 1)
    prev = lambda i: jnp.maximum(i - 1, 0)
    tile = lambda pick: pl.BlockSpec(
        (None, tq, D_MODEL), lambda i, *_: (pick(i) // per_batch_tiles, pick(i) % per_batch_tiles, 0))
    per_batch = lambda arr: pl.BlockSpec(
        (None,) + arr.shape[1:], lambda i, *_: (cur(i) // per_batch_tiles,) + (0,) * (arr.ndim - 1))
    return pl.pallas_call(
        functools.partial(_attn_kernel, n_blocks, n_steps),
        out_shape=jax.ShapeDtypeStruct(x.shape, F32),
        grid_spec=pltpu.PrefetchScalarGridSpec(
            num_scalar_prefetch=1,
            grid=(n_steps + 1,),
            in_specs=[
                tile(cur),
                per_batch(k), per_batch(vt), per_batch(kc), per_batch(vct),
                tile(prev),
                pl.BlockSpec((None, N_MOD, D_MODEL), lambda i, *_: (prev(i) // per_batch_tiles, 0, 0)),
                _resident((D_MODEL, D_MODEL)),
                _resident((1, D_MODEL)),
                _resident((1, D_MODEL)),
            ],
            out_specs=tile(prev),
            scratch_shapes=[pltpu.VMEM((tq, Q_WIDTH), BF16)],
        ),
        compiler_params=pltpu.CompilerParams(dimension_semantics=("arbitrary",), vmem_limit_bytes=VMEM_LIMIT),
        name="attn",
    )(sink, q, k, vt, kc, vct, x, mod, wo, g.reshape(1, D_MODEL), b.reshape(1, D_MODEL))


def _reverse_shift(t, first_row, flip_ref):
    rows = t.shape[0]
    blocks = [jnp.dot(flip_ref[...], t[r0:r0 + FLIP, :], preferred_element_type=F32)
              for r0 in range(rows - FLIP, -FLIP, -FLIP)]
    shifted = pltpu.roll(jnp.concatenate(blocks, axis=0), 1, 0)
    is_first = lax.broadcasted_iota(jnp.int32, shifted.shape, 0) == 0
    return jnp.where(is_first, first_row, shifted)


def _channel_dft(t, w_ref):
    return jnp.concatenate(
        [jnp.dot(t[:, g0:g0 + FOURIER_GROUP_CH], w_ref[...], preferred_element_type=F32)
         for g0 in range(0, D_MODEL, FOURIER_GROUP_CH)], axis=1)


def _dft_fold_kernel(lo_ref, up_ref, mod_ref, cc_ref, sc_ref, flip_ref, ab_ref, an_ref, carry_ref, alt_ref):
    @pl.when(pl.program_id(1) == 0)
    def _():
        carry_ref[...] = jnp.zeros_like(carry_ref)
        alt_ref[...] = jnp.zeros_like(alt_ref)

    h_lo = _modulate(lo_ref[...], mod_ref, 1)
    h_up = _modulate(up_ref[...], mod_ref, 1).astype(BF16)
    partner = _reverse_shift(h_up, carry_ref[0:1, :], flip_ref)
    carry_ref[...] = h_up[0:8, :].astype(F32)
    ae = _channel_dft((h_lo + partner).astype(BF16), cc_ref).astype(BF16)
    ab_ref[0] = ae
    ab_ref[1] = _channel_dft((h_lo - partner).astype(BF16), sc_ref).astype(BF16)
    sign = (1 - 2 * (lax.broadcasted_iota(jnp.int32, (ae.shape[0], 1), 0) & 1)).astype(F32)
    alt_ref[0:1, :] += jnp.sum(ae.astype(F32) * sign, axis=0, keepdims=True)
    a_n = _channel_dft(h_up[0:8, :], cc_ref)
    an_ref[...] = jnp.where(lax.broadcasted_iota(jnp.int32, a_n.shape, 0) == 1, alt_ref[0:1, :], a_n)


def _dft_fold_call(x4, mod, cc, sc, flip, tm):
    bsz, _, half, _ = x4.shape
    n_t = half // tm
    ch = FOURIER_GROUP_CH
    return pl.pallas_call(
        _dft_fold_kernel,
        out_shape=(jax.ShapeDtypeStruct((bsz, 2, half, D_MODEL), BF16),
                   jax.ShapeDtypeStruct((bsz, 8, D_MODEL), F32)),
        grid=(bsz, n_t),
        in_specs=[
            pl.BlockSpec((None, None, tm, D_MODEL), lambda bi, t: (bi, 0, t, 0)),
            pl.BlockSpec((None, None, tm, D_MODEL), lambda bi, t: (bi, 1, n_t - 1 - t, 0)),
            pl.BlockSpec((None, N_MOD, D_MODEL), lambda bi, t: (bi, 0, 0)),
            _resident((ch, ch)),
            _resident((ch, ch)),
            _resident((FLIP, FLIP)),
        ],
        out_specs=(pl.BlockSpec((None, 2, tm, D_MODEL), lambda bi, t: (bi, 0, t, 0)),
                   pl.BlockSpec((None, 8, D_MODEL), lambda bi, t: (bi, 0, 0))),
        scratch_shapes=[pltpu.VMEM((8, D_MODEL), F32), pltpu.VMEM((8, D_MODEL), F32)],
        compiler_params=pltpu.CompilerParams(dimension_semantics=("parallel", "arbitrary"),
                                             vmem_limit_bytes=VMEM_LIMIT),
        name="dft_fold",
    )(x4, x4, mod, cc, sc, flip)


def _dft_seq_kernel(norm, n_t, n_steps, wlo_ref, wup_ref, ab_ref, an_ref, flip_ref, x_ref, mod_ref, wo_ref,
                    g_ref, beta_ref, out_ref, carry_ref, z_ref):
    i = pl.program_id(0)
    t = i % n_t
    tk = wlo_ref.shape[0]

    @pl.when(i == 0)
    def _():
        z_ref[...] = jnp.zeros_like(z_ref)

    @pl.when(t == 0)
    def _():
        f_n = (an_ref[1:2, :] + an_ref[0:1, :]) * norm
        carry_ref[0:1, :] = f_n.astype(BF16).astype(F32)

    def norm_previous():
        out_ref[0] = _layer_norm(z_ref[0], g_ref[...], beta_ref[...])
        out_ref[1] = _layer_norm(z_ref[1], g_ref[...], beta_ref[...])

    @pl.when(i < n_steps)
    def _():
        norm_previous()
        ab = ab_ref[...]
        a_n = an_ref[0:1, :]
        row = lax.broadcasted_iota(jnp.int32, (tk, 1), 0)

        def alternating(k0):
            return (1 - 2 * ((k0 + row) & 1)).astype(F32)

        f_mir = (jnp.dot(wup_ref[...], ab, preferred_element_type=F32)
                 + alternating((n_t - 1 - t) * tk) * a_n) * norm
        f_mir = f_mir.astype(BF16)
        f_lo = (jnp.dot(wlo_ref[...], ab, preferred_element_type=F32) + alternating(t * tk) * a_n) * norm
        f_up = _reverse_shift(f_mir, carry_ref[0:1, :], flip_ref)
        carry_ref[...] = f_mir[0:8, :].astype(F32)
        gate = mod_ref[3 * 1 + 2:3 * 1 + 3, :]
        y_lo = jnp.dot(f_lo.astype(BF16), wo_ref[...], preferred_element_type=F32)
        z_ref[0] = ALPHA * x_ref[0] + gate * y_lo
        y_up = jnp.dot(f_up.astype(BF16), wo_ref[...], preferred_element_type=F32)
        z_ref[1] = ALPHA * x_ref[1] + gate * y_up

    @pl.when(i == n_steps)
    def _():
        norm_previous()


def _dft_seq_call(w_lo, w_up, ab, a_n, flip, x4, mod, wo, g, beta, tk):
    bsz, _, half, _ = x4.shape
    n_t = half // tk
    n_steps = bsz * n_t
    norm = float(1.0 / np.sqrt(2 * half * FOURIER_GROUP_CH))
    cur = lambda i: jnp.minimum(i, n_steps - 1)
    prev = lambda i: jnp.maximum(i - 1, 0)
    tok = lambda pick: pl.BlockSpec((None, 2, tk, D_MODEL), lambda i: (pick(i) // n_t, 0, pick(i) % n_t, 0))
    per_batch = lambda rows: pl.BlockSpec((None, rows, D_MODEL), lambda i: (cur(i) // n_t, 0, 0))
    return pl.pallas_call(
        functools.partial(_dft_seq_kernel, norm, n_t, n_steps),
        out_shape=jax.ShapeDtypeStruct(x4.shape, F32),
        grid=(n_steps + 1,),
        in_specs=[
            pl.BlockSpec((tk, 2 * half), lambda i: (cur(i) % n_t, 0)),
            pl.BlockSpec((tk, 2 * half), lambda i: (n_t - 1 - cur(i) % n_t, 0)),
            per_batch(2 * half), per_batch(8),
            _resident((FLIP, FLIP)),
            tok(cur),
            per_batch(N_MOD),
            _resident((D_MODEL, D_MODEL)),
            _resident((1, D_MODEL)),
            _resident((1, D_MODEL)),
        ],
        out_specs=tok(prev),
        scratch_shapes=[pltpu.VMEM((8, D_MODEL), F32), pltpu.VMEM((2, tk, D_MODEL), F32)],
        compiler_params=pltpu.CompilerParams(dimension_semantics=("arbitrary",), vmem_limit_bytes=VMEM_LIMIT),
        name="dft_seq",
    )(w_lo, w_up, ab, a_n, flip, x4, mod, wo, g.reshape(1, D_MODEL), beta.reshape(1, D_MODEL))


def _rope_tables(n_tokens):
    rows = n_tokens // GRID_W
    row = jnp.repeat(jnp.arange(rows), GRID_W).astype(F32)
    col = jnp.tile(jnp.arange(GRID_W), rows).astype(F32)
    inv = ROPE_BASE ** (-jnp.arange(0, AXIS_DIM, 2, dtype=F32) / AXIS_DIM)
    ang_r, ang_c = row[:, None] * inv, col[:, None] * inv
    cos_r, sin_r, cos_c, sin_c = jnp.cos(ang_r), jnp.sin(ang_r), jnp.cos(ang_c), jnp.sin(ang_c)
    zero = jnp.zeros_like(sin_r)
    cos_h = jnp.concatenate([cos_r, cos_r, cos_c, cos_c], axis=-1)
    sa_h = jnp.concatenate([-sin_r, zero, -sin_c, zero], axis=-1)
    sb_h = jnp.concatenate([zero, sin_r, zero, sin_c], axis=-1)
    rep = LANES // HEAD_DIM
    return tuple(jnp.tile(t, (1, rep)) for t in (cos_h, sa_h, sb_h))


def _dft_tables(n, rows):
    idx = jnp.arange(rows, dtype=jnp.int32)
    ang = ((idx[:, None] * idx[None, :]) % n).astype(F32) * (2.0 * np.pi / n)
    return jnp.cos(ang), jnp.sin(ang)


def kernel(x, c, ctx, c_ctx, mod_w, mod_b, ln_g, ln_b, ffn_wi, ffn_wo,
           attn_wqkv, attn_bqkv, attn_wo, attn_sink, fourier_wo):
    bsz, seq, _ = x.shape
    n_ctx = ctx.shape[1]
    assert DEPTH == 2 and seq % BLOCK == 0 and seq % GRID_W == 0

    cv = jnp.concatenate([c, c_ctx[None, :], jnp.zeros((MOD_ROWS - bsz - 1, D_MODEL), F32)], axis=0)
    mod = _mod_call(cv, mod_w, mod_b).reshape(DEPTH, MOD_ROWS, N_MOD, D_MODEL)
    wi = ffn_wi.astype(BF16)
    wo = ffn_wo.astype(BF16)

    mod_lat, mod_ctx = mod[0, :bsz], mod[0, bsz:bsz + 1]
    g, b = ln_g[0], ln_b[0]
    x = _ffn_call(x, mod_lat, 0, wi, wo, 0, 0, g[0], b[0], tm=FFN_TILE)
    ctx_s = _ffn_call(ctx.reshape(1, bsz * n_ctx, D_MODEL), mod_ctx, 0, wi, wo, 0, 0, g[0], b[0], tm=FFN_TILE)

    w_qkv = attn_wqkv[0].astype(BF16)
    cos_t, sa_t, sb_t = _rope_tables(seq)
    q, k, vt = _qkv_call(x, mod_lat, w_qkv, attn_bqkv[0], cos_t, sa_t, sb_t, tm=512)
    kc, vct = _kv_ctx_call(ctx_s.reshape(bsz, n_ctx, D_MODEL), mod_ctx, w_qkv[:, Q_WIDTH:], attn_bqkv[0, Q_WIDTH:])
    x = _attn_call(q, k, vt, kc, vct, attn_sink[0], x, mod_lat, attn_wo[0].astype(BF16), g[1], b[1], tq=4 * BLOCK)
    x = _ffn_call(x, mod_lat, 2, wi, wo, 0, 1, g[2], b[2], tm=FFN_TILE)

    mod_lat = mod[1, :bsz]
    g, b = ln_g[1], ln_b[1]
    x = _ffn_call(x, mod_lat, 0, wi, wo, 1, 0, g[0], b[0], tm=FFN_TILE)
    half = seq // 2
    cos_c, sin_c = _dft_tables(FOURIER_GROUP_CH, FOURIER_GROUP_CH)
    cos_s, sin_s = _dft_tables(seq, half)
    w_lo = jnp.concatenate([cos_s, -sin_s], axis=1).astype(BF16)
    w_up = jnp.concatenate([cos_s, sin_s], axis=1).astype(BF16)
    anti = jnp.arange(FLIP)[:, None] + jnp.arange(FLIP)[None, :] == FLIP - 1
    flip = anti.astype(BF16)
    x4 = x.reshape(bsz, 2, half, D_MODEL)
    ab, a_n = _dft_fold_call(x4, mod_lat, cos_c.astype(BF16), sin_c.astype(BF16), flip, tm=512)
    x4 = _dft_seq_call(w_lo, w_up, ab.reshape(bsz, seq, D_MODEL), a_n, flip, x4, mod_lat,
                       fourier_wo[0].astype(BF16), g[1], b[1], tk=512)
    x = x4.reshape(bsz, seq, D_MODEL)
    x = _ffn_call(x, mod_lat, 2, wi, wo, 1, 1, g[2], b[2], tm=FFN_TILE)
    return x
```

```python
import functools

import jax
import jax.numpy as jnp
import numpy as np
from jax import lax
from jax.experimental import pallas as pl
from jax.experimental.pallas import tpu as pltpu

D_MODEL = 1024
DEPTH = 2
GRID_W = 64
N_HEADS = 16
N_KV_HEADS = 4
HEAD_DIM = 64
GROUP = N_HEADS // N_KV_HEADS
Q_WIDTH = N_HEADS * HEAD_DIM
KV_WIDTH = N_KV_HEADS * HEAD_DIM
QKV_WIDTH = Q_WIDTH + 2 * KV_WIDTH
KV_DUP_WIDTH = 2 * KV_WIDTH
WINDOW = 128
BLOCK = 128
SPAN = BLOCK + 2 * WINDOW
ROPE_BASE = 10000.0
AXIS_DIM = HEAD_DIM // 2
FOURIER_GROUPS = 4
FOURIER_GROUP_CH = D_MODEL // FOURIER_GROUPS
D_FF = 2816
N_MOD = 9
LN_EPS = 1e-5
ALPHA = (2.0 * DEPTH) ** 0.25
NEG_INF = -1e30
LOG2E = float(np.log2(np.e))

LANES = 128
MOD_ROWS = 24
VMEM_LIMIT = 60 * 1024 * 1024
FFN_TILE = 1024
FFN_CHUNKS = (1024, 1024, 768)
FLIP = 256
WEIGHT_SLABS = 11

BF16 = jnp.bfloat16
F32 = jnp.float32


def _params(n_axes):
    return pltpu.CompilerParams(dimension_semantics=("parallel",) * n_axes,
                                vmem_limit_bytes=VMEM_LIMIT)


def _resident(shape):
    return pl.BlockSpec(shape, lambda *_: (0,) * len(shape), pipeline_mode=pl.Buffered(1))


def _layer_norm(z, g, b):
    mu = jnp.mean(z, axis=-1, keepdims=True)
    d = z - mu
    var = jnp.mean(d * d, axis=-1, keepdims=True)
    return d * lax.rsqrt(var + LN_EPS) * g + b


def _modulate(x, mod_ref, j):
    shift = mod_ref[3 * j:3 * j + 1, :]
    scale = mod_ref[3 * j + 1:3 * j + 2, :]
    return x * (1.0 + scale) + shift


def _post_norm(x, y, mod_ref, j, g_ref, b_ref):
    gate = mod_ref[3 * j + 2:3 * j + 3, :]
    return _layer_norm(ALPHA * x + gate * y, g_ref[...], b_ref[...])


def _mod_kernel(cv_ref, w_ref, b_ref, o_ref):
    cv = cv_ref[...]
    s = (cv * jax.nn.sigmoid(cv)).astype(BF16)
    o_ref[...] = jnp.dot(s, w_ref[...].astype(BF16), preferred_element_type=F32) + b_ref[...]


def _mod_call(cv, mod_w, mod_b):
    tn = D_MODEL
    n_out = N_MOD * D_MODEL
    return pl.pallas_call(
        _mod_kernel,
        out_shape=jax.ShapeDtypeStruct((DEPTH, MOD_ROWS, n_out), F32),
        grid=(DEPTH, n_out // tn),
        in_specs=[
            pl.BlockSpec((MOD_ROWS, D_MODEL), lambda i, n: (0, 0)),
            pl.BlockSpec((None, D_MODEL, tn), lambda i, n: (i, 0, n)),
            pl.BlockSpec((None, 1, tn), lambda i, n: (i, 0, n)),
        ],
        out_specs=pl.BlockSpec((None, MOD_ROWS, tn), lambda i, n: (i, 0, n)),
        compiler_params=_params(2),
        name="mod",
    )(cv, mod_w, mod_b.reshape(DEPTH, 1, n_out))


def _ffn_kernel(j, n_tiles, x_ref, mod_ref, wi_ref, wo_ref, g_ref, b_ref, o_ref, u_ref, z_ref):
    i = pl.program_id(0)

    @pl.when(i == 0)
    def _():
        z_ref[...] = jnp.zeros_like(z_ref)

    def norm_previous():
        o_ref[...] = _layer_norm(z_ref[...], g_ref[...], b_ref[...])

    @pl.when(i < n_tiles)
    def _():
        norm_previous()
        h = _modulate(x_ref[...], mod_ref, j).astype(BF16)
        start = 0
        for width in FFN_CHUNKS:
            a = jnp.dot(h, wi_ref[:, start:start + width], preferred_element_type=F32)
            g = jnp.dot(h, wi_ref[:, D_FF + start:D_FF + start + width], preferred_element_type=F32)
            u_ref[:, start:start + width] = (a * (g * jax.nn.sigmoid(g))).astype(BF16)
            start += width
        y = 0.5 * jnp.dot(u_ref[...], wo_ref[...], preferred_element_type=F32)
        z_ref[...] = ALPHA * x_ref[...] + mod_ref[3 * j + 2:3 * j + 3, :] * y

    @pl.when(i == n_tiles)
    def _():
        norm_previous()


def _ffn_call(x, mod, j, wi_sets, wo_sets, which, g, b, tm):
    bsz, length, _ = x.shape
    per_batch = length // tm
    n_tiles = bsz * per_batch
    cur = lambda i: jnp.minimum(i, n_tiles - 1)
    prev = lambda i: jnp.maximum(i - 1, 0)
    tile = lambda pick: pl.BlockSpec((None, tm, D_MODEL), lambda i: (pick(i) // per_batch, pick(i) % per_batch, 0))
    weight = lambda rows, cols: pl.BlockSpec((None, rows, cols), lambda i: (which, 0, 0),
                                             pipeline_mode=pl.Buffered(1))
    return pl.pallas_call(
        functools.partial(_ffn_kernel, j, n_tiles),
        out_shape=jax.ShapeDtypeStruct(x.shape, F32),
        grid=(n_tiles + 1,),
        in_specs=[
            tile(cur),
            pl.BlockSpec((None, N_MOD, D_MODEL), lambda i: (cur(i) // per_batch, 0, 0)),
            weight(D_MODEL, 2 * D_FF),
            weight(D_FF, D_MODEL),
            _resident((1, D_MODEL)),
            _resident((1, D_MODEL)),
        ],
        out_specs=tile(prev),
        scratch_shapes=[pltpu.VMEM((tm, D_FF), BF16), pltpu.VMEM((tm, D_MODEL), F32)],
        compiler_params=pltpu.CompilerParams(dimension_semantics=("arbitrary",), vmem_limit_bytes=VMEM_LIMIT),
        name="ffn",
    )(x, mod, wi_sets, wo_sets, g.reshape(1, D_MODEL), b.reshape(1, D_MODEL))


def _rope(t, cos_ref, sa_ref, sb_ref):
    half = AXIS_DIM // 2
    outs = []
    for c in range(t.shape[1] // LANES):
        tc = t[:, c * LANES:(c + 1) * LANES]
        up = pltpu.roll(tc, LANES - half, 1)
        down = pltpu.roll(tc, half, 1)
        outs.append(tc * cos_ref[...] + up * sa_ref[...] + down * sb_ref[...])
    return jnp.concatenate(outs, axis=1)


def _dup_heads(t):
    pieces = []
    for h in range(t.shape[1] // HEAD_DIM):
        head = t[:, h * HEAD_DIM:(h + 1) * HEAD_DIM]
        pieces += [head, head]
    return jnp.concatenate(pieces, axis=1)


def _qkv_kernel(x_ref, mod_ref, w_ref, b_ref, cos_ref, sa_ref, sb_ref, q_ref, k_ref, vt_ref):
    h = _modulate(x_ref[...], mod_ref, 1).astype(BF16)
    qkv = jnp.dot(h, w_ref[...], preferred_element_type=F32) + b_ref[...]
    q = qkv[:, :Q_WIDTH] * (HEAD_DIM ** -0.5 * LOG2E)
    q_ref[...] = _rope(q, cos_ref, sa_ref, sb_ref).astype(BF16)
    k = _rope(qkv[:, Q_WIDTH:Q_WIDTH + KV_WIDTH], cos_ref, sa_ref, sb_ref)
    k_ref[...] = _dup_heads(k).astype(BF16)
    v = _dup_heads(qkv[:, Q_WIDTH + KV_WIDTH:])
    for blk in range(vt_ref.shape[0]):
        vt_ref[blk] = v[blk * BLOCK:(blk + 1) * BLOCK, :].T.astype(BF16)


def _qkv_call(x, mod, w, b, cos_t, sa_t, sb_t, tm):
    bsz, length, _ = x.shape
    tok = lambda width: pl.BlockSpec((None, tm, width), lambda bi, t: (bi, t, 0))
    table = pl.BlockSpec((tm, LANES), lambda bi, t: (t, 0))
    return pl.pallas_call(
        _qkv_kernel,
        out_shape=(jax.ShapeDtypeStruct((bsz, length, Q_WIDTH), BF16),
                   jax.ShapeDtypeStruct((bsz, length, KV_DUP_WIDTH), BF16),
                   jax.ShapeDtypeStruct((bsz, length // BLOCK, KV_DUP_WIDTH, BLOCK), BF16)),
        grid=(bsz, length // tm),
        in_specs=[
            tok(D_MODEL),
            pl.BlockSpec((None, N_MOD, D_MODEL), lambda bi, t: (bi, 0, 0)),
            _resident((D_MODEL, QKV_WIDTH)),
            _resident((1, QKV_WIDTH)),
            table, table, table,
        ],
        out_specs=(tok(Q_WIDTH), tok(KV_DUP_WIDTH),
                   pl.BlockSpec((None, tm // BLOCK, KV_DUP_WIDTH, BLOCK), lambda bi, t: (bi, t, 0, 0))),
        compiler_params=_params(2),
        name="qkv",
    )(x, mod, w, b.reshape(1, QKV_WIDTH), cos_t, sa_t, sb_t)


def _kv_ctx_kernel(x_ref, mod_ref, w_ref, b_ref, k_ref, vt_ref):
    h = _modulate(x_ref[...], mod_ref, 1).astype(BF16)
    kv = jnp.dot(h, w_ref[...], preferred_element_type=F32) + b_ref[...]
    k_ref[...] = _dup_heads(kv[:, :KV_WIDTH]).astype(BF16)
    vt_ref[...] = _dup_heads(kv[:, KV_WIDTH:]).T.astype(BF16)


def _kv_ctx_call(x, mod, w, b):
    bsz, n_ctx, _ = x.shape
    return pl.pallas_call(
        _kv_ctx_kernel,
        out_shape=(jax.ShapeDtypeStruct((bsz, n_ctx, KV_DUP_WIDTH), BF16),
                   jax.ShapeDtypeStruct((bsz, KV_DUP_WIDTH, n_ctx), BF16)),
        grid=(bsz,),
        in_specs=[
            pl.BlockSpec((None, n_ctx, D_MODEL), lambda bi: (bi, 0, 0)),
            _resident((None, N_MOD, D_MODEL)),
            _resident((D_MODEL, 2 * KV_WIDTH)),
            _resident((1, 2 * KV_WIDTH)),
        ],
        out_specs=(pl.BlockSpec((None, n_ctx, KV_DUP_WIDTH), lambda bi: (bi, 0, 0)),
                   pl.BlockSpec((None, KV_DUP_WIDTH, n_ctx), lambda bi: (bi, 0, 0))),
        compiler_params=_params(1),
        name="kv_ctx",
    )(x, mod, w, b.reshape(1, 2 * KV_WIDTH))


def _attn_kernel(n_blocks, n_steps, sink_ref, q_ref, k_ref, vt_ref, kc_ref, vct_ref, xp_ref, modp_ref, wo_ref,
                 g_ref, b_ref, wi32_ref, wo32_ref, out_ref, wi16_ref, wo16_ref, o_ref):
    i = pl.program_id(0)

    @pl.when(i == 0)
    def _():
        o_ref[...] = jnp.zeros_like(o_ref)

    def project(o_prev):
        return jnp.dot(o_prev, wo_ref[...], preferred_element_type=F32)

    def norm(y):
        out_ref[...] = _post_norm(xp_ref[...], y, modp_ref, 1, g_ref, b_ref)

    def round_weights():
        wi16_ref[...] = wi32_ref[...].astype(BF16)
        wo16_ref[...] = wo32_ref[...].astype(BF16)

    @pl.when(i < n_steps)
    def _():
        round_weights()
        o_prev = o_ref[...]
        y = []
        tile_in_batch = i % (n_blocks * BLOCK // q_ref.shape[0])
        _attend(n_blocks, tile_in_batch, sink_ref, q_ref, k_ref, vt_ref, kc_ref, vct_ref, o_ref,
                after_slot={9: lambda: y.append(project(o_prev)), 21: lambda: norm(y[0])})

    @pl.when(i == n_steps)
    def _():
        round_weights()
        norm(project(o_ref[...]))


def _attend(n_blocks, tile_in_batch, sink_ref, q_ref, k_ref, vt_ref, kc_ref, vct_ref, o_ref, after_slot):
    n_sub = q_ref.shape[0] // BLOCK
    qb0 = tile_in_batch * n_sub
    cols2 = 2 * BLOCK
    key_i = lax.broadcasted_iota(jnp.int32, (BLOCK, cols2), 0)
    qry_i = lax.broadcasted_iota(jnp.int32, (BLOCK, cols2), 1) & (BLOCK - 1)
    first_head = lax.broadcasted_iota(jnp.int32, (1, cols2), 1) < BLOCK
    lo_q = lax.broadcasted_iota(jnp.int32, (cols2, LANES), 1) < HEAD_DIM
    lo_v = lax.broadcasted_iota(jnp.int32, (LANES, 1), 0) < HEAD_DIM

    def block_ids(sub):
        return [jnp.clip(qb0 + sub + d, 0, n_blocks - 1) for d in (-1, 0, 1)]

    slots_per_block = 2 * N_KV_HEADS
    p_t, inv = {}, {}

    def scores(slot):
        sub, rest = divmod(slot, slots_per_block)
        h, half = divmod(rest, 2)
        c0 = 2 * h * LANES
        q_rows = slice(sub * BLOCK, (sub + 1) * BLOCK)
        q_cat = jnp.concatenate([q_ref[q_rows, c0:c0 + LANES], q_ref[q_rows, c0 + LANES:c0 + 2 * LANES]], axis=0)
        keep = lo_q if half == 0 else jnp.logical_not(lo_q)
        q_sel = jnp.where(keep, q_cat, jnp.zeros_like(q_cat))
        lanes = slice(h * LANES, (h + 1) * LANES)
        k_h = jnp.concatenate([k_ref[pl.ds(pl.multiple_of(blk * BLOCK, BLOCK), BLOCK), lanes]
                               for blk in block_ids(sub)] + [kc_ref[:, lanes]], axis=0)
        return lax.dot_general(k_h, q_sel, (((1,), (1,)), ((), ())), preferred_element_type=F32)

    def softmax(slot, s_t):
        sub, rest = divmod(slot, slots_per_block)
        h, half = divmod(rest, 2)
        left_ok = (qry_i <= key_i) & (qb0 + sub > 0)
        right_ok = (key_i <= qry_i) & (qb0 + sub < n_blocks - 1)
        sink = jnp.where(first_head, sink_ref[4 * h + half], sink_ref[4 * h + 2 + half]) * LOG2E
        parts = [jnp.where(left_ok, s_t[:BLOCK], NEG_INF),
                 s_t[BLOCK:2 * BLOCK],
                 jnp.where(right_ok, s_t[2 * BLOCK:SPAN], NEG_INF)]
        parts += [s_t[c:c + BLOCK] for c in range(SPAN, s_t.shape[0], BLOCK)]
        m = jnp.max(functools.reduce(jnp.maximum, parts), axis=0, keepdims=True)
        m = jnp.maximum(m, sink)
        p = [jnp.exp2(t - m) for t in parts]
        denom = jnp.sum(functools.reduce(jnp.add, p), axis=0, keepdims=True) + jnp.exp2(sink - m)
        inv[slot] = 1.0 / denom
        p_t[slot] = jnp.concatenate(p, axis=0).astype(BF16)

    def weighted_values(pair):
        sub, h = divmod(pair, N_KV_HEADS)
        rows = slice(h * LANES, (h + 1) * LANES)
        v_t = jnp.concatenate([vt_ref[blk, rows, :] for blk in block_ids(sub)] + [vct_ref[rows, :]], axis=1)
        v_lo = jnp.where(lo_v, v_t, jnp.zeros_like(v_t))
        v_hi = jnp.where(lo_v, jnp.zeros_like(v_t), v_t)
        acc_t = (jnp.dot(v_lo, p_t.pop(2 * pair), preferred_element_type=F32)
                 + jnp.dot(v_hi, p_t.pop(2 * pair + 1), preferred_element_type=F32))
        out = (acc_t * jnp.where(lo_v, inv.pop(2 * pair), inv.pop(2 * pair + 1))).T.astype(BF16)
        c0 = 2 * h * LANES
        q_rows = slice(sub * BLOCK, (sub + 1) * BLOCK)
        o_ref[q_rows, c0:c0 + LANES] = out[:BLOCK]
        o_ref[q_rows, c0 + LANES:c0 + 2 * LANES] = out[BLOCK:]

    n_slots = n_sub * slots_per_block
    ahead = 1
    s_t = {slot: scores(slot) for slot in range(ahead)}
    for slot in range(n_slots):
        if slot + ahead < n_slots:
            s_t[slot + ahead] = scores(slot + ahead)
        softmax(slot, s_t.pop(slot))
        if slot % 2 == 0 and slot > 0:
            weighted_values(slot // 2 - 1)
        if slot in after_slot:
            after_slot[slot]()
    weighted_values(n_slots // 2 - 1)


def _attn_call(q, k, vt, kc, vct, sink, x, mod, wo, g, b, wi32, wo32, tq):
    bsz, length, _ = q.shape
    n_blocks = length // BLOCK
    per_batch_tiles = length // tq
    n_steps = bsz * per_batch_tiles
    n_later = wi32.shape[0] - 1
    wi_cols, wo_rows = 2 * D_FF // WEIGHT_SLABS, D_FF // WEIGHT_SLABS
    assert n_later * WEIGHT_SLABS <= n_steps + 1
    slab = lambda i: jnp.minimum(i, n_later * WEIGHT_SLABS - 1)
    cur = lambda i: jnp.minimum(i, n_steps - 1)
    prev = lambda i: jnp.maximum(i - 1, 0)
    tile = lambda pick: pl.BlockSpec(
        (None, tq, D_MODEL), lambda i, *_: (pick(i) // per_batch_tiles, pick(i) % per_batch_tiles, 0))
    per_batch = lambda arr: pl.BlockSpec(
        (None,) + arr.shape[1:], lambda i, *_: (cur(i) // per_batch_tiles,) + (0,) * (arr.ndim - 1))
    return pl.pallas_call(
        functools.partial(_attn_kernel, n_blocks, n_steps),
        out_shape=(jax.ShapeDtypeStruct(x.shape, F32),
                   jax.ShapeDtypeStruct((n_later,) + wi32.shape[1:], BF16),
                   jax.ShapeDtypeStruct((n_later,) + wo32.shape[1:], BF16)),
        grid_spec=pltpu.PrefetchScalarGridSpec(
            num_scalar_prefetch=1,
            grid=(n_steps + 1,),
            in_specs=[
                tile(cur),
                per_batch(k), per_batch(vt), per_batch(kc), per_batch(vct),
                tile(prev),
                pl.BlockSpec((None, N_MOD, D_MODEL), lambda i, *_: (prev(i) // per_batch_tiles, 0, 0)),
                _resident((D_MODEL, D_MODEL)),
                _resident((1, D_MODEL)),
                _resident((1, D_MODEL)),
                pl.BlockSpec((None, D_MODEL, wi_cols),
                             lambda i, *_: (1 + slab(i) // WEIGHT_SLABS, 0, slab(i) % WEIGHT_SLABS)),
                pl.BlockSpec((None, wo_rows, D_MODEL),
                             lambda i, *_: (1 + slab(i) // WEIGHT_SLABS, slab(i) % WEIGHT_SLABS, 0)),
            ],
            out_specs=(tile(prev),
                       pl.BlockSpec((None, D_MODEL, wi_cols),
                                    lambda i, *_: (slab(i) // WEIGHT_SLABS, 0, slab(i) % WEIGHT_SLABS)),
                       pl.BlockSpec((None, wo_rows, D_MODEL),
                                    lambda i, *_: (slab(i) // WEIGHT_SLABS, slab(i) % WEIGHT_SLABS, 0))),
            scratch_shapes=[pltpu.VMEM((tq, Q_WIDTH), BF16)],
        ),
        compiler_params=pltpu.CompilerParams(dimension_semantics=("arbitrary",), vmem_limit_bytes=VMEM_LIMIT),
        name="attn",
    )(sink, q, k, vt, kc, vct, x, mod, wo, g.reshape(1, D_MODEL), b.reshape(1, D_MODEL), wi32, wo32)


def _reverse_shift(t, first_row, flip_ref):
    rows = t.shape[0]
    blocks = [jnp.dot(flip_ref[...], t[r0:r0 + FLIP, :], preferred_element_type=F32)
              for r0 in range(rows - FLIP, -FLIP, -FLIP)]
    shifted = pltpu.roll(jnp.concatenate(blocks, axis=0), 1, 0)
    is_first = lax.broadcasted_iota(jnp.int32, shifted.shape, 0) == 0
    return jnp.where(is_first, first_row, shifted)


def _channel_dft(t, w_ref):
    return jnp.concatenate(
        [jnp.dot(t[:, g0:g0 + FOURIER_GROUP_CH], w_ref[...], preferred_element_type=F32)
         for g0 in range(0, D_MODEL, FOURIER_GROUP_CH)], axis=1)


def _dft_fold_kernel(lo_ref, up_ref, mod_ref, cc_ref, sc_ref, flip_ref, ab_ref, an_ref, carry_ref):
    @pl.when(pl.program_id(1) == 0)
    def _():
        carry_ref[...] = jnp.zeros_like(carry_ref)

    h_lo = _modulate(lo_ref[...], mod_ref, 1)
    h_up = _modulate(up_ref[...], mod_ref, 1).astype(BF16)
    partner = _reverse_shift(h_up, carry_ref[0:1, :], flip_ref)
    carry_ref[...] = h_up[0:8, :].astype(F32)
    ab_ref[0] = _channel_dft((h_lo + partner).astype(BF16), cc_ref).astype(BF16)
    ab_ref[1] = _channel_dft((h_lo - partner).astype(BF16), sc_ref).astype(BF16)
    an_ref[...] = _channel_dft(h_up[0:8, :], cc_ref)


def _dft_fold_call(x4, mod, cc, sc, flip, tm):
    bsz, _, half, _ = x4.shape
    n_t = half // tm
    ch = FOURIER_GROUP_CH
    return pl.pallas_call(
        _dft_fold_kernel,
        out_shape=(jax.ShapeDtypeStruct((bsz, 2, half, D_MODEL), BF16),
                   jax.ShapeDtypeStruct((bsz, 8, D_MODEL), F32)),
        grid=(bsz, n_t),
        in_specs=[
            pl.BlockSpec((None, None, tm, D_MODEL), lambda bi, t: (bi, 0, t, 0)),
            pl.BlockSpec((None, None, tm, D_MODEL), lambda bi, t: (bi, 1, n_t - 1 - t, 0)),
            pl.BlockSpec((None, N_MOD, D_MODEL), lambda bi, t: (bi, 0, 0)),
            _resident((ch, ch)),
            _resident((ch, ch)),
            _resident((FLIP, FLIP)),
        ],
        out_specs=(pl.BlockSpec((None, 2, tm, D_MODEL), lambda bi, t: (bi, 0, t, 0)),
                   pl.BlockSpec((None, 8, D_MODEL), lambda bi, t: (bi, 0, 0))),
        scratch_shapes=[pltpu.VMEM((8, D_MODEL), F32)],
        compiler_params=pltpu.CompilerParams(dimension_semantics=("parallel", "arbitrary"),
                                             vmem_limit_bytes=VMEM_LIMIT),
        name="dft_fold",
    )(x4, x4, mod, cc, sc, flip)


def _dft_seq_kernel(norm, n_t, wlo_ref, wup_ref, ab_ref, an_ref, alt_ref, flip_ref, x_ref, mod_ref, wo_ref,
                    g_ref, beta_ref, out_ref, carry_ref):
    t = pl.program_id(1)
    tk = wlo_ref.shape[0]
    ab = ab_ref[...]
    a_n = an_ref[0:1, :]
    row = lax.broadcasted_iota(jnp.int32, (tk, 1), 0)

    def alternating(k0):
        return (1 - 2 * ((k0 + row) & 1)).astype(F32)

    f_lo = (jnp.dot(wlo_ref[...], ab, preferred_element_type=F32) + alternating(t * tk) * a_n) * norm
    f_mir = (jnp.dot(wup_ref[...], ab, preferred_element_type=F32) + alternating((n_t - 1 - t) * tk) * a_n) * norm
    f_mir = f_mir.astype(BF16)

    @pl.when(t == 0)
    def _():
        f_n = (jnp.dot(alt_ref[...], ab, preferred_element_type=F32)[0:1, :] + a_n) * norm
        carry_ref[0:1, :] = f_n.astype(BF16).astype(F32)

    f_up = _reverse_shift(f_mir, carry_ref[0:1, :], flip_ref)
    carry_ref[...] = f_mir[0:8, :].astype(F32)
    y_lo = jnp.dot(f_lo.astype(BF16), wo_ref[...], preferred_element_type=F32)
    y_up = jnp.dot(f_up.astype(BF16), wo_ref[...], preferred_element_type=F32)
    out_ref[0] = _post_norm(x_ref[0], y_lo, mod_ref, 1, g_ref, beta_ref)
    out_ref[1] = _post_norm(x_ref[1], y_up, mod_ref, 1, g_ref, beta_ref)


def _dft_seq_call(w_lo, w_up, ab, a_n, alt, flip, x4, mod, wo, g, beta, tk):
    bsz, _, half, _ = x4.shape
    n_t = half // tk
    norm = float(1.0 / np.sqrt(2 * half * FOURIER_GROUP_CH))
    tok = pl.BlockSpec((None, 2, tk, D_MODEL), lambda bi, t: (bi, 0, t, 0))
    return pl.pallas_call(
        functools.partial(_dft_seq_kernel, norm, n_t),
        out_shape=jax.ShapeDtypeStruct(x4.shape, F32),
        grid=(bsz, n_t),
        in_specs=[
            pl.BlockSpec((tk, 2 * half), lambda bi, t: (t, 0)),
            pl.BlockSpec((tk, 2 * half), lambda bi, t: (n_t - 1 - t, 0)),
            pl.BlockSpec((None, 2 * half, D_MODEL), lambda bi, t: (bi, 0, 0)),
            pl.BlockSpec((None, 8, D_MODEL), lambda bi, t: (bi, 0, 0)),
            _resident((8, 2 * half)),
            _resident((FLIP, FLIP)),
            tok,
            pl.BlockSpec((None, N_MOD, D_MODEL), lambda bi, t: (bi, 0, 0)),
            _resident((D_MODEL, D_MODEL)),
            _resident((1, D_MODEL)),
            _resident((1, D_MODEL)),
        ],
        out_specs=tok,
        scratch_shapes=[pltpu.VMEM((8, D_MODEL), F32)],
        compiler_params=pltpu.CompilerParams(dimension_semantics=("parallel", "arbitrary"),
                                             vmem_limit_bytes=VMEM_LIMIT),
        name="dft_seq",
    )(w_lo, w_up, ab, a_n, alt, flip, x4, mod, wo, g.reshape(1, D_MODEL), beta.reshape(1, D_MODEL))


def _rope_tables(n_tokens):
    rows = n_tokens // GRID_W
    row = jnp.repeat(jnp.arange(rows), GRID_W).astype(F32)
    col = jnp.tile(jnp.arange(GRID_W), rows).astype(F32)
    inv = ROPE_BASE ** (-jnp.arange(0, AXIS_DIM, 2, dtype=F32) / AXIS_DIM)
    ang_r, ang_c = row[:, None] * inv, col[:, None] * inv
    cos_r, sin_r, cos_c, sin_c = jnp.cos(ang_r), jnp.sin(ang_r), jnp.cos(ang_c), jnp.sin(ang_c)
    zero = jnp.zeros_like(sin_r)
    cos_h = jnp.concatenate([cos_r, cos_r, cos_c, cos_c], axis=-1)
    sa_h = jnp.concatenate([-sin_r, zero, -sin_c, zero], axis=-1)
    sb_h = jnp.concatenate([zero, sin_r, zero, sin_c], axis=-1)
    rep = LANES // HEAD_DIM
    return tuple(jnp.tile(t, (1, rep)) for t in (cos_h, sa_h, sb_h))


def _dft_tables(n, rows):
    idx = jnp.arange(rows, dtype=jnp.int32)
    ang = ((idx[:, None] * idx[None, :]) % n).astype(F32) * (2.0 * np.pi / n)
    return jnp.cos(ang), jnp.sin(ang)


def kernel(x, c, ctx, c_ctx, mod_w, mod_b, ln_g, ln_b, ffn_wi, ffn_wo,
           attn_wqkv, attn_bqkv, attn_wo, attn_sink, fourier_wo):
    bsz, seq, _ = x.shape
    n_ctx = ctx.shape[1]
    assert DEPTH == 2 and seq % BLOCK == 0 and seq % GRID_W == 0

    cv = jnp.concatenate([c, c_ctx[None, :], jnp.zeros((MOD_ROWS - bsz - 1, D_MODEL), F32)], axis=0)
    mod = _mod_call(cv, mod_w, mod_b).reshape(DEPTH, MOD_ROWS, N_MOD, D_MODEL)
    wi32 = ffn_wi.reshape(2 * DEPTH, D_MODEL, 2 * D_FF)
    wo32 = ffn_wo.reshape(2 * DEPTH, D_FF, D_MODEL)
    wi0, wo0 = wi32[:1].astype(BF16), wo32[:1].astype(BF16)

    mod_lat, mod_ctx = mod[0, :bsz], mod[0, bsz:bsz + 1]
    g, b = ln_g[0], ln_b[0]
    x = _ffn_call(x, mod_lat, 0, wi0, wo0, 0, g[0], b[0], tm=FFN_TILE)
    ctx_s = _ffn_call(ctx.reshape(1, bsz * n_ctx, D_MODEL), mod_ctx, 0, wi0, wo0, 0, g[0], b[0], tm=FFN_TILE)

    w_qkv = attn_wqkv[0].astype(BF16)
    cos_t, sa_t, sb_t = _rope_tables(seq)
    q, k, vt = _qkv_call(x, mod_lat, w_qkv, attn_bqkv[0], cos_t, sa_t, sb_t, tm=512)
    kc, vct = _kv_ctx_call(ctx_s.reshape(bsz, n_ctx, D_MODEL), mod_ctx, w_qkv[:, Q_WIDTH:], attn_bqkv[0, Q_WIDTH:])
    x, wi, wo = _attn_call(q, k, vt, kc, vct, attn_sink[0], x, mod_lat, attn_wo[0].astype(BF16), g[1], b[1],
                           wi32, wo32, tq=4 * BLOCK)
    x = _ffn_call(x, mod_lat, 2, wi, wo, 0, g[2], b[2], tm=FFN_TILE)

    mod_lat = mod[1, :bsz]
    g, b = ln_g[1], ln_b[1]
    x = _ffn_call(x, mod_lat, 0, wi, wo, 1, g[0], b[0], tm=FFN_TILE)
    half = seq // 2
    cos_c, sin_c = _dft_tables(FOURIER_GROUP_CH, FOURIER_GROUP_CH)
    cos_s, sin_s = _dft_tables(seq, half)
    w_lo = jnp.concatenate([cos_s, -sin_s], axis=1).astype(BF16)
    w_up = jnp.concatenate([cos_s, sin_s], axis=1).astype(BF16)
    alt = jnp.zeros((8, seq), F32).at[0, :half].set(1.0 - 2.0 * (jnp.arange(half) % 2)).astype(BF16)
    anti = jnp.arange(FLIP)[:, None] + jnp.arange(FLIP)[None, :] == FLIP - 1
    flip = anti.astype(BF16)
    x4 = x.reshape(bsz, 2, half, D_MODEL)
    ab, a_n = _dft_fold_call(x4, mod_lat, cos_c.astype(BF16), sin_c.astype(BF16), flip, tm=512)
    x4 = _dft_seq_call(w_lo, w_up, ab.reshape(bsz, seq, D_MODEL), a_n, alt, flip, x4, mod_lat,
                       fourier_wo[0].astype(BF16), g[1], b[1], tk=512)
    x = x4.reshape(bsz, seq, D_MODEL)
    x = _ffn_call(x, mod_lat, 2, wi, wo, 2, g[2], b[2], tm=FFN_TILE)
    return x
```

```python
import functools

import jax
import jax.numpy as jnp
import numpy as np
from jax import lax
from jax.experimental import pallas as pl
from jax.experimental.pallas import tpu as pltpu

D_MODEL = 1024
DEPTH = 2
GRID_W = 64
N_HEADS = 16
N_KV_HEADS = 4
HEAD_DIM = 64
GROUP = N_HEADS // N_KV_HEADS
Q_WIDTH = N_HEADS * HEAD_DIM
KV_WIDTH = N_KV_HEADS * HEAD_DIM
QKV_WIDTH = Q_WIDTH + 2 * KV_WIDTH
KV_DUP_WIDTH = 2 * KV_WIDTH
WINDOW = 128
BLOCK = 128
SPAN = BLOCK + 2 * WINDOW
ROPE_BASE = 10000.0
AXIS_DIM = HEAD_DIM // 2
FOURIER_GROUPS = 4
FOURIER_GROUP_CH = D_MODEL // FOURIER_GROUPS
D_FF = 2816
N_MOD = 9
LN_EPS = 1e-5
ALPHA = (2.0 * DEPTH) ** 0.25
NEG_INF = -1e30
LOG2E = float(np.log2(np.e))

LANES = 128
MOD_ROWS = 24
VMEM_LIMIT = 60 * 1024 * 1024
FFN_TILE = 1024
FFN_CHUNKS = (1024, 1024, 768)
FLIP = 256
TWIDDLE_STEP = 32
WEIGHT_SLABS = 11

BF16 = jnp.bfloat16
F32 = jnp.float32


def _params(n_axes):
    return pltpu.CompilerParams(dimension_semantics=("parallel",) * n_axes,
                                vmem_limit_bytes=VMEM_LIMIT)


def _resident(shape):
    return pl.BlockSpec(shape, lambda *_: (0,) * len(shape), pipeline_mode=pl.Buffered(1))


def _layer_norm(z, g, b):
    mu = jnp.mean(z, axis=-1, keepdims=True)
    d = z - mu
    var = jnp.mean(d * d, axis=-1, keepdims=True)
    return d * lax.rsqrt(var + LN_EPS) * g + b


def _modulate(x, mod_ref, j):
    shift = mod_ref[3 * j:3 * j + 1, :]
    scale = mod_ref[3 * j + 1:3 * j + 2, :]
    return x * (1.0 + scale) + shift


def _post_norm(x, y, mod_ref, j, g_ref, b_ref):
    gate = mod_ref[3 * j + 2:3 * j + 3, :]
    return _layer_norm(ALPHA * x + gate * y, g_ref[...], b_ref[...])


def _mod_kernel(cv_ref, w_ref, b_ref, o_ref):
    cv = cv_ref[...]
    s = (cv * jax.nn.sigmoid(cv)).astype(BF16)
    o_ref[...] = jnp.dot(s, w_ref[...].astype(BF16), preferred_element_type=F32) + b_ref[...]


def _mod_call(cv, mod_w, mod_b):
    tn = D_MODEL
    n_out = N_MOD * D_MODEL
    return pl.pallas_call(
        _mod_kernel,
        out_shape=jax.ShapeDtypeStruct((DEPTH, MOD_ROWS, n_out), F32),
        grid=(DEPTH, n_out // tn),
        in_specs=[
            pl.BlockSpec((MOD_ROWS, D_MODEL), lambda i, n: (0, 0)),
            pl.BlockSpec((None, D_MODEL, tn), lambda i, n: (i, 0, n)),
            pl.BlockSpec((None, 1, tn), lambda i, n: (i, 0, n)),
        ],
        out_specs=pl.BlockSpec((None, MOD_ROWS, tn), lambda i, n: (i, 0, n)),
        compiler_params=_params(2),
        name="mod",
    )(cv, mod_w, mod_b.reshape(DEPTH, 1, n_out))


def _ffn_kernel(j, n_tiles, x_ref, mod_ref, wi_ref, wo_ref, g_ref, b_ref, o_ref, u_ref, z_ref):
    i = pl.program_id(0)

    @pl.when(i == 0)
    def _():
        z_ref[...] = jnp.zeros_like(z_ref)

    def norm_previous():
        o_ref[...] = _layer_norm(z_ref[...], g_ref[...], b_ref[...])

    @pl.when(i < n_tiles)
    def _():
        norm_previous()
        h = _modulate(x_ref[...], mod_ref, j).astype(BF16)
        start = 0
        for width in FFN_CHUNKS:
            a = jnp.dot(h, wi_ref[:, start:start + width], preferred_element_type=F32)
            g = jnp.dot(h, wi_ref[:, D_FF + start:D_FF + start + width], preferred_element_type=F32)
            u_ref[:, start:start + width] = (a * (g * jax.nn.sigmoid(g))).astype(BF16)
            start += width
        y = 0.5 * jnp.dot(u_ref[...], wo_ref[...], preferred_element_type=F32)
        z_ref[...] = ALPHA * x_ref[...] + mod_ref[3 * j + 2:3 * j + 3, :] * y

    @pl.when(i == n_tiles)
    def _():
        norm_previous()


def _ffn_call(x, mod, j, wi_sets, wo_sets, which, g, b, tm):
    bsz, length, _ = x.shape
    per_batch = length // tm
    n_tiles = bsz * per_batch
    cur = lambda i: jnp.minimum(i, n_tiles - 1)
    prev = lambda i: jnp.maximum(i - 1, 0)
    tile = lambda pick: pl.BlockSpec((None, tm, D_MODEL), lambda i: (pick(i) // per_batch, pick(i) % per_batch, 0))
    weight = lambda rows, cols: pl.BlockSpec((None, rows, cols), lambda i: (which, 0, 0),
                                             pipeline_mode=pl.Buffered(1))
    return pl.pallas_call(
        functools.partial(_ffn_kernel, j, n_tiles),
        out_shape=jax.ShapeDtypeStruct(x.shape, F32),
        grid=(n_tiles + 1,),
        in_specs=[
            tile(cur),
            pl.BlockSpec((None, N_MOD, D_MODEL), lambda i: (cur(i) // per_batch, 0, 0)),
            weight(D_MODEL, 2 * D_FF),
            weight(D_FF, D_MODEL),
            _resident((1, D_MODEL)),
            _resident((1, D_MODEL)),
        ],
        out_specs=tile(prev),
        scratch_shapes=[pltpu.VMEM((tm, D_FF), BF16), pltpu.VMEM((tm, D_MODEL), F32)],
        compiler_params=pltpu.CompilerParams(dimension_semantics=("arbitrary",), vmem_limit_bytes=VMEM_LIMIT),
        name="ffn",
    )(x, mod, wi_sets, wo_sets, g.reshape(1, D_MODEL), b.reshape(1, D_MODEL))


def _split_heads_columns(w):
    lead = w.shape[:-1]
    quarter = AXIS_DIM // 2
    w = w.reshape(lead + (-1, 2, 2, 2, quarter))
    return jnp.swapaxes(jnp.swapaxes(w, -2, -3), -3, -4).reshape(lead + (-1,))


def _rope(t, cos_ref, sin_ref):
    outs = []
    for c in range(t.shape[1] // LANES):
        tc = t[:, c * LANES:(c + 1) * LANES]
        outs.append(tc * cos_ref[...] + pltpu.roll(tc, LANES // 2, 1) * sin_ref[...])
    return jnp.concatenate(outs, axis=1)


def _dup_heads(t):
    pieces = []
    for h in range(t.shape[1] // HEAD_DIM):
        head = t[:, h * HEAD_DIM:(h + 1) * HEAD_DIM]
        pieces += [head, head]
    return jnp.concatenate(pieces, axis=1)


def _dup_split_heads(t):
    part = HEAD_DIM // 2
    pieces = []
    for h in range(t.shape[1] // HEAD_DIM):
        c0 = (h // 2) * LANES + (h % 2) * part
        first, second = t[:, c0:c0 + part], t[:, c0 + HEAD_DIM:c0 + HEAD_DIM + part]
        pieces += [first, first, second, second]
    return jnp.concatenate(pieces, axis=1)


def _qkv_kernel(x_ref, mod_ref, w_ref, b_ref, cos_ref, sin_ref, q_ref, k_ref, vt_ref):
    h = _modulate(x_ref[...], mod_ref, 1).astype(BF16)
    qkv = jnp.dot(h, w_ref[...], preferred_element_type=F32) + b_ref[...]
    q = qkv[:, :Q_WIDTH] * (HEAD_DIM ** -0.5 * LOG2E)
    q_ref[...] = _rope(q, cos_ref, sin_ref).astype(BF16)
    k = _rope(qkv[:, Q_WIDTH:Q_WIDTH + KV_WIDTH], cos_ref, sin_ref)
    k_ref[...] = _dup_split_heads(k).astype(BF16)
    v = _dup_heads(qkv[:, Q_WIDTH + KV_WIDTH:])
    for blk in range(vt_ref.shape[0]):
        vt_ref[blk] = v[blk * BLOCK:(blk + 1) * BLOCK, :].T.astype(BF16)


def _qkv_call(x, mod, w, b, cos_t, sin_t, tm):
    bsz, length, _ = x.shape
    tok = lambda width: pl.BlockSpec((None, tm, width), lambda bi, t: (bi, t, 0))
    table = pl.BlockSpec((tm, LANES), lambda bi, t: (t, 0))
    return pl.pallas_call(
        _qkv_kernel,
        out_shape=(jax.ShapeDtypeStruct((bsz, length, Q_WIDTH), BF16),
                   jax.ShapeDtypeStruct((bsz, length, KV_DUP_WIDTH), BF16),
                   jax.ShapeDtypeStruct((bsz, length // BLOCK, KV_DUP_WIDTH, BLOCK), BF16)),
        grid=(bsz, length // tm),
        in_specs=[
            tok(D_MODEL),
            pl.BlockSpec((None, N_MOD, D_MODEL), lambda bi, t: (bi, 0, 0)),
            _resident((D_MODEL, QKV_WIDTH)),
            _resident((1, QKV_WIDTH)),
            table, table,
        ],
        out_specs=(tok(Q_WIDTH), tok(KV_DUP_WIDTH),
                   pl.BlockSpec((None, tm // BLOCK, KV_DUP_WIDTH, BLOCK), lambda bi, t: (bi, t, 0, 0))),
        compiler_params=_params(2),
        name="qkv",
    )(x, mod, w, b.reshape(1, QKV_WIDTH), cos_t, sin_t)


def _kv_ctx_kernel(x_ref, mod_ref, w_ref, b_ref, k_ref, vt_ref):
    h = _modulate(x_ref[...], mod_ref, 1).astype(BF16)
    kv = jnp.dot(h, w_ref[...], preferred_element_type=F32) + b_ref[...]
    k_ref[...] = _dup_split_heads(kv[:, :KV_WIDTH]).astype(BF16)
    vt_ref[...] = _dup_heads(kv[:, KV_WIDTH:]).T.astype(BF16)


def _kv_ctx_call(x, mod, w, b):
    bsz, n_ctx, _ = x.shape
    return pl.pallas_call(
        _kv_ctx_kernel,
        out_shape=(jax.ShapeDtypeStruct((bsz, n_ctx, KV_DUP_WIDTH), BF16),
                   jax.ShapeDtypeStruct((bsz, KV_DUP_WIDTH, n_ctx), BF16)),
        grid=(bsz,),
        in_specs=[
            pl.BlockSpec((None, n_ctx, D_MODEL), lambda bi: (bi, 0, 0)),
            _resident((None, N_MOD, D_MODEL)),
            _resident((D_MODEL, 2 * KV_WIDTH)),
            _resident((1, 2 * KV_WIDTH)),
        ],
        out_specs=(pl.BlockSpec((None, n_ctx, KV_DUP_WIDTH), lambda bi: (bi, 0, 0)),
                   pl.BlockSpec((None, KV_DUP_WIDTH, n_ctx), lambda bi: (bi, 0, 0))),
        compiler_params=_params(1),
        name="kv_ctx",
    )(x, mod, w, b.reshape(1, 2 * KV_WIDTH))


def _attn_kernel(n_blocks, n_steps, sink_ref, q_ref, k_ref, vt_ref, kc_ref, vct_ref, xp_ref, modp_ref, wo_ref,
                 g_ref, b_ref, wi32_ref, wo32_ref, out_ref, wi16_ref, wo16_ref, o_ref):
    i = pl.program_id(0)

    @pl.when(i == 0)
    def _():
        o_ref[...] = jnp.zeros_like(o_ref)

    def project(o_prev):
        return jnp.dot(o_prev, wo_ref[...], preferred_element_type=F32)

    def norm(y):
        out_ref[...] = _post_norm(xp_ref[...], y, modp_ref, 1, g_ref, b_ref)

    def round_weights():
        wi16_ref[...] = wi32_ref[...].astype(BF16)
        wo16_ref[...] = wo32_ref[...].astype(BF16)

    @pl.when(i < n_steps)
    def _():
        round_weights()
        o_prev = o_ref[...]
        y = []
        tile_in_batch = i % (n_blocks * BLOCK // q_ref.shape[0])
        _attend(n_blocks, tile_in_batch, sink_ref, q_ref, k_ref, vt_ref, kc_ref, vct_ref, o_ref,
                after_slot={9: lambda: y.append(project(o_prev)), 21: lambda: norm(y[0])})

    @pl.when(i == n_steps)
    def _():
        round_weights()
        norm(project(o_ref[...]))


def _attend(n_blocks, tile_in_batch, sink_ref, q_ref, k_ref, vt_ref, kc_ref, vct_ref, o_ref, after_slot):
    n_sub = q_ref.shape[0] // BLOCK
    qb0 = tile_in_batch * n_sub
    cols2 = 2 * BLOCK
    key_i = lax.broadcasted_iota(jnp.int32, (BLOCK, cols2), 0)
    qry_i = lax.broadcasted_iota(jnp.int32, (BLOCK, cols2), 1) & (BLOCK - 1)
    first_head = lax.broadcasted_iota(jnp.int32, (1, cols2), 1) < BLOCK
    lo_q = (lax.broadcasted_iota(jnp.int32, (cols2, LANES), 1) & (HEAD_DIM - 1)) < HEAD_DIM // 2
    lo_v = lax.broadcasted_iota(jnp.int32, (LANES, 1), 0) < HEAD_DIM

    def block_ids(sub):
        return [jnp.clip(qb0 + sub + d, 0, n_blocks - 1) for d in (-1, 0, 1)]

    slots_per_block = 2 * N_KV_HEADS
    p_t, inv = {}, {}

    def scores(slot):
        sub, rest = divmod(slot, slots_per_block)
        h, half = divmod(rest, 2)
        c0 = 2 * h * LANES
        q_rows = slice(sub * BLOCK, (sub + 1) * BLOCK)
        q_cat = jnp.concatenate([q_ref[q_rows, c0:c0 + LANES], q_ref[q_rows, c0 + LANES:c0 + 2 * LANES]], axis=0)
        keep = lo_q if half == 0 else jnp.logical_not(lo_q)
        q_sel = jnp.where(keep, q_cat, jnp.zeros_like(q_cat))
        lanes = slice(h * LANES, (h + 1) * LANES)
        k_h = jnp.concatenate([k_ref[pl.ds(pl.multiple_of(blk * BLOCK, BLOCK), BLOCK), lanes]
                               for blk in block_ids(sub)] + [kc_ref[:, lanes]], axis=0)
        return lax.dot_general(k_h, q_sel, (((1,), (1,)), ((), ())), preferred_element_type=F32)

    def softmax(slot, s_t):
        sub, rest = divmod(slot, slots_per_block)
        h, half = divmod(rest, 2)
        left_ok = (qry_i <= key_i) & (qb0 + sub > 0)
        right_ok = (key_i <= qry_i) & (qb0 + sub < n_blocks - 1)
        sink = jnp.where(first_head, sink_ref[4 * h + half], sink_ref[4 * h + 2 + half]) * LOG2E
        parts = [jnp.where(left_ok, s_t[:BLOCK], NEG_INF),
                 s_t[BLOCK:2 * BLOCK],
                 jnp.where(right_ok, s_t[2 * BLOCK:SPAN], NEG_INF)]
        parts += [s_t[c:c + BLOCK] for c in range(SPAN, s_t.shape[0], BLOCK)]
        m = jnp.max(functools.reduce(jnp.maximum, parts), axis=0, keepdims=True)
        m = jnp.maximum(m, sink)
        p = [jnp.exp2(t - m) for t in parts]
        denom = jnp.sum(functools.reduce(jnp.add, p), axis=0, keepdims=True) + jnp.exp2(sink - m)
        inv[slot] = 1.0 / denom
        p_t[slot] = jnp.concatenate(p, axis=0).astype(BF16)

    def weighted_values(pair):
        sub, h = divmod(pair, N_KV_HEADS)
        rows = slice(h * LANES, (h + 1) * LANES)
        v_t = jnp.concatenate([vt_ref[blk, rows, :] for blk in block_ids(sub)] + [vct_ref[rows, :]], axis=1)
        v_lo = jnp.where(lo_v, v_t, jnp.zeros_like(v_t))
        v_hi = jnp.where(lo_v, jnp.zeros_like(v_t), v_t)
        acc_t = (jnp.dot(v_lo, p_t.pop(2 * pair), preferred_element_type=F32)
                 + jnp.dot(v_hi, p_t.pop(2 * pair + 1), preferred_element_type=F32))
        out = (acc_t * jnp.where(lo_v, inv.pop(2 * pair), inv.pop(2 * pair + 1))).T.astype(BF16)
        c0 = 2 * h * LANES
        q_rows = slice(sub * BLOCK, (sub + 1) * BLOCK)
        o_ref[q_rows, c0:c0 + LANES] = out[:BLOCK]
        o_ref[q_rows, c0 + LANES:c0 + 2 * LANES] = out[BLOCK:]

    n_slots = n_sub * slots_per_block
    ahead = 1
    s_t = {slot: scores(slot) for slot in range(ahead)}
    for slot in range(n_slots):
        if slot + ahead < n_slots:
            s_t[slot + ahead] = scores(slot + ahead)
        softmax(slot, s_t.pop(slot))
        if slot % 2 == 0 and slot > 0:
            weighted_values(slot // 2 - 1)
        if slot in after_slot:
            after_slot[slot]()
    weighted_values(n_slots // 2 - 1)


def _attn_call(q, k, vt, kc, vct, sink, x, mod, wo, g, b, wi32, wo32, tq):
    bsz, length, _ = q.shape
    n_blocks = length // BLOCK
    per_batch_tiles = length // tq
    n_steps = bsz * per_batch_tiles
    n_later = wi32.shape[0] - 1
    wi_cols, wo_rows = 2 * D_FF // WEIGHT_SLABS, D_FF // WEIGHT_SLABS
    assert n_later * WEIGHT_SLABS <= n_steps + 1
    slab = lambda i: jnp.minimum(i, n_later * WEIGHT_SLABS - 1)
    cur = lambda i: jnp.minimum(i, n_steps - 1)
    prev = lambda i: jnp.maximum(i - 1, 0)
    tile = lambda pick: pl.BlockSpec(
        (None, tq, D_MODEL), lambda i, *_: (pick(i) // per_batch_tiles, pick(i) % per_batch_tiles, 0))
    per_batch = lambda arr: pl.BlockSpec(
        (None,) + arr.shape[1:], lambda i, *_: (cur(i) // per_batch_tiles,) + (0,) * (arr.ndim - 1))
    return pl.pallas_call(
        functools.partial(_attn_kernel, n_blocks, n_steps),
        out_shape=(jax.ShapeDtypeStruct(x.shape, F32),
                   jax.ShapeDtypeStruct((n_later,) + wi32.shape[1:], BF16),
                   jax.ShapeDtypeStruct((n_later,) + wo32.shape[1:], BF16)),
        grid_spec=pltpu.PrefetchScalarGridSpec(
            num_scalar_prefetch=1,
            grid=(n_steps + 1,),
            in_specs=[
                tile(cur),
                per_batch(k), per_batch(vt), per_batch(kc), per_batch(vct),
                tile(prev),
                pl.BlockSpec((None, N_MOD, D_MODEL), lambda i, *_: (prev(i) // per_batch_tiles, 0, 0)),
                _resident((D_MODEL, D_MODEL)),
                _resident((1, D_MODEL)),
                _resident((1, D_MODEL)),
                pl.BlockSpec((None, D_MODEL, wi_cols),
                             lambda i, *_: (1 + slab(i) // WEIGHT_SLABS, 0, slab(i) % WEIGHT_SLABS)),
                pl.BlockSpec((None, wo_rows, D_MODEL),
                             lambda i, *_: (1 + slab(i) // WEIGHT_SLABS, slab(i) % WEIGHT_SLABS, 0)),
            ],
            out_specs=(tile(prev),
                       pl.BlockSpec((None, D_MODEL, wi_cols),
                                    lambda i, *_: (slab(i) // WEIGHT_SLABS, 0, slab(i) % WEIGHT_SLABS)),
                       pl.BlockSpec((None, wo_rows, D_MODEL),
                                    lambda i, *_: (slab(i) // WEIGHT_SLABS, slab(i) % WEIGHT_SLABS, 0))),
            scratch_shapes=[pltpu.VMEM((tq, Q_WIDTH), BF16)],
        ),
        compiler_params=pltpu.CompilerParams(dimension_semantics=("arbitrary",), vmem_limit_bytes=VMEM_LIMIT),
        name="attn",
    )(sink, q, k, vt, kc, vct, x, mod, wo, g.reshape(1, D_MODEL), b.reshape(1, D_MODEL), wi32, wo32)


def _reverse_shift(t, first_row, flip_ref):
    rows = t.shape[0]
    blocks = [jnp.dot(flip_ref[...], t[r0:r0 + FLIP, :], preferred_element_type=F32)
              for r0 in range(rows - FLIP, -FLIP, -FLIP)]
    shifted = pltpu.roll(jnp.concatenate(blocks, axis=0), 1, 0)
    is_first = lax.broadcasted_iota(jnp.int32, shifted.shape, 0) == 0
    return jnp.where(is_first, first_row, shifted)


def _channel_dft(t, w_ref):
    return jnp.concatenate(
        [jnp.dot(t[:, g0:g0 + FOURIER_GROUP_CH], w_ref[...], preferred_element_type=F32)
         for g0 in range(0, D_MODEL, FOURIER_GROUP_CH)], axis=1)


def _dft_fold_kernel(lo_ref, up_ref, mod_ref, cc_ref, sc_ref, flip_ref, ab_ref, an_ref, carry_ref):
    @pl.when(pl.program_id(1) == 0)
    def _():
        carry_ref[...] = jnp.zeros_like(carry_ref)

    h_lo = _modulate(lo_ref[...], mod_ref, 1)
    h_up = _modulate(up_ref[...], mod_ref, 1).astype(BF16)
    partner = _reverse_shift(h_up, carry_ref[0:1, :], flip_ref)
    carry_ref[...] = h_up[0:8, :].astype(F32)
    ab_ref[0] = _channel_dft((h_lo + partner).astype(BF16), cc_ref).astype(BF16)
    ab_ref[1] = _channel_dft((h_lo - partner).astype(BF16), sc_ref).astype(BF16)
    an_ref[...] = _channel_dft(h_up[0:8, :], cc_ref)


def _dft_fold_call(x4, mod, cc, sc, flip, tm):
    bsz, _, half, _ = x4.shape
    n_t = half // tm
    ch = FOURIER_GROUP_CH
    return pl.pallas_call(
        _dft_fold_kernel,
        out_shape=(jax.ShapeDtypeStruct((bsz, 2, half, D_MODEL), BF16),
                   jax.ShapeDtypeStruct((bsz, 8, D_MODEL), F32)),
        grid=(bsz, n_t),
        in_specs=[
            pl.BlockSpec((None, None, tm, D_MODEL), lambda bi, t: (bi, 0, t, 0)),
            pl.BlockSpec((None, None, tm, D_MODEL), lambda bi, t: (bi, 1, n_t - 1 - t, 0)),
            pl.BlockSpec((None, N_MOD, D_MODEL), lambda bi, t: (bi, 0, 0)),
            _resident((ch, ch)),
            _resident((ch, ch)),
            _resident((FLIP, FLIP)),
        ],
        out_specs=(pl.BlockSpec((None, 2, tm, D_MODEL), lambda bi, t: (bi, 0, t, 0)),
                   pl.BlockSpec((None, 8, D_MODEL), lambda bi, t: (bi, 0, 0))),
        scratch_shapes=[pltpu.VMEM((8, D_MODEL), F32)],
        compiler_params=pltpu.CompilerParams(dimension_semantics=("parallel", "arbitrary"),
                                             vmem_limit_bytes=VMEM_LIMIT),
        name="dft_fold",
    )(x4, x4, mod, cc, sc, flip)


def _dft_seq_kernel(norm, n_t, wlo_ref, wup_ref, ab_ref, an_ref, alt_ref, flip_ref, x_ref, mod_ref, wo_ref,
                    g_ref, beta_ref, out_ref, carry_ref):
    t = pl.program_id(1)
    tk = wlo_ref.shape[0]
    ab = ab_ref[...]
    a_n = an_ref[0:1, :]
    row = lax.broadcasted_iota(jnp.int32, (tk, 1), 0)

    def alternating(k0):
        return (1 - 2 * ((k0 + row) & 1)).astype(F32)

    f_lo = (jnp.dot(wlo_ref[...], ab, preferred_element_type=F32) + alternating(t * tk) * a_n) * norm
    f_mir = (jnp.dot(wup_ref[...], ab, preferred_element_type=F32) + alternating((n_t - 1 - t) * tk) * a_n) * norm
    f_mir = f_mir.astype(BF16)

    @pl.when(t == 0)
    def _():
        f_n = (jnp.dot(alt_ref[...], ab, preferred_element_type=F32)[0:1, :] + a_n) * norm
        carry_ref[0:1, :] = f_n.astype(BF16).astype(F32)

    f_up = _reverse_shift(f_mir, carry_ref[0:1, :], flip_ref)
    carry_ref[...] = f_mir[0:8, :].astype(F32)
    y_lo = jnp.dot(f_lo.astype(BF16), wo_ref[...], preferred_element_type=F32)
    y_up = jnp.dot(f_up.astype(BF16), wo_ref[...], preferred_element_type=F32)
    out_ref[0] = _post_norm(x_ref[0], y_lo, mod_ref, 1, g_ref, beta_ref)
    out_ref[1] = _post_norm(x_ref[1], y_up, mod_ref, 1, g_ref, beta_ref)


def _dft_seq_call(w_lo, w_up, ab, a_n, alt, flip, x4, mod, wo, g, beta, tk):
    bsz, _, half, _ = x4.shape
    n_t = half // tk
    norm = float(1.0 / np.sqrt(2 * half * FOURIER_GROUP_CH))
    tok = pl.BlockSpec((None, 2, tk, D_MODEL), lambda bi, t: (bi, 0, t, 0))
    return pl.pallas_call(
        functools.partial(_dft_seq_kernel, norm, n_t),
        out_shape=jax.ShapeDtypeStruct(x4.shape, F32),
        grid=(bsz, n_t),
        in_specs=[
            pl.BlockSpec((tk, 2 * half), lambda bi, t: (t, 0)),
            pl.BlockSpec((tk, 2 * half), lambda bi, t: (n_t - 1 - t, 0)),
            pl.BlockSpec((None, 2 * half, D_MODEL), lambda bi, t: (bi, 0, 0)),
            pl.BlockSpec((None, 8, D_MODEL), lambda bi, t: (bi, 0, 0)),
            _resident((8, 2 * half)),
            _resident((FLIP, FLIP)),
            tok,
            pl.BlockSpec((None, N_MOD, D_MODEL), lambda bi, t: (bi, 0, 0)),
            _resident((D_MODEL, D_MODEL)),
            _resident((1, D_MODEL)),
            _resident((1, D_MODEL)),
        ],
        out_specs=tok,
        scratch_shapes=[pltpu.VMEM((8, D_MODEL), F32)],
        compiler_params=pltpu.CompilerParams(dimension_semantics=("parallel", "arbitrary"),
                                             vmem_limit_bytes=VMEM_LIMIT),
        name="dft_seq",
    )(w_lo, w_up, ab, a_n, alt, flip, x4, mod, wo, g.reshape(1, D_MODEL), beta.reshape(1, D_MODEL))


def _rope_tables(n_tokens):
    rows = n_tokens // GRID_W
    row = jnp.repeat(jnp.arange(rows), GRID_W).astype(F32)
    col = jnp.tile(jnp.arange(GRID_W), rows).astype(F32)
    inv = ROPE_BASE ** (-jnp.arange(0, AXIS_DIM, 2, dtype=F32) / AXIS_DIM)
    ang = jnp.concatenate([row[:, None] * inv, col[:, None] * inv], axis=-1)
    cos_a, sin_a = jnp.cos(ang), jnp.sin(ang)
    return (jnp.concatenate([cos_a] * 4, axis=-1),
            jnp.concatenate([-sin_a, -sin_a, sin_a, sin_a], axis=-1))


def _dft_tables(n, rows):
    step = TWIDDLE_STEP if rows % TWIDDLE_STEP == 0 else rows
    col = jnp.arange(rows, dtype=jnp.int32)[None, :]

    def thin(k):
        ang = ((k[:, None] * col) % n).astype(F32) * (2.0 * np.pi / n)
        return jnp.cos(ang), jnp.sin(ang)

    cos1, sin1 = (t[:, None, :] for t in thin(jnp.arange(0, rows, step, dtype=jnp.int32)))
    cos0, sin0 = (t[None, :, :] for t in thin(jnp.arange(step, dtype=jnp.int32)))
    return ((cos1 * cos0 - sin1 * sin0).reshape(rows, rows),
            (sin1 * cos0 + cos1 * sin0).reshape(rows, rows))


def kernel(x, c, ctx, c_ctx, mod_w, mod_b, ln_g, ln_b, ffn_wi, ffn_wo,
           attn_wqkv, attn_bqkv, attn_wo, attn_sink, fourier_wo):
    bsz, seq, _ = x.shape
    n_ctx = ctx.shape[1]
    assert DEPTH == 2 and seq % BLOCK == 0 and seq % GRID_W == 0

    cv = jnp.concatenate([c, c_ctx[None, :], jnp.zeros((MOD_ROWS - bsz - 1, D_MODEL), F32)], axis=0)
    mod = _mod_call(cv, mod_w, mod_b).reshape(DEPTH, MOD_ROWS, N_MOD, D_MODEL)
    wi32 = ffn_wi.reshape(2 * DEPTH, D_MODEL, 2 * D_FF)
    wo32 = ffn_wo.reshape(2 * DEPTH, D_FF, D_MODEL)
    wi0, wo0 = wi32[:1].astype(BF16), wo32[:1].astype(BF16)

    mod_lat, mod_ctx = mod[0, :bsz], mod[0, bsz:bsz + 1]
    g, b = ln_g[0], ln_b[0]
    x = _ffn_call(x, mod_lat, 0, wi0, wo0, 0, g[0], b[0], tm=FFN_TILE)
    ctx_s = _ffn_call(ctx.reshape(1, bsz * n_ctx, D_MODEL), mod_ctx, 0, wi0, wo0, 0, g[0], b[0], tm=FFN_TILE)

    qk = Q_WIDTH + KV_WIDTH
    split = lambda t: jnp.concatenate([_split_heads_columns(t[..., :qk]), t[..., qk:]], axis=-1)
    w_qkv, b_qkv = split(attn_wqkv[0]).astype(BF16), split(attn_bqkv[0])
    cos_t, sin_t = _rope_tables(seq)
    q, k, vt = _qkv_call(x, mod_lat, w_qkv, b_qkv, cos_t, sin_t, tm=512)
    kc, vct = _kv_ctx_call(ctx_s.reshape(bsz, n_ctx, D_MODEL), mod_ctx, w_qkv[:, Q_WIDTH:], b_qkv[Q_WIDTH:])
    x, wi, wo = _attn_call(q, k, vt, kc, vct, attn_sink[0], x, mod_lat, attn_wo[0].astype(BF16), g[1], b[1],
                           wi32, wo32, tq=4 * BLOCK)
    x = _ffn_call(x, mod_lat, 2, wi, wo, 0, g[2], b[2], tm=FFN_TILE)

    mod_lat = mod[1, :bsz]
    g, b = ln_g[1], ln_b[1]
    x = _ffn_call(x, mod_lat, 0, wi, wo, 1, g[0], b[0], tm=FFN_TILE)
    half = seq // 2
    cos_c, sin_c = _dft_tables(FOURIER_GROUP_CH, FOURIER_GROUP_CH)
    cos_s, sin_s = _dft_tables(seq, half)
    w_lo = jnp.concatenate([cos_s, -sin_s], axis=1).astype(BF16)
    w_up = jnp.concatenate([cos_s, sin_s], axis=1).astype(BF16)
    alt = jnp.zeros((8, seq), F32).at[0, :half].set(1.0 - 2.0 * (jnp.arange(half) % 2)).astype(BF16)
    anti = jnp.arange(FLIP)[:, None] + jnp.arange(FLIP)[None, :] == FLIP - 1
    flip = anti.astype(BF16)
    x4 = x.reshape(bsz, 2, half, D_MODEL)
    ab, a_n = _dft_fold_call(x4, mod_lat, cos_c.astype(BF16), sin_c.astype(BF16), flip, tm=512)
    x4 = _dft_seq_call(w_lo, w_up, ab.reshape(bsz, seq, D_MODEL), a_n, alt, flip, x4, mod_lat,
                       fourier_wo[0].astype(BF16), g[1], b[1], tk=512)
    x = x4.reshape(bsz, seq, D_MODEL)
    x = _ffn_call(x, mod_lat, 2, wi, wo, 2, g[2], b[2], tm=FFN_TILE)
    return x
```

```python
import functools

import jax
import jax.numpy as jnp
import numpy as np
from jax import lax
from jax.experimental import pallas as pl
from jax.experimental.pallas import tpu as pltpu

D_MODEL = 1024
DEPTH = 2
GRID_W = 64
N_HEADS = 16
N_KV_HEADS = 4
HEAD_DIM = 64
GROUP = N_HEADS // N_KV_HEADS
Q_WIDTH = N_HEADS * HEAD_DIM
KV_WIDTH = N_KV_HEADS * HEAD_DIM
QKV_WIDTH = Q_WIDTH + 2 * KV_WIDTH
KV_DUP_WIDTH = 2 * KV_WIDTH
WINDOW = 128
BLOCK = 128
SPAN = BLOCK + 2 * WINDOW
ROPE_BASE = 10000.0
AXIS_DIM = HEAD_DIM // 2
FOURIER_GROUPS = 4
FOURIER_GROUP_CH = D_MODEL // FOURIER_GROUPS
D_FF = 2816
N_MOD = 9
LN_EPS = 1e-5
ALPHA = (2.0 * DEPTH) ** 0.25
NEG_INF = -1e30
LOG2E = float(np.log2(np.e))

LANES = 128
MOD_ROWS = 24
VMEM_LIMIT = 60 * 1024 * 1024
FFN_TILE = 1024
FFN_CHUNKS = (1024, 1024, 768)
FLIP = 256
TWIDDLE_STEP = 32
WEIGHT_SLABS = 11

BF16 = jnp.bfloat16
F32 = jnp.float32


def _params(n_axes):
    return pltpu.CompilerParams(dimension_semantics=("parallel",) * n_axes,
                                vmem_limit_bytes=VMEM_LIMIT)


def _resident(shape):
    return pl.BlockSpec(shape, lambda *_: (0,) * len(shape), pipeline_mode=pl.Buffered(1))


def _layer_norm(z, g, b):
    mu = jnp.mean(z, axis=-1, keepdims=True)
    d = z - mu
    var = jnp.mean(d * d, axis=-1, keepdims=True)
    return d * lax.rsqrt(var + LN_EPS) * g + b


def _modulate(x, mod_ref, j):
    shift = mod_ref[3 * j:3 * j + 1, :]
    scale = mod_ref[3 * j + 1:3 * j + 2, :]
    return x * (1.0 + scale) + shift


def _post_norm(x, y, mod_ref, j, g_ref, b_ref):
    gate = mod_ref[3 * j + 2:3 * j + 3, :]
    return _layer_norm(ALPHA * x + gate * y, g_ref[...], b_ref[...])


def _mod_kernel(cv_ref, w_ref, b_ref, o_ref):
    cv = cv_ref[...]
    s = (cv * jax.nn.sigmoid(cv)).astype(BF16)
    o_ref[...] = jnp.dot(s, w_ref[...].astype(BF16), preferred_element_type=F32) + b_ref[...]


def _mod_call(cv, mod_w, mod_b):
    tn = D_MODEL
    n_out = N_MOD * D_MODEL
    return pl.pallas_call(
        _mod_kernel,
        out_shape=jax.ShapeDtypeStruct((DEPTH, MOD_ROWS, n_out), F32),
        grid=(DEPTH, n_out // tn),
        in_specs=[
            pl.BlockSpec((MOD_ROWS, D_MODEL), lambda i, n: (0, 0)),
            pl.BlockSpec((None, D_MODEL, tn), lambda i, n: (i, 0, n)),
            pl.BlockSpec((None, 1, tn), lambda i, n: (i, 0, n)),
        ],
        out_specs=pl.BlockSpec((None, MOD_ROWS, tn), lambda i, n: (i, 0, n)),
        compiler_params=_params(2),
        name="mod",
    )(cv, mod_w, mod_b.reshape(DEPTH, 1, n_out))


def _ffn_kernel(j, n_tiles, x_ref, mod_ref, wi_ref, wo_ref, g_ref, b_ref, o_ref, u_ref, z_ref):
    i = pl.program_id(0)

    @pl.when(i == 0)
    def _():
        z_ref[...] = jnp.zeros_like(z_ref)

    def norm_previous():
        o_ref[...] = _layer_norm(z_ref[...], g_ref[...], b_ref[...])

    @pl.when(i < n_tiles)
    def _():
        norm_previous()
        h = _modulate(x_ref[...], mod_ref, j).astype(BF16)
        start = 0
        for width in FFN_CHUNKS:
            a = jnp.dot(h, wi_ref[:, start:start + width], preferred_element_type=F32)
            g = jnp.dot(h, wi_ref[:, D_FF + start:D_FF + start + width], preferred_element_type=F32)
            u_ref[:, start:start + width] = (a * (g * jax.nn.sigmoid(g))).astype(BF16)
            start += width
        y = 0.5 * jnp.dot(u_ref[...], wo_ref[...], preferred_element_type=F32)
        z_ref[...] = ALPHA * x_ref[...] + mod_ref[3 * j + 2:3 * j + 3, :] * y

    @pl.when(i == n_tiles)
    def _():
        norm_previous()


def _ffn_call(x, mod, j, wi_sets, wo_sets, which, g, b, tm):
    bsz, length, _ = x.shape
    per_batch = length // tm
    n_tiles = bsz * per_batch
    cur = lambda i: jnp.minimum(i, n_tiles - 1)
    prev = lambda i: jnp.maximum(i - 1, 0)
    tile = lambda pick: pl.BlockSpec((None, tm, D_MODEL), lambda i: (pick(i) // per_batch, pick(i) % per_batch, 0))
    weight = lambda rows, cols: pl.BlockSpec((None, rows, cols), lambda i: (which, 0, 0),
                                             pipeline_mode=pl.Buffered(1))
    return pl.pallas_call(
        functools.partial(_ffn_kernel, j, n_tiles),
        out_shape=jax.ShapeDtypeStruct(x.shape, F32),
        grid=(n_tiles + 1,),
        in_specs=[
            tile(cur),
            pl.BlockSpec((None, N_MOD, D_MODEL), lambda i: (cur(i) // per_batch, 0, 0)),
            weight(D_MODEL, 2 * D_FF),
            weight(D_FF, D_MODEL),
            _resident((1, D_MODEL)),
            _resident((1, D_MODEL)),
        ],
        out_specs=tile(prev),
        scratch_shapes=[pltpu.VMEM((tm, D_FF), BF16), pltpu.VMEM((tm, D_MODEL), F32)],
        compiler_params=pltpu.CompilerParams(dimension_semantics=("arbitrary",), vmem_limit_bytes=VMEM_LIMIT),
        name="ffn",
    )(x, mod, wi_sets, wo_sets, g.reshape(1, D_MODEL), b.reshape(1, D_MODEL))


def _rope(t, cos_ref, sa_ref, sb_ref):
    half = AXIS_DIM // 2
    outs = []
    for c in range(t.shape[1] // LANES):
        tc = t[:, c * LANES:(c + 1) * LANES]
        up = pltpu.roll(tc, LANES - half, 1)
        down = pltpu.roll(tc, half, 1)
        outs.append(tc * cos_ref[...] + up * sa_ref[...] + down * sb_ref[...])
    return jnp.concatenate(outs, axis=1)


def _dup_heads(t):
    pieces = []
    for h in range(t.shape[1] // HEAD_DIM):
        head = t[:, h * HEAD_DIM:(h + 1) * HEAD_DIM]
        pieces += [head, head]
    return jnp.concatenate(pieces, axis=1)


def _qkv_kernel(x_ref, mod_ref, w_ref, b_ref, cos_ref, sa_ref, sb_ref, q_ref, k_ref, vt_ref):
    h = _modulate(x_ref[...], mod_ref, 1).astype(BF16)
    qkv = jnp.dot(h, w_ref[...], preferred_element_type=F32) + b_ref[...]
    q = qkv[:, :Q_WIDTH] * (HEAD_DIM ** -0.5 * LOG2E)
    q_ref[...] = _rope(q, cos_ref, sa_ref, sb_ref).astype(BF16)
    k = _rope(qkv[:, Q_WIDTH:Q_WIDTH + KV_WIDTH], cos_ref, sa_ref, sb_ref)
    k_ref[...] = _dup_heads(k).astype(BF16)
    v = _dup_heads(qkv[:, Q_WIDTH + KV_WIDTH:])
    for blk in range(vt_ref.shape[0]):
        vt_ref[blk] = v[blk * BLOCK:(blk + 1) * BLOCK, :].T.astype(BF16)


def _qkv_call(x, mod, w, b, cos_t, sa_t, sb_t, tm):
    bsz, length, _ = x.shape
    tok = lambda width: pl.BlockSpec((None, tm, width), lambda bi, t: (bi, t, 0))
    table = pl.BlockSpec((tm, LANES), lambda bi, t: (t, 0))
    return pl.pallas_call(
        _qkv_kernel,
        out_shape=(jax.ShapeDtypeStruct((bsz, length, Q_WIDTH), BF16),
                   jax.ShapeDtypeStruct((bsz, length, KV_DUP_WIDTH), BF16),
                   jax.ShapeDtypeStruct((bsz, length // BLOCK, KV_DUP_WIDTH, BLOCK), BF16)),
        grid=(bsz, length // tm),
        in_specs=[
            tok(D_MODEL),
            pl.BlockSpec((None, N_MOD, D_MODEL), lambda bi, t: (bi, 0, 0)),
            _resident((D_MODEL, QKV_WIDTH)),
            _resident((1, QKV_WIDTH)),
            table, table, table,
        ],
        out_specs=(tok(Q_WIDTH), tok(KV_DUP_WIDTH),
                   pl.BlockSpec((None, tm // BLOCK, KV_DUP_WIDTH, BLOCK), lambda bi, t: (bi, t, 0, 0))),
        compiler_params=_params(2),
        name="qkv",
    )(x, mod, w, b.reshape(1, QKV_WIDTH), cos_t, sa_t, sb_t)


def _kv_ctx_kernel(x_ref, mod_ref, w_ref, b_ref, k_ref, vt_ref):
    h = _modulate(x_ref[...], mod_ref, 1).astype(BF16)
    kv = jnp.dot(h, w_ref[...], preferred_element_type=F32) + b_ref[...]
    k_ref[...] = _dup_heads(kv[:, :KV_WIDTH]).astype(BF16)
    vt_ref[...] = _dup_heads(kv[:, KV_WIDTH:]).T.astype(BF16)


def _kv_ctx_call(x, mod, w, b):
    bsz, n_ctx, _ = x.shape
    return pl.pallas_call(
        _kv_ctx_kernel,
        out_shape=(jax.ShapeDtypeStruct((bsz, n_ctx, KV_DUP_WIDTH), BF16),
                   jax.ShapeDtypeStruct((bsz, KV_DUP_WIDTH, n_ctx), BF16)),
        grid=(bsz,),
        in_specs=[
            pl.BlockSpec((None, n_ctx, D_MODEL), lambda bi: (bi, 0, 0)),
            _resident((None, N_MOD, D_MODEL)),
            _resident((D_MODEL, 2 * KV_WIDTH)),
            _resident((1, 2 * KV_WIDTH)),
        ],
        out_specs=(pl.BlockSpec((None, n_ctx, KV_DUP_WIDTH), lambda bi: (bi, 0, 0)),
                   pl.BlockSpec((None, KV_DUP_WIDTH, n_ctx), lambda bi: (bi, 0, 0))),
        compiler_params=_params(1),
        name="kv_ctx",
    )(x, mod, w, b.reshape(1, 2 * KV_WIDTH))


def _attn_kernel(n_blocks, n_steps, sink_ref, q_ref, k_ref, vt_ref, kc_ref, vct_ref, xp_ref, modp_ref, wo_ref,
                 g_ref, b_ref, wi32_ref, wo32_ref, out_ref, wi16_ref, wo16_ref, o_ref):
    i = pl.program_id(0)

    @pl.when(i == 0)
    def _():
        o_ref[...] = jnp.zeros_like(o_ref)

    def project(o_prev):
        return jnp.dot(o_prev, wo_ref[...], preferred_element_type=F32)

    def norm(y):
        out_ref[...] = _post_norm(xp_ref[...], y, modp_ref, 1, g_ref, b_ref)

    def round_weights():
        wi16_ref[...] = wi32_ref[...].astype(BF16)
        wo16_ref[...] = wo32_ref[...].astype(BF16)

    @pl.when(i < n_steps)
    def _():
        round_weights()
        o_prev = o_ref[...]
        y = []
        tile_in_batch = i % (n_blocks * BLOCK // q_ref.shape[0])
        _attend(n_blocks, tile_in_batch, sink_ref, q_ref, k_ref, vt_ref, kc_ref, vct_ref, o_ref,
                after_slot={9: lambda: y.append(project(o_prev)), 21: lambda: norm(y[0])})

    @pl.when(i == n_steps)
    def _():
        round_weights()
        norm(project(o_ref[...]))


def _attend(n_blocks, tile_in_batch, sink_ref, q_ref, k_ref, vt_ref, kc_ref, vct_ref, o_ref, after_slot):
    n_sub = q_ref.shape[0] // BLOCK
    qb0 = tile_in_batch * n_sub
    cols2 = 2 * BLOCK
    key_i = lax.broadcasted_iota(jnp.int32, (BLOCK, cols2), 0)
    qry_i = lax.broadcasted_iota(jnp.int32, (BLOCK, cols2), 1) & (BLOCK - 1)
    first_head = lax.broadcasted_iota(jnp.int32, (1, cols2), 1) < BLOCK
    lo_q = lax.broadcasted_iota(jnp.int32, (cols2, LANES), 1) < HEAD_DIM
    lo_v = lax.broadcasted_iota(jnp.int32, (LANES, 1), 0) < HEAD_DIM

    def block_ids(sub):
        return [jnp.clip(qb0 + sub + d, 0, n_blocks - 1) for d in (-1, 0, 1)]

    slots_per_block = 2 * N_KV_HEADS
    p_t, inv = {}, {}

    def scores(slot):
        sub, rest = divmod(slot, slots_per_block)
        h, half = divmod(rest, 2)
        c0 = 2 * h * LANES
        q_rows = slice(sub * BLOCK, (sub + 1) * BLOCK)
        q_cat = jnp.concatenate([q_ref[q_rows, c0:c0 + LANES], q_ref[q_rows, c0 + LANES:c0 + 2 * LANES]], axis=0)
        keep = lo_q if half == 0 else jnp.logical_not(lo_q)
        q_sel = jnp.where(keep, q_cat, jnp.zeros_like(q_cat))
        lanes = slice(h * LANES, (h + 1) * LANES)
        k_h = jnp.concatenate([k_ref[pl.ds(pl.multiple_of(blk * BLOCK, BLOCK), BLOCK), lanes]
                               for blk in block_ids(sub)] + [kc_ref[:, lanes]], axis=0)
        return lax.dot_general(k_h, q_sel, (((1,), (1,)), ((), ())), preferred_element_type=F32)

    def softmax(slot, s_t):
        sub, rest = divmod(slot, slots_per_block)
        h, half = divmod(rest, 2)
        left_ok = (qry_i <= key_i) & (qb0 + sub > 0)
        right_ok = (key_i <= qry_i) & (qb0 + sub < n_blocks - 1)
        sink = jnp.where(first_head, sink_ref[4 * h + half], sink_ref[4 * h + 2 + half]) * LOG2E
        parts = [jnp.where(left_ok, s_t[:BLOCK], NEG_INF),
                 s_t[BLOCK:2 * BLOCK],
                 jnp.where(right_ok, s_t[2 * BLOCK:SPAN], NEG_INF)]
        parts += [s_t[c:c + BLOCK] for c in range(SPAN, s_t.shape[0], BLOCK)]
        m = jnp.max(functools.reduce(jnp.maximum, parts), axis=0, keepdims=True)
        m = jnp.maximum(m, sink)
        p = [jnp.exp2(t - m) for t in parts]
        denom = jnp.sum(functools.reduce(jnp.add, p), axis=0, keepdims=True) + jnp.exp2(sink - m)
        inv[slot] = 1.0 / denom
        p_t[slot] = jnp.concatenate(p, axis=0).astype(BF16)

    def weighted_values(pair):
        sub, h = divmod(pair, N_KV_HEADS)
        rows = slice(h * LANES, (h + 1) * LANES)
        v_t = jnp.concatenate([vt_ref[blk, rows, :] for blk in block_ids(sub)] + [vct_ref[rows, :]], axis=1)
        v_lo = jnp.where(lo_v, v_t, jnp.zeros_like(v_t))
        v_hi = jnp.where(lo_v, jnp.zeros_like(v_t), v_t)
        acc_t = (jnp.dot(v_lo, p_t.pop(2 * pair), preferred_element_type=F32)
                 + jnp.dot(v_hi, p_t.pop(2 * pair + 1), preferred_element_type=F32))
        out = (acc_t * jnp.where(lo_v, inv.pop(2 * pair), inv.pop(2 * pair + 1))).T.astype(BF16)
        c0 = 2 * h * LANES
        q_rows = slice(sub * BLOCK, (sub + 1) * BLOCK)
        o_ref[q_rows, c0:c0 + LANES] = out[:BLOCK]
        o_ref[q_rows, c0 + LANES:c0 + 2 * LANES] = out[BLOCK:]

    n_slots = n_sub * slots_per_block
    ahead = 1
    s_t = {slot: scores(slot) for slot in range(ahead)}
    for slot in range(n_slots):
        if slot + ahead < n_slots:
            s_t[slot + ahead] = scores(slot + ahead)
        softmax(slot, s_t.pop(slot))
        if slot % 2 == 0 and slot > 0:
            weighted_values(slot // 2 - 1)
        if slot in after_slot:
            after_slot[slot]()
    weighted_values(n_slots // 2 - 1)


def _attn_call(q, k, vt, kc, vct, sink, x, mod, wo, g, b, wi32, wo32, tq):
    bsz, length, _ = q.shape
    n_blocks = length // BLOCK
    per_batch_tiles = length // tq
    n_steps = bsz * per_batch_tiles
    n_later = wi32.shape[0] - 1
    wi_cols, wo_rows = 2 * D_FF // WEIGHT_SLABS, D_FF // WEIGHT_SLABS
    assert n_later * WEIGHT_SLABS <= n_steps + 1
    slab = lambda i: jnp.minimum(i, n_later * WEIGHT_SLABS - 1)
    cur = lambda i: jnp.minimum(i, n_steps - 1)
    prev = lambda i: jnp.maximum(i - 1, 0)
    tile = lambda pick: pl.BlockSpec(
        (None, tq, D_MODEL), lambda i, *_: (pick(i) // per_batch_tiles, pick(i) % per_batch_tiles, 0))
    per_batch = lambda arr: pl.BlockSpec(
        (None,) + arr.shape[1:], lambda i, *_: (cur(i) // per_batch_tiles,) + (0,) * (arr.ndim - 1))
    return pl.pallas_call(
        functools.partial(_attn_kernel, n_blocks, n_steps),
        out_shape=(jax.ShapeDtypeStruct(x.shape, F32),
                   jax.ShapeDtypeStruct((n_later,) + wi32.shape[1:], BF16),
                   jax.ShapeDtypeStruct((n_later,) + wo32.shape[1:], BF16)),
        grid_spec=pltpu.PrefetchScalarGridSpec(
            num_scalar_prefetch=1,
            grid=(n_steps + 1,),
            in_specs=[
                tile(cur),
                per_batch(k), per_batch(vt), per_batch(kc), per_batch(vct),
                tile(prev),
                pl.BlockSpec((None, N_MOD, D_MODEL), lambda i, *_: (prev(i) // per_batch_tiles, 0, 0)),
                _resident((D_MODEL, D_MODEL)),
                _resident((1, D_MODEL)),
                _resident((1, D_MODEL)),
                pl.BlockSpec((None, D_MODEL, wi_cols),
                             lambda i, *_: (1 + slab(i) // WEIGHT_SLABS, 0, slab(i) % WEIGHT_SLABS)),
                pl.BlockSpec((None, wo_rows, D_MODEL),
                             lambda i, *_: (1 + slab(i) // WEIGHT_SLABS, slab(i) % WEIGHT_SLABS, 0)),
            ],
            out_specs=(tile(prev),
                       pl.BlockSpec((None, D_MODEL, wi_cols),
                                    lambda i, *_: (slab(i) // WEIGHT_SLABS, 0, slab(i) % WEIGHT_SLABS)),
                       pl.BlockSpec((None, wo_rows, D_MODEL),
                                    lambda i, *_: (slab(i) // WEIGHT_SLABS, slab(i) % WEIGHT_SLABS, 0))),
            scratch_shapes=[pltpu.VMEM((tq, Q_WIDTH), BF16)],
        ),
        compiler_params=pltpu.CompilerParams(dimension_semantics=("arbitrary",), vmem_limit_bytes=VMEM_LIMIT),
        name="attn",
    )(sink, q, k, vt, kc, vct, x, mod, wo, g.reshape(1, D_MODEL), b.reshape(1, D_MODEL), wi32, wo32)


def _reverse_shift(t, first_row, flip_ref):
    rows = t.shape[0]
    blocks = [jnp.dot(flip_ref[...], t[r0:r0 + FLIP, :], preferred_element_type=F32)
              for r0 in range(rows - FLIP, -FLIP, -FLIP)]
    shifted = pltpu.roll(jnp.concatenate(blocks, axis=0), 1, 0)
    is_first = lax.broadcasted_iota(jnp.int32, shifted.shape, 0) == 0
    return jnp.where(is_first, first_row, shifted)


def _channel_dft(t, w_ref):
    return jnp.concatenate(
        [jnp.dot(t[:, g0:g0 + FOURIER_GROUP_CH], w_ref[...], preferred_element_type=F32)
         for g0 in range(0, D_MODEL, FOURIER_GROUP_CH)], axis=1)


def _dft_fold_kernel(lo_ref, up_ref, mod_ref, cc_ref, sc_ref, flip_ref, ab_ref, an_ref, carry_ref):
    @pl.when(pl.program_id(1) == 0)
    def _():
        carry_ref[...] = jnp.zeros_like(carry_ref)

    h_lo = _modulate(lo_ref[...], mod_ref, 1)
    h_up = _modulate(up_ref[...], mod_ref, 1).astype(BF16)
    partner = _reverse_shift(h_up, carry_ref[0:1, :], flip_ref)
    carry_ref[...] = h_up[0:8, :].astype(F32)
    ab_ref[0] = _channel_dft((h_lo + partner).astype(BF16), cc_ref).astype(BF16)
    ab_ref[1] = _channel_dft((h_lo - partner).astype(BF16), sc_ref).astype(BF16)
    an_ref[...] = _channel_dft(h_up[0:8, :], cc_ref)


def _dft_fold_call(x4, mod, cc, sc, flip, tm):
    bsz, _, half, _ = x4.shape
    n_t = half // tm
    ch = FOURIER_GROUP_CH
    return pl.pallas_call(
        _dft_fold_kernel,
        out_shape=(jax.ShapeDtypeStruct((bsz, 2, half, D_MODEL), BF16),
                   jax.ShapeDtypeStruct((bsz, 8, D_MODEL), F32)),
        grid=(bsz, n_t),
        in_specs=[
            pl.BlockSpec((None, None, tm, D_MODEL), lambda bi, t: (bi, 0, t, 0)),
            pl.BlockSpec((None, None, tm, D_MODEL), lambda bi, t: (bi, 1, n_t - 1 - t, 0)),
            pl.BlockSpec((None, N_MOD, D_MODEL), lambda bi, t: (bi, 0, 0)),
            _resident((ch, ch)),
            _resident((ch, ch)),
            _resident((FLIP, FLIP)),
        ],
        out_specs=(pl.BlockSpec((None, 2, tm, D_MODEL), lambda bi, t: (bi, 0, t, 0)),
                   pl.BlockSpec((None, 8, D_MODEL), lambda bi, t: (bi, 0, 0))),
        scratch_shapes=[pltpu.VMEM((8, D_MODEL), F32)],
        compiler_params=pltpu.CompilerParams(dimension_semantics=("parallel", "arbitrary"),
                                             vmem_limit_bytes=VMEM_LIMIT),
        name="dft_fold",
    )(x4, x4, mod, cc, sc, flip)


def _dft_seq_kernel(norm, n_t, wlo_ref, wup_ref, ab_ref, an_ref, alt_ref, flip_ref, x_ref, mod_ref, wo_ref,
                    g_ref, beta_ref, out_ref, carry_ref):
    t = pl.program_id(1)
    tk = wlo_ref.shape[0]
    ab = ab_ref[...]
    a_n = an_ref[0:1, :]
    row = lax.broadcasted_iota(jnp.int32, (tk, 1), 0)

    def alternating(k0):
        return (1 - 2 * ((k0 + row) & 1)).astype(F32)

    f_lo = (jnp.dot(wlo_ref[...], ab, preferred_element_type=F32) + alternating(t * tk) * a_n) * norm
    f_mir = (jnp.dot(wup_ref[...], ab, preferred_element_type=F32) + alternating((n_t - 1 - t) * tk) * a_n) * norm
    f_mir = f_mir.astype(BF16)

    @pl.when(t == 0)
    def _():
        f_n = (jnp.dot(alt_ref[...], ab, preferred_element_type=F32)[0:1, :] + a_n) * norm
        carry_ref[0:1, :] = f_n.astype(BF16).astype(F32)

    f_up = _reverse_shift(f_mir, carry_ref[0:1, :], flip_ref)
    carry_ref[...] = f_mir[0:8, :].astype(F32)
    y_lo = jnp.dot(f_lo.astype(BF16), wo_ref[...], preferred_element_type=F32)
    y_up = jnp.dot(f_up.astype(BF16), wo_ref[...], preferred_element_type=F32)
    out_ref[0] = _post_norm(x_ref[0], y_lo, mod_ref, 1, g_ref, beta_ref)
    out_ref[1] = _post_norm(x_ref[1], y_up, mod_ref, 1, g_ref, beta_ref)


def _dft_seq_call(w_lo, w_up, ab, a_n, alt, flip, x4, mod, wo, g, beta, tk):
    bsz, _, half, _ = x4.shape
    n_t = half // tk
    norm = float(1.0 / np.sqrt(2 * half * FOURIER_GROUP_CH))
    tok = pl.BlockSpec((None, 2, tk, D_MODEL), lambda bi, t: (bi, 0, t, 0))
    return pl.pallas_call(
        functools.partial(_dft_seq_kernel, norm, n_t),
        out_shape=jax.ShapeDtypeStruct(x4.shape, F32),
        grid=(bsz, n_t),
        in_specs=[
            pl.BlockSpec((tk, 2 * half), lambda bi, t: (t, 0)),
            pl.BlockSpec((tk, 2 * half), lambda bi, t: (n_t - 1 - t, 0)),
            pl.BlockSpec((None, 2 * half, D_MODEL), lambda bi, t: (bi, 0, 0)),
            pl.BlockSpec((None, 8, D_MODEL), lambda bi, t: (bi, 0, 0)),
            _resident((8, 2 * half)),
            _resident((FLIP, FLIP)),
            tok,
            pl.BlockSpec((None, N_MOD, D_MODEL), lambda bi, t: (bi, 0, 0)),
            _resident((D_MODEL, D_MODEL)),
            _resident((1, D_MODEL)),
            _resident((1, D_MODEL)),
        ],
        out_specs=tok,
        scratch_shapes=[pltpu.VMEM((8, D_MODEL), F32)],
        compiler_params=pltpu.CompilerParams(dimension_semantics=("parallel", "arbitrary"),
                                             vmem_limit_bytes=VMEM_LIMIT),
        name="dft_seq",
    )(w_lo, w_up, ab, a_n, alt, flip, x4, mod, wo, g.reshape(1, D_MODEL), beta.reshape(1, D_MODEL))


def _rope_tables(n_tokens):
    rows = n_tokens // GRID_W
    row = jnp.repeat(jnp.arange(rows), GRID_W).astype(F32)
    col = jnp.tile(jnp.arange(GRID_W), rows).astype(F32)
    inv = ROPE_BASE ** (-jnp.arange(0, AXIS_DIM, 2, dtype=F32) / AXIS_DIM)
    ang_r, ang_c = row[:, None] * inv, col[:, None] * inv
    cos_r, sin_r, cos_c, sin_c = jnp.cos(ang_r), jnp.sin(ang_r), jnp.cos(ang_c), jnp.sin(ang_c)
    zero = jnp.zeros_like(sin_r)
    cos_h = jnp.concatenate([cos_r, cos_r, cos_c, cos_c], axis=-1)
    sa_h = jnp.concatenate([-sin_r, zero, -sin_c, zero], axis=-1)
    sb_h = jnp.concatenate([zero, sin_r, zero, sin_c], axis=-1)
    rep = LANES // HEAD_DIM
    return tuple(jnp.tile(t, (1, rep)) for t in (cos_h, sa_h, sb_h))


def _dft_tables(n, rows):
    step = TWIDDLE_STEP if rows % TWIDDLE_STEP == 0 else rows
    col = jnp.arange(rows, dtype=jnp.int32)[None, :]

    def thin(k):
        ang = ((k[:, None] * col) % n).astype(F32) * (2.0 * np.pi / n)
        return jnp.cos(ang), jnp.sin(ang)

    cos1, sin1 = (t[:, None, :] for t in thin(jnp.arange(0, rows, step, dtype=jnp.int32)))
    cos0, sin0 = (t[None, :, :] for t in thin(jnp.arange(step, dtype=jnp.int32)))
    return ((cos1 * cos0 - sin1 * sin0).reshape(rows, rows),
            (sin1 * cos0 + cos1 * sin0).reshape(rows, rows))


def kernel(x, c, ctx, c_ctx, mod_w, mod_b, ln_g, ln_b, ffn_wi, ffn_wo,
           attn_wqkv, attn_bqkv, attn_wo, attn_sink, fourier_wo):
    bsz, seq, _ = x.shape
    n_ctx = ctx.shape[1]
    assert DEPTH == 2 and seq % BLOCK == 0 and seq % GRID_W == 0

    cv = jnp.concatenate([c, c_ctx[None, :], jnp.zeros((MOD_ROWS - bsz - 1, D_MODEL), F32)], axis=0)
    mod = _mod_call(cv, mod_w, mod_b).reshape(DEPTH, MOD_ROWS, N_MOD, D_MODEL)
    wi32 = ffn_wi.reshape(2 * DEPTH, D_MODEL, 2 * D_FF)
    wo32 = ffn_wo.reshape(2 * DEPTH, D_FF, D_MODEL)
    wi0, wo0 = wi32[:1].astype(BF16), wo32[:1].astype(BF16)

    mod_lat, mod_ctx = mod[0, :bsz], mod[0, bsz:bsz + 1]
    g, b = ln_g[0], ln_b[0]
    x = _ffn_call(x, mod_lat, 0, wi0, wo0, 0, g[0], b[0], tm=FFN_TILE)
    ctx_s = _ffn_call(ctx.reshape(1, bsz * n_ctx, D_MODEL), mod_ctx, 0, wi0, wo0, 0, g[0], b[0], tm=FFN_TILE)

    w_qkv = attn_wqkv[0].astype(BF16)
    cos_t, sa_t, sb_t = _rope_tables(seq)
    q, k, vt = _qkv_call(x, mod_lat, w_qkv, attn_bqkv[0], cos_t, sa_t, sb_t, tm=512)
    kc, vct = _kv_ctx_call(ctx_s.reshape(bsz, n_ctx, D_MODEL), mod_ctx, w_qkv[:, Q_WIDTH:], attn_bqkv[0, Q_WIDTH:])
    x, wi, wo = _attn_call(q, k, vt, kc, vct, attn_sink[0], x, mod_lat, attn_wo[0].astype(BF16), g[1], b[1],
                           wi32, wo32, tq=4 * BLOCK)
    x = _ffn_call(x, mod_lat, 2, wi, wo, 0, g[2], b[2], tm=FFN_TILE)

    mod_lat = mod[1, :bsz]
    g, b = ln_g[1], ln_b[1]
    x = _ffn_call(x, mod_lat, 0, wi, wo, 1, g[0], b[0], tm=FFN_TILE)
    half = seq // 2
    cos_c, sin_c = _dft_tables(FOURIER_GROUP_CH, FOURIER_GROUP_CH)
    cos_s, sin_s = _dft_tables(seq, half)
    w_lo = jnp.concatenate([cos_s, -sin_s], axis=1).astype(BF16)
    w_up = jnp.concatenate([cos_s, sin_s], axis=1).astype(BF16)
    alt = jnp.zeros((8, seq), F32).at[0, :half].set(1.0 - 2.0 * (jnp.arange(half) % 2)).astype(BF16)
    anti = jnp.arange(FLIP)[:, None] + jnp.arange(FLIP)[None, :] == FLIP - 1
    flip = anti.astype(BF16)
    x4 = x.reshape(bsz, 2, half, D_MODEL)
    ab, a_n = _dft_fold_call(x4, mod_lat, cos_c.astype(BF16), sin_c.astype(BF16), flip, tm=512)
    x4 = _dft_seq_call(w_lo, w_up, ab.reshape(bsz, seq, D_MODEL), a_n, alt, flip, x4, mod_lat,
                       fourier_wo[0].astype(BF16), g[1], b[1], tk=512)
    x = x4.reshape(bsz, seq, D_MODEL)
    x = _ffn_call(x, mod_lat, 2, wi, wo, 2, g[2], b[2], tm=FFN_TILE)
    return x
```

```python
import functools

import jax
import jax.numpy as jnp
import numpy as np
from jax import lax
from jax.experimental import pallas as pl
from jax.experimental.pallas import tpu as pltpu

D_MODEL = 1024
DEPTH = 2
GRID_W = 64
N_HEADS = 16
N_KV_HEADS = 4
HEAD_DIM = 64
GROUP = N_HEADS // N_KV_HEADS
Q_WIDTH = N_HEADS * HEAD_DIM
KV_WIDTH = N_KV_HEADS * HEAD_DIM
QKV_WIDTH = Q_WIDTH + 2 * KV_WIDTH
KV_DUP_WIDTH = 2 * KV_WIDTH
WINDOW = 128
BLOCK = 128
SPAN = BLOCK + 2 * WINDOW
ROPE_BASE = 10000.0
AXIS_DIM = HEAD_DIM // 2
FOURIER_GROUPS = 4
FOURIER_GROUP_CH = D_MODEL // FOURIER_GROUPS
D_FF = 2816
N_MOD = 9
LN_EPS = 1e-5
ALPHA = (2.0 * DEPTH) ** 0.25
NEG_INF = -1e30
LOG2E = float(np.log2(np.e))

LANES = 128
MOD_ROWS = 24
VMEM_LIMIT = 60 * 1024 * 1024
FFN_TILE = 1024
FFN_CHUNKS = (1024, 1024, 768)
FLIP = 256
TWIDDLE_STEP = 32
WEIGHT_SLABS = 11

BF16 = jnp.bfloat16
F32 = jnp.float32


def _params(n_axes):
    return pltpu.CompilerParams(dimension_semantics=("parallel",) * n_axes,
                                vmem_limit_bytes=VMEM_LIMIT)


def _resident(shape):
    return pl.BlockSpec(shape, lambda *_: (0,) * len(shape), pipeline_mode=pl.Buffered(1))


def _layer_norm(z, g, b):
    mu = jnp.mean(z, axis=-1, keepdims=True)
    d = z - mu
    var = jnp.mean(d * d, axis=-1, keepdims=True)
    return d * lax.rsqrt(var + LN_EPS) * g + b


def _modulate(x, mod_ref, j):
    shift = mod_ref[3 * j:3 * j + 1, :]
    scale = mod_ref[3 * j + 1:3 * j + 2, :]
    return x * (1.0 + scale) + shift


def _post_norm(x, y, mod_ref, j, g_ref, b_ref):
    gate = mod_ref[3 * j + 2:3 * j + 3, :]
    return _layer_norm(ALPHA * x + gate * y, g_ref[...], b_ref[...])


def _mod_kernel(cv_ref, w_ref, b_ref, o_ref):
    cv = cv_ref[...]
    s = (cv * jax.nn.sigmoid(cv)).astype(BF16)
    o_ref[...] = jnp.dot(s, w_ref[...].astype(BF16), preferred_element_type=F32) + b_ref[...]


def _mod_call(cv, mod_w, mod_b):
    tn = D_MODEL
    n_out = N_MOD * D_MODEL
    return pl.pallas_call(
        _mod_kernel,
        out_shape=jax.ShapeDtypeStruct((DEPTH, MOD_ROWS, n_out), F32),
        grid=(DEPTH, n_out // tn),
        in_specs=[
            pl.BlockSpec((MOD_ROWS, D_MODEL), lambda i, n: (0, 0)),
            pl.BlockSpec((None, D_MODEL, tn), lambda i, n: (i, 0, n)),
            pl.BlockSpec((None, 1, tn), lambda i, n: (i, 0, n)),
        ],
        out_specs=pl.BlockSpec((None, MOD_ROWS, tn), lambda i, n: (i, 0, n)),
        compiler_params=_params(2),
        name="mod",
    )(cv, mod_w, mod_b.reshape(DEPTH, 1, n_out))


def _ffn_kernel(j, n_tiles, x_ref, mod_ref, wi_ref, wo_ref, g_ref, b_ref, o_ref, u_ref, z_ref):
    i = pl.program_id(0)

    @pl.when(i == 0)
    def _():
        z_ref[...] = jnp.zeros_like(z_ref)

    def norm_previous():
        o_ref[...] = _layer_norm(z_ref[...], g_ref[...], b_ref[...])

    @pl.when(i < n_tiles)
    def _():
        norm_previous()
        h = _modulate(x_ref[...], mod_ref, j).astype(BF16)
        start = 0
        for width in FFN_CHUNKS:
            a = jnp.dot(h, wi_ref[:, start:start + width], preferred_element_type=F32)
            g = jnp.dot(h, wi_ref[:, D_FF + start:D_FF + start + width], preferred_element_type=F32)
            u_ref[:, start:start + width] = (a * (g * jax.nn.sigmoid(g))).astype(BF16)
            start += width
        y = 0.5 * jnp.dot(u_ref[...], wo_ref[...], preferred_element_type=F32)
        z_ref[...] = ALPHA * x_ref[...] + mod_ref[3 * j + 2:3 * j + 3, :] * y

    @pl.when(i == n_tiles)
    def _():
        norm_previous()


def _ffn_call(x, mod, j, wi_sets, wo_sets, which, g, b, tm):
    bsz, length, _ = x.shape
    per_batch = length // tm
    n_tiles = bsz * per_batch
    cur = lambda i: jnp.minimum(i, n_tiles - 1)
    prev = lambda i: jnp.maximum(i - 1, 0)
    tile = lambda pick: pl.BlockSpec((None, tm, D_MODEL), lambda i: (pick(i) // per_batch, pick(i) % per_batch, 0))
    weight = lambda rows, cols: pl.BlockSpec((None, rows, cols), lambda i: (which, 0, 0),
                                             pipeline_mode=pl.Buffered(1))
    return pl.pallas_call(
        functools.partial(_ffn_kernel, j, n_tiles),
        out_shape=jax.ShapeDtypeStruct(x.shape, F32),
        grid=(n_tiles + 1,),
        in_specs=[
            tile(cur),
            pl.BlockSpec((None, N_MOD, D_MODEL), lambda i: (cur(i) // per_batch, 0, 0)),
            weight(D_MODEL, 2 * D_FF),
            weight(D_FF, D_MODEL),
            _resident((1, D_MODEL)),
            _resident((1, D_MODEL)),
        ],
        out_specs=tile(prev),
        scratch_shapes=[pltpu.VMEM((tm, D_FF), BF16), pltpu.VMEM((tm, D_MODEL), F32)],
        compiler_params=pltpu.CompilerParams(dimension_semantics=("arbitrary",), vmem_limit_bytes=VMEM_LIMIT),
        name="ffn",
    )(x, mod, wi_sets, wo_sets, g.reshape(1, D_MODEL), b.reshape(1, D_MODEL))


def _rope(t, cos_ref, sa_ref, sb_ref):
    half = AXIS_DIM // 2
    outs = []
    for c in range(t.shape[1] // LANES):
        tc = t[:, c * LANES:(c + 1) * LANES]
        up = pltpu.roll(tc, LANES - half, 1)
        down = pltpu.roll(tc, half, 1)
        outs.append(tc * cos_ref[...] + up * sa_ref[...] + down * sb_ref[...])
    return jnp.concatenate(outs, axis=1)


def _dup_heads(t):
    pieces = []
    for h in range(t.shape[1] // HEAD_DIM):
        head = t[:, h * HEAD_DIM:(h + 1) * HEAD_DIM]
        pieces += [head, head]
    return jnp.concatenate(pieces, axis=1)


def _qkv_kernel(x_ref, mod_ref, w_ref, b_ref, cos_ref, sa_ref, sb_ref, q_ref, k_ref, vt_ref):
    h = _modulate(x_ref[...], mod_ref, 1).astype(BF16)
    qkv = jnp.dot(h, w_ref[...], preferred_element_type=F32) + b_ref[...]
    q = qkv[:, :Q_WIDTH] * (HEAD_DIM ** -0.5 * LOG2E)
    q_ref[...] = _rope(q, cos_ref, sa_ref, sb_ref).astype(BF16)
    k = _rope(qkv[:, Q_WIDTH:Q_WIDTH + KV_WIDTH], cos_ref, sa_ref, sb_ref)
    k_ref[...] = _dup_heads(k).astype(BF16)
    v = _dup_heads(qkv[:, Q_WIDTH + KV_WIDTH:])
    for blk in range(vt_ref.shape[0]):
        vt_ref[blk] = v[blk * BLOCK:(blk + 1) * BLOCK, :].T.astype(BF16)


def _qkv_call(x, mod, w, b, cos_t, sa_t, sb_t, tm):
    bsz, length, _ = x.shape
    tok = lambda width: pl.BlockSpec((None, tm, width), lambda bi, t: (bi, t, 0))
    table = pl.BlockSpec((tm, LANES), lambda bi, t: (t, 0))
    return pl.pallas_call(
        _qkv_kernel,
        out_shape=(jax.ShapeDtypeStruct((bsz, length, Q_WIDTH), BF16),
                   jax.ShapeDtypeStruct((bsz, length, KV_DUP_WIDTH), BF16),
                   jax.ShapeDtypeStruct((bsz, length // BLOCK, KV_DUP_WIDTH, BLOCK), BF16)),
        grid=(bsz, length // tm),
        in_specs=[
            tok(D_MODEL),
            pl.BlockSpec((None, N_MOD, D_MODEL), lambda bi, t: (bi, 0, 0)),
            _resident((D_MODEL, QKV_WIDTH)),
            _resident((1, QKV_WIDTH)),
            table, table, table,
        ],
        out_specs=(tok(Q_WIDTH), tok(KV_DUP_WIDTH),
                   pl.BlockSpec((None, tm // BLOCK, KV_DUP_WIDTH, BLOCK), lambda bi, t: (bi, t, 0, 0))),
        compiler_params=_params(2),
        name="qkv",
    )(x, mod, w, b.reshape(1, QKV_WIDTH), cos_t, sa_t, sb_t)


def _kv_ctx_kernel(x_ref, mod_ref, w_ref, b_ref, k_ref, vt_ref):
    h = _modulate(x_ref[...], mod_ref, 1).astype(BF16)
    kv = jnp.dot(h, w_ref[...], preferred_element_type=F32) + b_ref[...]
    k_ref[...] = _dup_heads(kv[:, :KV_WIDTH]).astype(BF16)
    vt_ref[...] = _dup_heads(kv[:, KV_WIDTH:]).T.astype(BF16)


def _kv_ctx_call(x, mod, w, b):
    bsz, n_ctx, _ = x.shape
    return pl.pallas_call(
        _kv_ctx_kernel,
        out_shape=(jax.ShapeDtypeStruct((bsz, n_ctx, KV_DUP_WIDTH), BF16),
                   jax.ShapeDtypeStruct((bsz, KV_DUP_WIDTH, n_ctx), BF16)),
        grid=(bsz,),
        in_specs=[
            pl.BlockSpec((None, n_ctx, D_MODEL), lambda bi: (bi, 0, 0)),
            _resident((None, N_MOD, D_MODEL)),
            _resident((D_MODEL, 2 * KV_WIDTH)),
            _resident((1, 2 * KV_WIDTH)),
        ],
        out_specs=(pl.BlockSpec((None, n_ctx, KV_DUP_WIDTH), lambda bi: (bi, 0, 0)),
                   pl.BlockSpec((None, KV_DUP_WIDTH, n_ctx), lambda bi: (bi, 0, 0))),
        compiler_params=_params(1),
        name="kv_ctx",
    )(x, mod, w, b.reshape(1, 2 * KV_WIDTH))


def _attn_kernel(n_blocks, n_steps, sink_ref, q_ref, k_ref, vt_ref, kc_ref, vct_ref, xp_ref, modp_ref, wo_ref,
                 g_ref, b_ref, wi32_ref, wo32_ref, out_ref, wi16_ref, wo16_ref, o_ref):
    i = pl.program_id(0)

    @pl.when(i == 0)
    def _():
        o_ref[...] = jnp.zeros_like(o_ref)

    def project(o_prev):
        return jnp.dot(o_prev, wo_ref[...], preferred_element_type=F32)

    def norm(y):
        out_ref[...] = _post_norm(xp_ref[...], y, modp_ref, 1, g_ref, b_ref)

    def round_weights():
        wi16_ref[...] = wi32_ref[...].astype(BF16)
        wo16_ref[...] = wo32_ref[...].astype(BF16)

    @pl.when(i < n_steps)
    def _():
        round_weights()
        o_prev = o_ref[...]
        y = []
        tile_in_batch = i % (n_blocks * BLOCK // q_ref.shape[0])
        _attend(n_blocks, tile_in_batch, sink_ref, q_ref, k_ref, vt_ref, kc_ref, vct_ref, o_ref,
                after_slot={9: lambda: y.append(project(o_prev)), 21: lambda: norm(y[0])})

    @pl.when(i == n_steps)
    def _():
        round_weights()
        norm(project(o_ref[...]))


def _attend(n_blocks, tile_in_batch, sink_ref, q_ref, k_ref, vt_ref, kc_ref, vct_ref, o_ref, after_slot):
    n_sub = q_ref.shape[0] // BLOCK
    qb0 = tile_in_batch * n_sub
    cols2 = 2 * BLOCK
    key_i = lax.broadcasted_iota(jnp.int32, (BLOCK, cols2), 0)
    qry_i = lax.broadcasted_iota(jnp.int32, (BLOCK, cols2), 1) & (BLOCK - 1)
    first_head = lax.broadcasted_iota(jnp.int32, (1, cols2), 1) < BLOCK
    lo_q = lax.broadcasted_iota(jnp.int32, (cols2, LANES), 1) < HEAD_DIM
    lo_v = lax.broadcasted_iota(jnp.int32, (LANES, 1), 0) < HEAD_DIM

    def block_ids(sub):
        return [jnp.clip(qb0 + sub + d, 0, n_blocks - 1) for d in (-1, 0, 1)]

    slots_per_block = 2 * N_KV_HEADS
    p_t, inv = {}, {}

    def scores(slot):
        sub, rest = divmod(slot, slots_per_block)
        h, half = divmod(rest, 2)
        c0 = 2 * h * LANES
        q_rows = slice(sub * BLOCK, (sub + 1) * BLOCK)
        q_cat = jnp.concatenate([q_ref[q_rows, c0:c0 + LANES], q_ref[q_rows, c0 + LANES:c0 + 2 * LANES]], axis=0)
        keep = lo_q if half == 0 else jnp.logical_not(lo_q)
        q_sel = jnp.where(keep, q_cat, jnp.zeros_like(q_cat))
        lanes = slice(h * LANES, (h + 1) * LANES)
        k_h = jnp.concatenate([k_ref[pl.ds(pl.multiple_of(blk * BLOCK, BLOCK), BLOCK), lanes]
                               for blk in block_ids(sub)] + [kc_ref[:, lanes]], axis=0)
        return lax.dot_general(k_h, q_sel, (((1,), (1,)), ((), ())), preferred_element_type=F32)

    def softmax(slot, s_t):
        sub, rest = divmod(slot, slots_per_block)
        h, half = divmod(rest, 2)
        left_ok = (qry_i <= key_i) & (qb0 + sub > 0)
        right_ok = (key_i <= qry_i) & (qb0 + sub < n_blocks - 1)
        sink = jnp.where(first_head, sink_ref[4 * h + half], sink_ref[4 * h + 2 + half]) * LOG2E
        parts = [jnp.where(left_ok, s_t[:BLOCK], NEG_INF),
                 s_t[BLOCK:2 * BLOCK],
                 jnp.where(right_ok, s_t[2 * BLOCK:SPAN], NEG_INF)]
        parts += [s_t[c:c + BLOCK] for c in range(SPAN, s_t.shape[0], BLOCK)]
        m = jnp.max(functools.reduce(jnp.maximum, parts), axis=0, keepdims=True)
        m = jnp.maximum(m, sink)
        p = [jnp.exp2(t - m) for t in parts]
        denom = jnp.sum(functools.reduce(jnp.add, p), axis=0, keepdims=True) + jnp.exp2(sink - m)
        inv[slot] = 1.0 / denom
        p_t[slot] = jnp.concatenate(p, axis=0).astype(BF16)

    def weighted_values(pair):
        sub, h = divmod(pair, N_KV_HEADS)
        rows = slice(h * LANES, (h + 1) * LANES)
        v_t = jnp.concatenate([vt_ref[blk, rows, :] for blk in block_ids(sub)] + [vct_ref[rows, :]], axis=1)
        v_lo = jnp.where(lo_v, v_t, jnp.zeros_like(v_t))
        v_hi = jnp.where(lo_v, jnp.zeros_like(v_t), v_t)
        acc_t = (jnp.dot(v_lo, p_t.pop(2 * pair), preferred_element_type=F32)
                 + jnp.dot(v_hi, p_t.pop(2 * pair + 1), preferred_element_type=F32))
        out = (acc_t * jnp.where(lo_v, inv.pop(2 * pair), inv.pop(2 * pair + 1))).T.astype(BF16)
        c0 = 2 * h * LANES
        q_rows = slice(sub * BLOCK, (sub + 1) * BLOCK)
        o_ref[q_rows, c0:c0 + LANES] = out[:BLOCK]
        o_ref[q_rows, c0 + LANES:c0 + 2 * LANES] = out[BLOCK:]

    n_slots = n_sub * slots_per_block
    ahead = 1
    s_t = {slot: scores(slot) for slot in range(ahead)}
    for slot in range(n_slots):
        if slot + ahead < n_slots:
            s_t[slot + ahead] = scores(slot + ahead)
        softmax(slot, s_t.pop(slot))
        if slot % 2 == 0 and slot > 0:
            weighted_values(slot // 2 - 1)
        if slot in after_slot:
            after_slot[slot]()
    weighted_values(n_slots // 2 - 1)


def _attn_call(q, k, vt, kc, vct, sink, x, mod, wo, g, b, wi32, wo32, tq):
    bsz, length, _ = q.shape
    n_blocks = length // BLOCK
    per_batch_tiles = length // tq
    n_steps = bsz * per_batch_tiles
    n_later = wi32.shape[0] - 1
    wi_cols, wo_rows = 2 * D_FF // WEIGHT_SLABS, D_FF // WEIGHT_SLABS
    assert n_later * WEIGHT_SLABS <= n_steps + 1
    slab = lambda i: jnp.minimum(i, n_later * WEIGHT_SLABS - 1)
    cur = lambda i: jnp.minimum(i, n_steps - 1)
    prev = lambda i: jnp.maximum(i - 1, 0)
    tile = lambda pick: pl.BlockSpec(
        (None, tq, D_MODEL), lambda i, *_: (pick(i) // per_batch_tiles, pick(i) % per_batch_tiles, 0))
    per_batch = lambda arr: pl.BlockSpec(
        (None,) + arr.shape[1:], lambda i, *_: (cur(i) // per_batch_tiles,) + (0,) * (arr.ndim - 1))
    return pl.pallas_call(
        functools.partial(_attn_kernel, n_blocks, n_steps),
        out_shape=(jax.ShapeDtypeStruct(x.shape, F32),
                   jax.ShapeDtypeStruct((n_later,) + wi32.shape[1:], BF16),
                   jax.ShapeDtypeStruct((n_later,) + wo32.shape[1:], BF16)),
        grid_spec=pltpu.PrefetchScalarGridSpec(
            num_scalar_prefetch=1,
            grid=(n_steps + 1,),
            in_specs=[
                tile(cur),
                per_batch(k), per_batch(vt), per_batch(kc), per_batch(vct),
                tile(prev),
                pl.BlockSpec((None, N_MOD, D_MODEL), lambda i, *_: (prev(i) // per_batch_tiles, 0, 0)),
                _resident((D_MODEL, D_MODEL)),
                _resident((1, D_MODEL)),
                _resident((1, D_MODEL)),
                pl.BlockSpec((None, D_MODEL, wi_cols),
                             lambda i, *_: (1 + slab(i) // WEIGHT_SLABS, 0, slab(i) % WEIGHT_SLABS)),
                pl.BlockSpec((None, wo_rows, D_MODEL),
                             lambda i, *_: (1 + slab(i) // WEIGHT_SLABS, slab(i) % WEIGHT_SLABS, 0)),
            ],
            out_specs=(tile(prev),
                       pl.BlockSpec((None, D_MODEL, wi_cols),
                                    lambda i, *_: (slab(i) // WEIGHT_SLABS, 0, slab(i) % WEIGHT_SLABS)),
                       pl.BlockSpec((None, wo_rows, D_MODEL),
                                    lambda i, *_: (slab(i) // WEIGHT_SLABS, slab(i) % WEIGHT_SLABS, 0))),
            scratch_shapes=[pltpu.VMEM((tq, Q_WIDTH), BF16)],
        ),
        compiler_params=pltpu.CompilerParams(dimension_semantics=("arbitrary",), vmem_limit_bytes=VMEM_LIMIT),
        name="attn",
    )(sink, q, k, vt, kc, vct, x, mod, wo, g.reshape(1, D_MODEL), b.reshape(1, D_MODEL), wi32, wo32)


def _reverse_shift(t, first_row, flip_ref):
    rows = t.shape[0]
    blocks = [jnp.dot(flip_ref[...], t[r0:r0 + FLIP, :], preferred_element_type=F32)
              for r0 in range(rows - FLIP, -FLIP, -FLIP)]
    shifted = pltpu.roll(jnp.concatenate(blocks, axis=0), 1, 0)
    is_first = lax.broadcasted_iota(jnp.int32, shifted.shape, 0) == 0
    return jnp.where(is_first, first_row, shifted)


def _channel_dft(t, w_ref):
    return jnp.concatenate(
        [jnp.dot(t[:, g0:g0 + FOURIER_GROUP_CH], w_ref[...], preferred_element_type=F32)
         for g0 in range(0, D_MODEL, FOURIER_GROUP_CH)], axis=1)


def _dft_fold_kernel(lo_ref, up_ref, mod_ref, cc_ref, sc_ref, flip_ref, ab_ref, an_ref, carry_ref):
    @pl.when(pl.program_id(1) == 0)
    def _():
        carry_ref[...] = jnp.zeros_like(carry_ref)

    h_lo = _modulate(lo_ref[...], mod_ref, 1)
    h_up = _modulate(up_ref[...], mod_ref, 1).astype(BF16)
    partner = _reverse_shift(h_up, carry_ref[0:1, :], flip_ref)
    carry_ref[...] = h_up[0:8, :].astype(F32)
    ab_ref[0] = _channel_dft((h_lo + partner).astype(BF16), cc_ref).astype(BF16)
    ab_ref[1] = _channel_dft((h_lo - partner).astype(BF16), sc_ref).astype(BF16)
    an_ref[...] = _channel_dft(h_up[0:8, :], cc_ref)


def _dft_fold_call(x4, mod, cc, sc, flip, tm):
    bsz, _, half, _ = x4.shape
    n_t = half // tm
    ch = FOURIER_GROUP_CH
    return pl.pallas_call(
        _dft_fold_kernel,
        out_shape=(jax.ShapeDtypeStruct((bsz, 2, half, D_MODEL), BF16),
                   jax.ShapeDtypeStruct((bsz, 8, D_MODEL), F32)),
        grid=(bsz, n_t),
        in_specs=[
            pl.BlockSpec((None, None, tm, D_MODEL), lambda bi, t: (bi, 0, t, 0)),
            pl.BlockSpec((None, None, tm, D_MODEL), lambda bi, t: (bi, 1, n_t - 1 - t, 0)),
            pl.BlockSpec((None, N_MOD, D_MODEL), lambda bi, t: (bi, 0, 0)),
            _resident((ch, ch)),
            _resident((ch, ch)),
            _resident((FLIP, FLIP)),
        ],
        out_specs=(pl.BlockSpec((None, 2, tm, D_MODEL), lambda bi, t: (bi, 0, t, 0)),
                   pl.BlockSpec((None, 8, D_MODEL), lambda bi, t: (bi, 0, 0))),
        scratch_shapes=[pltpu.VMEM((8, D_MODEL), F32)],
        compiler_params=pltpu.CompilerParams(dimension_semantics=("parallel", "arbitrary"),
                                             vmem_limit_bytes=VMEM_LIMIT),
        name="dft_fold",
    )(x4, x4, mod, cc, sc, flip)


def _dft_seq_kernel(norm, n_t, wlo_ref, wup_ref, ab_ref, an_ref, alt_ref, flip_ref, x_ref, mod_ref, wo_ref,
                    g_ref, beta_ref, out_ref, carry_ref):
    t = pl.program_id(1)
    tk = wlo_ref.shape[0]
    ab = ab_ref[...]
    a_n = an_ref[0:1, :]
    row = lax.broadcasted_iota(jnp.int32, (tk, 1), 0)

    def alternating(k0):
        return (1 - 2 * ((k0 + row) & 1)).astype(F32)

    f_lo = (jnp.dot(wlo_ref[...], ab, preferred_element_type=F32) + alternating(t * tk) * a_n) * norm
    f_mir = (jnp.dot(wup_ref[...], ab, preferred_element_type=F32) + alternating((n_t - 1 - t) * tk) * a_n) * norm
    f_mir = f_mir.astype(BF16)

    @pl.when(t == 0)
    def _():
        f_n = (jnp.dot(alt_ref[...], ab, preferred_element_type=F32)[0:1, :] + a_n) * norm
        carry_ref[0:1, :] = f_n.astype(BF16).astype(F32)

    f_up = _reverse_shift(f_mir, carry_ref[0:1, :], flip_ref)
    carry_ref[...] = f_mir[0:8, :].astype(F32)
    y_lo = jnp.dot(f_lo.astype(BF16), wo_ref[...], preferred_element_type=F32)
    y_up = jnp.dot(f_up.astype(BF16), wo_ref[...], preferred_element_type=F32)
    out_ref[0] = _post_norm(x_ref[0], y_lo, mod_ref, 1, g_ref, beta_ref)
    out_ref[1] = _post_norm(x_ref[1], y_up, mod_ref, 1, g_ref, beta_ref)


def _dft_seq_call(w_lo, w_up, ab, a_n, alt, flip, x4, mod, wo, g, beta, tk):
    bsz, _, half, _ = x4.shape
    n_t = half // tk
    norm = float(1.0 / np.sqrt(2 * half * FOURIER_GROUP_CH))
    tok = pl.BlockSpec((None, 2, tk, D_MODEL), lambda bi, t: (bi, 0, t, 0))
    return pl.pallas_call(
        functools.partial(_dft_seq_kernel, norm, n_t),
        out_shape=jax.ShapeDtypeStruct(x4.shape, F32),
        grid=(bsz, n_t),
        in_specs=[
            pl.BlockSpec((tk, 2 * half), lambda bi, t: (t, 0)),
            pl.BlockSpec((tk, 2 * half), lambda bi, t: (n_t - 1 - t, 0)),
            pl.BlockSpec((None, 2 * half, D_MODEL), lambda bi, t: (bi, 0, 0)),
            pl.BlockSpec((None, 8, D_MODEL), lambda bi, t: (bi, 0, 0)),
            _resident((8, 2 * half)),
            _resident((FLIP, FLIP)),
            tok,
            pl.BlockSpec((None, N_MOD, D_MODEL), lambda bi, t: (bi, 0, 0)),
            _resident((D_MODEL, D_MODEL)),
            _resident((1, D_MODEL)),
            _resident((1, D_MODEL)),
        ],
        out_specs=tok,
        scratch_shapes=[pltpu.VMEM((8, D_MODEL), F32)],
        compiler_params=pltpu.CompilerParams(dimension_semantics=("parallel", "arbitrary"),
                                             vmem_limit_bytes=VMEM_LIMIT),
        name="dft_seq",
    )(w_lo, w_up, ab, a_n, alt, flip, x4, mod, wo, g.reshape(1, D_MODEL), beta.reshape(1, D_MODEL))


def _rope_tables(n_tokens):
    rows = n_tokens // GRID_W
    row = jnp.repeat(jnp.arange(rows), GRID_W).astype(F32)
    col = jnp.tile(jnp.arange(GRID_W), rows).astype(F32)
    inv = ROPE_BASE ** (-jnp.arange(0, AXIS_DIM, 2, dtype=F32) / AXIS_DIM)
    ang_r, ang_c = row[:, None] * inv, col[:, None] * inv
    cos_r, sin_r, cos_c, sin_c = jnp.cos(ang_r), jnp.sin(ang_r), jnp.cos(ang_c), jnp.sin(ang_c)
    zero = jnp.zeros_like(sin_r)
    cos_h = jnp.concatenate([cos_r, cos_r, cos_c, cos_c], axis=-1)
    sa_h = jnp.concatenate([-sin_r, zero, -sin_c, zero], axis=-1)
    sb_h = jnp.concatenate([zero, sin_r, zero, sin_c], axis=-1)
    rep = LANES // HEAD_DIM
    return tuple(jnp.tile(t, (1, rep)) for t in (cos_h, sa_h, sb_h))


def _dft_tables(n, rows):
    step = TWIDDLE_STEP if rows % TWIDDLE_STEP == 0 else rows
    col = jnp.arange(rows, dtype=jnp.int32)[None, :]

    def thin(k):
        ang = ((k[:, None] * col) % n).astype(F32) * (2.0 * np.pi / n)
        return jnp.cos(ang), jnp.sin(ang)

    cos1, sin1 = (t[:, None, :] for t in thin(jnp.arange(0, rows, step, dtype=jnp.int32)))
    cos0, sin0 = (t[None, :, :] for t in thin(jnp.arange(step, dtype=jnp.int32)))
    return ((cos1 * cos0 - sin1 * sin0).reshape(rows, rows),
            (sin1 * cos0 + cos1 * sin0).reshape(rows, rows))


def kernel(x, c, ctx, c_ctx, mod_w, mod_b, ln_g, ln_b, ffn_wi, ffn_wo,
           attn_wqkv, attn_bqkv, attn_wo, attn_sink, fourier_wo):
    bsz, seq, _ = x.shape
    n_ctx = ctx.shape[1]
    assert DEPTH == 2 and seq % BLOCK == 0 and seq % GRID_W == 0

    cv = jnp.concatenate([c, c_ctx[None, :], jnp.zeros((MOD_ROWS - bsz - 1, D_MODEL), F32)], axis=0)
    mod = _mod_call(cv, mod_w, mod_b).reshape(DEPTH, MOD_ROWS, N_MOD, D_MODEL)
    wi32 = ffn_wi.reshape(2 * DEPTH, D_MODEL, 2 * D_FF)
    wo32 = ffn_wo.reshape(2 * DEPTH, D_FF, D_MODEL)
    wi0, wo0 = wi32[:1].astype(BF16), wo32[:1].astype(BF16)

    mod_lat, mod_ctx = mod[0, :bsz], mod[0, bsz:bsz + 1]
    g, b = ln_g[0], ln_b[0]
    x = _ffn_call(x, mod_lat, 0, wi0, wo0, 0, g[0], b[0], tm=FFN_TILE)
    ctx_s = _ffn_call(ctx.reshape(1, bsz * n_ctx, D_MODEL), mod_ctx, 0, wi0, wo0, 0, g[0], b[0], tm=FFN_TILE)

    w_qkv = attn_wqkv[0].astype(BF16)
    cos_t, sa_t, sb_t = _rope_tables(seq)
    q, k, vt = _qkv_call(x, mod_lat, w_qkv, attn_bqkv[0], cos_t, sa_t, sb_t, tm=1024)
    kc, vct = _kv_ctx_call(ctx_s.reshape(bsz, n_ctx, D_MODEL), mod_ctx, w_qkv[:, Q_WIDTH:], attn_bqkv[0, Q_WIDTH:])
    x, wi, wo = _attn_call(q, k, vt, kc, vct, attn_sink[0], x, mod_lat, attn_wo[0].astype(BF16), g[1], b[1],
                           wi32, wo32, tq=4 * BLOCK)
    x = _ffn_call(x, mod_lat, 2, wi, wo, 0, g[2], b[2], tm=FFN_TILE)

    mod_lat = mod[1, :bsz]
    g, b = ln_g[1], ln_b[1]
    x = _ffn_call(x, mod_lat, 0, wi, wo, 1, g[0], b[0], tm=FFN_TILE)
    half = seq // 2
    cos_c, sin_c = _dft_tables(FOURIER_GROUP_CH, FOURIER_GROUP_CH)
    cos_s, sin_s = _dft_tables(seq, half)
    w_lo = jnp.concatenate([cos_s, -sin_s], axis=1).astype(BF16)
    w_up = jnp.concatenate([cos_s, sin_s], axis=1).astype(BF16)
    alt = jnp.zeros((8, seq), F32).at[0, :half].set(1.0 - 2.0 * (jnp.arange(half) % 2)).astype(BF16)
    anti = jnp.arange(FLIP)[:, None] + jnp.arange(FLIP)[None, :] == FLIP - 1
    flip = anti.astype(BF16)
    x4 = x.reshape(bsz, 2, half, D_MODEL)
    ab, a_n = _dft_fold_call(x4, mod_lat, cos_c.astype(BF16), sin_c.astype(BF16), flip, tm=1024)
    x4 = _dft_seq_call(w_lo, w_up, ab.reshape(bsz, seq, D_MODEL), a_n, alt, flip, x4, mod_lat,
                       fourier_wo[0].astype(BF16), g[1], b[1], tk=512)
    x = x4.reshape(bsz, seq, D_MODEL)
    x = _ffn_call(x, mod_lat, 2, wi, wo, 2, g[2], b[2], tm=FFN_TILE)
    return x
```

```python
import functools

import jax
import jax.numpy as jnp
import numpy as np
from jax import lax
from jax.experimental import pallas as pl
from jax.experimental.pallas import tpu as pltpu

D_MODEL = 1024
DEPTH = 2
GRID_W = 64
N_HEADS = 16
N_KV_HEADS = 4
HEAD_DIM = 64
GROUP = N_HEADS // N_KV_HEADS
Q_WIDTH = N_HEADS * HEAD_DIM
KV_WIDTH = N_KV_HEADS * HEAD_DIM
QKV_WIDTH = Q_WIDTH + 2 * KV_WIDTH
KV_DUP_WIDTH = 2 * KV_WIDTH
WINDOW = 128
BLOCK = 128
SPAN = BLOCK + 2 * WINDOW
ROPE_BASE = 10000.0
AXIS_DIM = HEAD_DIM // 2
FOURIER_GROUPS = 4
FOURIER_GROUP_CH = D_MODEL // FOURIER_GROUPS
D_FF = 2816
N_MOD = 9
LN_EPS = 1e-5
ALPHA = (2.0 * DEPTH) ** 0.25
NEG_INF = -1e30
LOG2E = float(np.log2(np.e))

LANES = 128
MOD_ROWS = 24
VMEM_LIMIT = 60 * 1024 * 1024
FFN_TILE = 1024
FFN_CHUNKS = (1024, 1024, 768)
FLIP = 256
TWIDDLE_STEP = 32
WEIGHT_SLABS = 11

BF16 = jnp.bfloat16
F32 = jnp.float32


def _params(n_axes):
    return pltpu.CompilerParams(dimension_semantics=("parallel",) * n_axes,
                                vmem_limit_bytes=VMEM_LIMIT)


def _resident(shape):
    return pl.BlockSpec(shape, lambda *_: (0,) * len(shape), pipeline_mode=pl.Buffered(1))


def _layer_norm(z, g, b):
    mu = jnp.mean(z, axis=-1, keepdims=True)
    d = z - mu
    var = jnp.mean(d * d, axis=-1, keepdims=True)
    return d * lax.rsqrt(var + LN_EPS) * g + b


def _modulate(x, mod_ref, j):
    shift = mod_ref[3 * j:3 * j + 1, :]
    scale = mod_ref[3 * j + 1:3 * j + 2, :]
    return x * (1.0 + scale) + shift


def _post_norm(x, y, mod_ref, j, g_ref, b_ref):
    gate = mod_ref[3 * j + 2:3 * j + 3, :]
    return _layer_norm(ALPHA * x + gate * y, g_ref[...], b_ref[...])


def _mod_kernel(cv_ref, w_ref, b_ref, o_ref):
    cv = cv_ref[...]
    s = (cv * jax.nn.sigmoid(cv)).astype(BF16)
    o_ref[...] = jnp.dot(s, w_ref[...].astype(BF16), preferred_element_type=F32) + b_ref[...]


def _mod_call(cv, mod_w, mod_b):
    tn = 3 * D_MODEL
    n_out = N_MOD * D_MODEL
    return pl.pallas_call(
        _mod_kernel,
        out_shape=jax.ShapeDtypeStruct((DEPTH, MOD_ROWS, n_out), F32),
        grid=(DEPTH, n_out // tn),
        in_specs=[
            pl.BlockSpec((MOD_ROWS, D_MODEL), lambda i, n: (0, 0)),
            pl.BlockSpec((None, D_MODEL, tn), lambda i, n: (i, 0, n)),
            pl.BlockSpec((None, 1, tn), lambda i, n: (i, 0, n)),
        ],
        out_specs=pl.BlockSpec((None, MOD_ROWS, tn), lambda i, n: (i, 0, n)),
        compiler_params=_params(2),
        name="mod",
    )(cv, mod_w, mod_b.reshape(DEPTH, 1, n_out))


def _ffn_kernel(j, n_tiles, x_ref, mod_ref, wi_ref, wo_ref, g_ref, b_ref, o_ref, u_ref, z_ref):
    i = pl.program_id(0)

    @pl.when(i == 0)
    def _():
        z_ref[...] = jnp.zeros_like(z_ref)

    def norm_previous():
        o_ref[...] = _layer_norm(z_ref[...], g_ref[...], b_ref[...])

    @pl.when(i < n_tiles)
    def _():
        norm_previous()
        h = _modulate(x_ref[...], mod_ref, j).astype(BF16)
        start = 0
        for width in FFN_CHUNKS:
            a = jnp.dot(h, wi_ref[:, start:start + width], preferred_element_type=F32)
            g = jnp.dot(h, wi_ref[:, D_FF + start:D_FF + start + width], preferred_element_type=F32)
            u_ref[:, start:start + width] = (a * (g * jax.nn.sigmoid(g))).astype(BF16)
            start += width
        y = 0.5 * jnp.dot(u_ref[...], wo_ref[...], preferred_element_type=F32)
        z_ref[...] = ALPHA * x_ref[...] + mod_ref[3 * j + 2:3 * j + 3, :] * y

    @pl.when(i == n_tiles)
    def _():
        norm_previous()


def _ffn_call(x, mod, j, wi_sets, wo_sets, which, g, b, tm):
    bsz, length, _ = x.shape
    per_batch = length // tm
    n_tiles = bsz * per_batch
    cur = lambda i: jnp.minimum(i, n_tiles - 1)
    prev = lambda i: jnp.maximum(i - 1, 0)
    tile = lambda pick: pl.BlockSpec((None, tm, D_MODEL), lambda i: (pick(i) // per_batch, pick(i) % per_batch, 0))
    weight = lambda rows, cols: pl.BlockSpec((None, rows, cols), lambda i: (which, 0, 0),
                                             pipeline_mode=pl.Buffered(1))
    return pl.pallas_call(
        functools.partial(_ffn_kernel, j, n_tiles),
        out_shape=jax.ShapeDtypeStruct(x.shape, F32),
        grid=(n_tiles + 1,),
        in_specs=[
            tile(cur),
            pl.BlockSpec((None, N_MOD, D_MODEL), lambda i: (cur(i) // per_batch, 0, 0)),
            weight(D_MODEL, 2 * D_FF),
            weight(D_FF, D_MODEL),
            _resident((1, D_MODEL)),
            _resident((1, D_MODEL)),
        ],
        out_specs=tile(prev),
        scratch_shapes=[pltpu.VMEM((tm, D_FF), BF16), pltpu.VMEM((tm, D_MODEL), F32)],
        compiler_params=pltpu.CompilerParams(dimension_semantics=("arbitrary",), vmem_limit_bytes=VMEM_LIMIT),
        name="ffn",
    )(x, mod, wi_sets, wo_sets, g.reshape(1, D_MODEL), b.reshape(1, D_MODEL))


def _rope(t, cos_ref, sa_ref, sb_ref):
    half = AXIS_DIM // 2
    outs = []
    for c in range(t.shape[1] // LANES):
        tc = t[:, c * LANES:(c + 1) * LANES]
        up = pltpu.roll(tc, LANES - half, 1)
        down = pltpu.roll(tc, half, 1)
        outs.append(tc * cos_ref[...] + up * sa_ref[...] + down * sb_ref[...])
    return jnp.concatenate(outs, axis=1)


def _dup_heads(t):
    pieces = []
    for h in range(t.shape[1] // HEAD_DIM):
        head = t[:, h * HEAD_DIM:(h + 1) * HEAD_DIM]
        pieces += [head, head]
    return jnp.concatenate(pieces, axis=1)


def _qkv_kernel(x_ref, mod_ref, w_ref, b_ref, cos_ref, sa_ref, sb_ref, q_ref, k_ref, vt_ref):
    h = _modulate(x_ref[...], mod_ref, 1).astype(BF16)
    qkv = jnp.dot(h, w_ref[...], preferred_element_type=F32) + b_ref[...]
    q = qkv[:, :Q_WIDTH] * (HEAD_DIM ** -0.5 * LOG2E)
    q_ref[...] = _rope(q, cos_ref, sa_ref, sb_ref).astype(BF16)
    k = _rope(qkv[:, Q_WIDTH:Q_WIDTH + KV_WIDTH], cos_ref, sa_ref, sb_ref)
    k_ref[...] = _dup_heads(k).astype(BF16)
    v = _dup_heads(qkv[:, Q_WIDTH + KV_WIDTH:])
    for blk in range(vt_ref.shape[0]):
        vt_ref[blk] = v[blk * BLOCK:(blk + 1) * BLOCK, :].T.astype(BF16)


def _qkv_call(x, mod, w, b, cos_t, sa_t, sb_t, tm):
    bsz, length, _ = x.shape
    tok = lambda width: pl.BlockSpec((None, tm, width), lambda bi, t: (bi, t, 0))
    table = pl.BlockSpec((tm, LANES), lambda bi, t: (t, 0))
    return pl.pallas_call(
        _qkv_kernel,
        out_shape=(jax.ShapeDtypeStruct((bsz, length, Q_WIDTH), BF16),
                   jax.ShapeDtypeStruct((bsz, length, KV_DUP_WIDTH), BF16),
                   jax.ShapeDtypeStruct((bsz, length // BLOCK, KV_DUP_WIDTH, BLOCK), BF16)),
        grid=(bsz, length // tm),
        in_specs=[
            tok(D_MODEL),
            pl.BlockSpec((None, N_MOD, D_MODEL), lambda bi, t: (bi, 0, 0)),
            _resident((D_MODEL, QKV_WIDTH)),
            _resident((1, QKV_WIDTH)),
            table, table, table,
        ],
        out_specs=(tok(Q_WIDTH), tok(KV_DUP_WIDTH),
                   pl.BlockSpec((None, tm // BLOCK, KV_DUP_WIDTH, BLOCK), lambda bi, t: (bi, t, 0, 0))),
        compiler_params=_params(2),
        name="qkv",
    )(x, mod, w, b.reshape(1, QKV_WIDTH), cos_t, sa_t, sb_t)


def _kv_ctx_kernel(x_ref, mod_ref, w_ref, b_ref, k_ref, vt_ref):
    h = _modulate(x_ref[...], mod_ref, 1).astype(BF16)
    kv = jnp.dot(h, w_ref[...], preferred_element_type=F32) + b_ref[...]
    k_ref[...] = _dup_heads(kv[:, :KV_WIDTH]).astype(BF16)
    vt_ref[...] = _dup_heads(kv[:, KV_WIDTH:]).T.astype(BF16)


def _kv_ctx_call(x, mod, w, b):
    bsz, n_ctx, _ = x.shape
    return pl.pallas_call(
        _kv_ctx_kernel,
        out_shape=(jax.ShapeDtypeStruct((bsz, n_ctx, KV_DUP_WIDTH), BF16),
                   jax.ShapeDtypeStruct((bsz, KV_DUP_WIDTH, n_ctx), BF16)),
        grid=(bsz,),
        in_specs=[
            pl.BlockSpec((None, n_ctx, D_MODEL), lambda bi: (bi, 0, 0)),
            _resident((None, N_MOD, D_MODEL)),
            _resident((D_MODEL, 2 * KV_WIDTH)),
            _resident((1, 2 * KV_WIDTH)),
        ],
        out_specs=(pl.BlockSpec((None, n_ctx, KV_DUP_WIDTH), lambda bi: (bi, 0, 0)),
                   pl.BlockSpec((None, KV_DUP_WIDTH, n_ctx), lambda bi: (bi, 0, 0))),
        compiler_params=_params(1),
        name="kv_ctx",
    )(x, mod, w, b.reshape(1, 2 * KV_WIDTH))


def _attn_kernel(n_blocks, n_steps, sink_ref, q_ref, k_ref, vt_ref, kc_ref, vct_ref, xp_ref, modp_ref, wo_ref,
                 g_ref, b_ref, wi32_ref, wo32_ref, out_ref, wi16_ref, wo16_ref, o_ref):
    i = pl.program_id(0)

    @pl.when(i == 0)
    def _():
        o_ref[...] = jnp.zeros_like(o_ref)

    def project(o_prev):
        return jnp.dot(o_prev, wo_ref[...], preferred_element_type=F32)

    def norm(y):
        out_ref[...] = _post_norm(xp_ref[...], y, modp_ref, 1, g_ref, b_ref)

    def round_weights():
        wi16_ref[...] = wi32_ref[...].astype(BF16)
        wo16_ref[...] = wo32_ref[...].astype(BF16)

    @pl.when(i < n_steps)
    def _():
        round_weights()
        o_prev = o_ref[...]
        y = []
        tile_in_batch = i % (n_blocks * BLOCK // q_ref.shape[0])
        _attend(n_blocks, tile_in_batch, sink_ref, q_ref, k_ref, vt_ref, kc_ref, vct_ref, o_ref,
                after_slot={9: lambda: y.append(project(o_prev)), 21: lambda: norm(y[0])})

    @pl.when(i == n_steps)
    def _():
        round_weights()
        norm(project(o_ref[...]))


def _attend(n_blocks, tile_in_batch, sink_ref, q_ref, k_ref, vt_ref, kc_ref, vct_ref, o_ref, after_slot):
    n_sub = q_ref.shape[0] // BLOCK
    qb0 = tile_in_batch * n_sub
    cols2 = 2 * BLOCK
    key_i = lax.broadcasted_iota(jnp.int32, (BLOCK, cols2), 0)
    qry_i = lax.broadcasted_iota(jnp.int32, (BLOCK, cols2), 1) & (BLOCK - 1)
    first_head = lax.broadcasted_iota(jnp.int32, (1, cols2), 1) < BLOCK
    lo_q = lax.broadcasted_iota(jnp.int32, (cols2, LANES), 1) < HEAD_DIM
    lo_v = lax.broadcasted_iota(jnp.int32, (LANES, 1), 0) < HEAD_DIM

    def block_ids(sub):
        return [jnp.clip(qb0 + sub + d, 0, n_blocks - 1) for d in (-1, 0, 1)]

    slots_per_block = 2 * N_KV_HEADS
    p_t, inv = {}, {}

    def scores(slot):
        sub, rest = divmod(slot, slots_per_block)
        h, half = divmod(rest, 2)
        c0 = 2 * h * LANES
        q_rows = slice(sub * BLOCK, (sub + 1) * BLOCK)
        q_cat = jnp.concatenate([q_ref[q_rows, c0:c0 + LANES], q_ref[q_rows, c0 + LANES:c0 + 2 * LANES]], axis=0)
        keep = lo_q if half == 0 else jnp.logical_not(lo_q)
        q_sel = jnp.where(keep, q_cat, jnp.zeros_like(q_cat))
        lanes = slice(h * LANES, (h + 1) * LANES)
        k_h = jnp.concatenate([k_ref[pl.ds(pl.multiple_of(blk * BLOCK, BLOCK), BLOCK), lanes]
                               for blk in block_ids(sub)] + [kc_ref[:, lanes]], axis=0)
        return lax.dot_general(k_h, q_sel, (((1,), (1,)), ((), ())), preferred_element_type=F32)

    def softmax(slot, s_t):
        sub, rest = divmod(slot, slots_per_block)
        h, half = divmod(rest, 2)
        left_ok = (qry_i <= key_i) & (qb0 + sub > 0)
        right_ok = (key_i <= qry_i) & (qb0 + sub < n_blocks - 1)
        sink = jnp.where(first_head, sink_ref[4 * h + half], sink_ref[4 * h + 2 + half]) * LOG2E
        parts = [jnp.where(left_ok, s_t[:BLOCK], NEG_INF),
                 s_t[BLOCK:2 * BLOCK],
                 jnp.where(right_ok, s_t[2 * BLOCK:SPAN], NEG_INF)]
        parts += [s_t[c:c + BLOCK] for c in range(SPAN, s_t.shape[0], BLOCK)]
        m = jnp.max(functools.reduce(jnp.maximum, parts), axis=0, keepdims=True)
        m = jnp.maximum(m, sink)
        p = [jnp.exp2(t - m) for t in parts]
        denom = jnp.sum(functools.reduce(jnp.add, p), axis=0, keepdims=True) + jnp.exp2(sink - m)
        inv[slot] = 1.0 / denom
        p_t[slot] = jnp.concatenate(p, axis=0).astype(BF16)

    def weighted_values(pair):
        sub, h = divmod(pair, N_KV_HEADS)
        rows = slice(h * LANES, (h + 1) * LANES)
        v_t = jnp.concatenate([vt_ref[blk, rows, :] for blk in block_ids(sub)] + [vct_ref[rows, :]], axis=1)
        v_lo = jnp.where(lo_v, v_t, jnp.zeros_like(v_t))
        v_hi = jnp.where(lo_v, jnp.zeros_like(v_t), v_t)
        acc_t = (jnp.dot(v_lo, p_t.pop(2 * pair), preferred_element_type=F32)
                 + jnp.dot(v_hi, p_t.pop(2 * pair + 1), preferred_element_type=F32))
        out = (acc_t * jnp.where(lo_v, inv.pop(2 * pair), inv.pop(2 * pair + 1))).T.astype(BF16)
        c0 = 2 * h * LANES
        q_rows = slice(sub * BLOCK, (sub + 1) * BLOCK)
        o_ref[q_rows, c0:c0 + LANES] = out[:BLOCK]
        o_ref[q_rows, c0 + LANES:c0 + 2 * LANES] = out[BLOCK:]

    n_slots = n_sub * slots_per_block
    ahead = 1
    s_t = {slot: scores(slot) for slot in range(ahead)}
    for slot in range(n_slots):
        if slot + ahead < n_slots:
            s_t[slot + ahead] = scores(slot + ahead)
        softmax(slot, s_t.pop(slot))
        if slot % 2 == 0 and slot > 0:
            weighted_values(slot // 2 - 1)
        if slot in after_slot:
            after_slot[slot]()
    weighted_values(n_slots // 2 - 1)


def _attn_call(q, k, vt, kc, vct, sink, x, mod, wo, g, b, wi32, wo32, tq):
    bsz, length, _ = q.shape
    n_blocks = length // BLOCK
    per_batch_tiles = length // tq
    n_steps = bsz * per_batch_tiles
    n_later = wi32.shape[0] - 1
    wi_cols, wo_rows = 2 * D_FF // WEIGHT_SLABS, D_FF // WEIGHT_SLABS
    assert n_later * WEIGHT_SLABS <= n_steps + 1
    slab = lambda i: jnp.minimum(i, n_later * WEIGHT_SLABS - 1)
    cur = lambda i: jnp.minimum(i, n_steps - 1)
    prev = lambda i: jnp.maximum(i - 1, 0)
    tile = lambda pick: pl.BlockSpec(
        (None, tq, D_MODEL), lambda i, *_: (pick(i) // per_batch_tiles, pick(i) % per_batch_tiles, 0))
    per_batch = lambda arr: pl.BlockSpec(
        (None,) + arr.shape[1:], lambda i, *_: (cur(i) // per_batch_tiles,) + (0,) * (arr.ndim - 1))
    return pl.pallas_call(
        functools.partial(_attn_kernel, n_blocks, n_steps),
        out_shape=(jax.ShapeDtypeStruct(x.shape, F32),
                   jax.ShapeDtypeStruct((n_later,) + wi32.shape[1:], BF16),
                   jax.ShapeDtypeStruct((n_later,) + wo32.shape[1:], BF16)),
        grid_spec=pltpu.PrefetchScalarGridSpec(
            num_scalar_prefetch=1,
            grid=(n_steps + 1,),
            in_specs=[
                tile(cur),
                per_batch(k), per_batch(vt), per_batch(kc), per_batch(vct),
                tile(prev),
                pl.BlockSpec((None, N_MOD, D_MODEL), lambda i, *_: (prev(i) // per_batch_tiles, 0, 0)),
                _resident((D_MODEL, D_MODEL)),
                _resident((1, D_MODEL)),
                _resident((1, D_MODEL)),
                pl.BlockSpec((None, D_MODEL, wi_cols),
                             lambda i, *_: (1 + slab(i) // WEIGHT_SLABS, 0, slab(i) % WEIGHT_SLABS)),
                pl.BlockSpec((None, wo_rows, D_MODEL),
                             lambda i, *_: (1 + slab(i) // WEIGHT_SLABS, slab(i) % WEIGHT_SLABS, 0)),
            ],
            out_specs=(tile(prev),
                       pl.BlockSpec((None, D_MODEL, wi_cols),
                                    lambda i, *_: (slab(i) // WEIGHT_SLABS, 0, slab(i) % WEIGHT_SLABS)),
                       pl.BlockSpec((None, wo_rows, D_MODEL),
                                    lambda i, *_: (slab(i) // WEIGHT_SLABS, slab(i) % WEIGHT_SLABS, 0))),
            scratch_shapes=[pltpu.VMEM((tq, Q_WIDTH), BF16)],
        ),
        compiler_params=pltpu.CompilerParams(dimension_semantics=("arbitrary",), vmem_limit_bytes=VMEM_LIMIT),
        name="attn",
    )(sink, q, k, vt, kc, vct, x, mod, wo, g.reshape(1, D_MODEL), b.reshape(1, D_MODEL), wi32, wo32)


def _reverse_shift(t, first_row, flip_ref):
    rows = t.shape[0]
    blocks = [jnp.dot(flip_ref[...], t[r0:r0 + FLIP, :], preferred_element_type=F32)
              for r0 in range(rows - FLIP, -FLIP, -FLIP)]
    shifted = pltpu.roll(jnp.concatenate(blocks, axis=0), 1, 0)
    is_first = lax.broadcasted_iota(jnp.int32, shifted.shape, 0) == 0
    return jnp.where(is_first, first_row, shifted)


def _channel_dft(t, w_ref):
    return jnp.concatenate(
        [jnp.dot(t[:, g0:g0 + FOURIER_GROUP_CH], w_ref[...], preferred_element_type=F32)
         for g0 in range(0, D_MODEL, FOURIER_GROUP_CH)], axis=1)


def _dft_fold_kernel(lo_ref, up_ref, mod_ref, cc_ref, sc_ref, flip_ref, ab_ref, an_ref, carry_ref):
    @pl.when(pl.program_id(1) == 0)
    def _():
        carry_ref[...] = jnp.zeros_like(carry_ref)

    h_lo = _modulate(lo_ref[...], mod_ref, 1)
    h_up = _modulate(up_ref[...], mod_ref, 1).astype(BF16)
    partner = _reverse_shift(h_up, carry_ref[0:1, :], flip_ref)
    carry_ref[...] = h_up[0:8, :].astype(F32)
    ab_ref[0] = _channel_dft((h_lo + partner).astype(BF16), cc_ref).astype(BF16)
    ab_ref[1] = _channel_dft((h_lo - partner).astype(BF16), sc_ref).astype(BF16)
    an_ref[...] = _channel_dft(h_up[0:8, :], cc_ref)


def _dft_fold_call(x4, mod, cc, sc, flip, tm):
    bsz, _, half, _ = x4.shape
    n_t = half // tm
    ch = FOURIER_GROUP_CH
    return pl.pallas_call(
        _dft_fold_kernel,
        out_shape=(jax.ShapeDtypeStruct((bsz, 2, half, D_MODEL), BF16),
                   jax.ShapeDtypeStruct((bsz, 8, D_MODEL), F32)),
        grid=(bsz, n_t),
        in_specs=[
            pl.BlockSpec((None, None, tm, D_MODEL), lambda bi, t: (bi, 0, t, 0)),
            pl.BlockSpec((None, None, tm, D_MODEL), lambda bi, t: (bi, 1, n_t - 1 - t, 0)),
            pl.BlockSpec((None, N_MOD, D_MODEL), lambda bi, t: (bi, 0, 0)),
            _resident((ch, ch)),
            _resident((ch, ch)),
            _resident((FLIP, FLIP)),
        ],
        out_specs=(pl.BlockSpec((None, 2, tm, D_MODEL), lambda bi, t: (bi, 0, t, 0)),
                   pl.BlockSpec((None, 8, D_MODEL), lambda bi, t: (bi, 0, 0))),
        scratch_shapes=[pltpu.VMEM((8, D_MODEL), F32)],
        compiler_params=pltpu.CompilerParams(dimension_semantics=("parallel", "arbitrary"),
                                             vmem_limit_bytes=VMEM_LIMIT),
        name="dft_fold",
    )(x4, x4, mod, cc, sc, flip)


def _dft_seq_kernel(norm, n_t, wlo_ref, wup_ref, ab_ref, an_ref, alt_ref, flip_ref, x_ref, mod_ref, wo_ref,
                    g_ref, beta_ref, out_ref, carry_ref):
    t = pl.program_id(1)
    tk = wlo_ref.shape[0]
    ab = ab_ref[...]
    a_n = an_ref[0:1, :]
    row = lax.broadcasted_iota(jnp.int32, (tk, 1), 0)

    def alternating(k0):
        return (1 - 2 * ((k0 + row) & 1)).astype(F32)

    f_lo = (jnp.dot(wlo_ref[...], ab, preferred_element_type=F32) + alternating(t * tk) * a_n) * norm
    f_mir = (jnp.dot(wup_ref[...], ab, preferred_element_type=F32) + alternating((n_t - 1 - t) * tk) * a_n) * norm
    f_mir = f_mir.astype(BF16)

    @pl.when(t == 0)
    def _():
        f_n = (jnp.dot(alt_ref[...], ab, preferred_element_type=F32)[0:1, :] + a_n) * norm
        carry_ref[0:1, :] = f_n.astype(BF16).astype(F32)

    f_up = _reverse_shift(f_mir, carry_ref[0:1, :], flip_ref)
    carry_ref[...] = f_mir[0:8, :].astype(F32)
    y_lo = jnp.dot(f_lo.astype(BF16), wo_ref[...], preferred_element_type=F32)
    y_up = jnp.dot(f_up.astype(BF16), wo_ref[...], preferred_element_type=F32)
    out_ref[0] = _post_norm(x_ref[0], y_lo, mod_ref, 1, g_ref, beta_ref)
    out_ref[1] = _post_norm(x_ref[1], y_up, mod_ref, 1, g_ref, beta_ref)


def _dft_seq_call(w_lo, w_up, ab, a_n, alt, flip, x4, mod, wo, g, beta, tk):
    bsz, _, half, _ = x4.shape
    n_t = half // tk
    norm = float(1.0 / np.sqrt(2 * half * FOURIER_GROUP_CH))
    tok = pl.BlockSpec((None, 2, tk, D_MODEL), lambda bi, t: (bi, 0, t, 0))
    return pl.pallas_call(
        functools.partial(_dft_seq_kernel, norm, n_t),
        out_shape=jax.ShapeDtypeStruct(x4.shape, F32),
        grid=(bsz, n_t),
        in_specs=[
            pl.BlockSpec((tk, 2 * half), lambda bi, t: (t, 0)),
            pl.BlockSpec((tk, 2 * half), lambda bi, t: (n_t - 1 - t, 0)),
            pl.BlockSpec((None, 2 * half, D_MODEL), lambda bi, t: (bi, 0, 0)),
            pl.BlockSpec((None, 8, D_MODEL), lambda bi, t: (bi, 0, 0)),
            _resident((8, 2 * half)),
            _resident((FLIP, FLIP)),
            tok,
            pl.BlockSpec((None, N_MOD, D_MODEL), lambda bi, t: (bi, 0, 0)),
            _resident((D_MODEL, D_MODEL)),
            _resident((1, D_MODEL)),
            _resident((1, D_MODEL)),
        ],
        out_specs=tok,
        scratch_shapes=[pltpu.VMEM((8, D_MODEL), F32)],
        compiler_params=pltpu.CompilerParams(dimension_semantics=("parallel", "arbitrary"),
                                             vmem_limit_bytes=VMEM_LIMIT),
        name="dft_seq",
    )(w_lo, w_up, ab, a_n, alt, flip, x4, mod, wo, g.reshape(1, D_MODEL), beta.reshape(1, D_MODEL))


def _rope_tables(n_tokens):
    rows = n_tokens // GRID_W
    row = jnp.repeat(jnp.arange(rows), GRID_W).astype(F32)
    col = jnp.tile(jnp.arange(GRID_W), rows).astype(F32)
    inv = ROPE_BASE ** (-jnp.arange(0, AXIS_DIM, 2, dtype=F32) / AXIS_DIM)
    ang_r, ang_c = row[:, None] * inv, col[:, None] * inv
    cos_r, sin_r, cos_c, sin_c = jnp.cos(ang_r), jnp.sin(ang_r), jnp.cos(ang_c), jnp.sin(ang_c)
    zero = jnp.zeros_like(sin_r)
    cos_h = jnp.concatenate([cos_r, cos_r, cos_c, cos_c], axis=-1)
    sa_h = jnp.concatenate([-sin_r, zero, -sin_c, zero], axis=-1)
    sb_h = jnp.concatenate([zero, sin_r, zero, sin_c], axis=-1)
    rep = LANES // HEAD_DIM
    return tuple(jnp.tile(t, (1, rep)) for t in (cos_h, sa_h, sb_h))


def _dft_tables(n, rows):
    step = TWIDDLE_STEP if rows % TWIDDLE_STEP == 0 else rows
    col = jnp.arange(rows, dtype=jnp.int32)[None, :]

    def thin(k):
        ang = ((k[:, None] * col) % n).astype(F32) * (2.0 * np.pi / n)
        return jnp.cos(ang), jnp.sin(ang)

    cos1, sin1 = (t[:, None, :] for t in thin(jnp.arange(0, rows, step, dtype=jnp.int32)))
    cos0, sin0 = (t[None, :, :] for t in thin(jnp.arange(step, dtype=jnp.int32)))
    return ((cos1 * cos0 - sin1 * sin0).reshape(rows, rows),
            (sin1 * cos0 + cos1 * sin0).reshape(rows, rows))


def kernel(x, c, ctx, c_ctx, mod_w, mod_b, ln_g, ln_b, ffn_wi, ffn_wo,
           attn_wqkv, attn_bqkv, attn_wo, attn_sink, fourier_wo):
    bsz, seq, _ = x.shape
    n_ctx = ctx.shape[1]
    assert DEPTH == 2 and seq % BLOCK == 0 and seq % GRID_W == 0

    cv = jnp.concatenate([c, c_ctx[None, :], jnp.zeros((MOD_ROWS - bsz - 1, D_MODEL), F32)], axis=0)
    mod = _mod_call(cv, mod_w, mod_b).reshape(DEPTH, MOD_ROWS, N_MOD, D_MODEL)
    wi32 = ffn_wi.reshape(2 * DEPTH, D_MODEL, 2 * D_FF)
    wo32 = ffn_wo.reshape(2 * DEPTH, D_FF, D_MODEL)
    wi0, wo0 = wi32[:1].astype(BF16), wo32[:1].astype(BF16)

    mod_lat, mod_ctx = mod[0, :bsz], mod[0, bsz:bsz + 1]
    g, b = ln_g[0], ln_b[0]
    x = _ffn_call(x, mod_lat, 0, wi0, wo0, 0, g[0], b[0], tm=FFN_TILE)
    ctx_s = _ffn_call(ctx.reshape(1, bsz * n_ctx, D_MODEL), mod_ctx, 0, wi0, wo0, 0, g[0], b[0], tm=FFN_TILE)

    w_qkv = attn_wqkv[0].astype(BF16)
    cos_t, sa_t, sb_t = _rope_tables(seq)
    q, k, vt = _qkv_call(x, mod_lat, w_qkv, attn_bqkv[0], cos_t, sa_t, sb_t, tm=1024)
    kc, vct = _kv_ctx_call(ctx_s.reshape(bsz, n_ctx, D_MODEL), mod_ctx, w_qkv[:, Q_WIDTH:], attn_bqkv[0, Q_WIDTH:])
    x, wi, wo = _attn_call(q, k, vt, kc, vct, attn_sink[0], x, mod_lat, attn_wo[0].astype(BF16), g[1], b[1],
                           wi32, wo32, tq=4 * BLOCK)
    x = _ffn_call(x, mod_lat, 2, wi, wo, 0, g[2], b[2], tm=FFN_TILE)

    mod_lat = mod[1, :bsz]
    g, b = ln_g[1], ln_b[1]
    x = _ffn_call(x, mod_lat, 0, wi, wo, 1, g[0], b[0], tm=FFN_TILE)
    half = seq // 2
    cos_c, sin_c = _dft_tables(FOURIER_GROUP_CH, FOURIER_GROUP_CH)
    cos_s, sin_s = _dft_tables(seq, half)
    w_lo = jnp.concatenate([cos_s, -sin_s], axis=1).astype(BF16)
    w_up = jnp.concatenate([cos_s, sin_s], axis=1).astype(BF16)
    alt = jnp.zeros((8, seq), F32).at[0, :half].set(1.0 - 2.0 * (jnp.arange(half) % 2)).astype(BF16)
    anti = jnp.arange(FLIP)[:, None] + jnp.arange(FLIP)[None, :] == FLIP - 1
    flip = anti.astype(BF16)
    x4 = x.reshape(bsz, 2, half, D_MODEL)
    ab, a_n = _dft_fold_call(x4, mod_lat, cos_c.astype(BF16), sin_c.astype(BF16), flip, tm=1024)
    x4 = _dft_seq_call(w_lo, w_up, ab.reshape(bsz, seq, D_MODEL), a_n, alt, flip, x4, mod_lat,
                       fourier_wo[0].astype(BF16), g[1], b[1], tk=512)
    x = x4.reshape(bsz, seq, D_MODEL)
    x = _ffn_call(x, mod_lat, 2, wi, wo, 2, g[2], b[2], tm=FFN_TILE)
    return x
```

```python
import functools

import jax
import jax.numpy as jnp
import numpy as np
from jax import lax
from jax.experimental import pallas as pl
from jax.experimental.pallas import tpu as pltpu

D_MODEL = 1024
DEPTH = 2
GRID_W = 64
N_HEADS = 16
N_KV_HEADS = 4
HEAD_DIM = 64
GROUP = N_HEADS // N_KV_HEADS
Q_WIDTH = N_HEADS * HEAD_DIM
KV_WIDTH = N_KV_HEADS * HEAD_DIM
QKV_WIDTH = Q_WIDTH + 2 * KV_WIDTH
KV_DUP_WIDTH = 2 * KV_WIDTH
WINDOW = 128
BLOCK = 128
SPAN = BLOCK + 2 * WINDOW
ROPE_BASE = 10000.0
AXIS_DIM = HEAD_DIM // 2
FOURIER_GROUPS = 4
FOURIER_GROUP_CH = D_MODEL // FOURIER_GROUPS
D_FF = 2816
N_MOD = 9
LN_EPS = 1e-5
ALPHA = (2.0 * DEPTH) ** 0.25
NEG_INF = -1e30
LOG2E = float(np.log2(np.e))

LANES = 128
MOD_ROWS = 24
VMEM_LIMIT = 60 * 1024 * 1024
FFN_TILE = 1024
FFN_CHUNKS = (1024, 1024, 768)
FLIP = 256
TWIDDLE_STEP = 32
WEIGHT_SLABS = 11

BF16 = jnp.bfloat16
F32 = jnp.float32


def _params(n_axes):
    return pltpu.CompilerParams(dimension_semantics=("parallel",) * n_axes,
                                vmem_limit_bytes=VMEM_LIMIT)


def _resident(shape):
    return pl.BlockSpec(shape, lambda *_: (0,) * len(shape), pipeline_mode=pl.Buffered(1))


def _layer_norm(z, g, b):
    mu = jnp.mean(z, axis=-1, keepdims=True)
    d = z - mu
    var = jnp.mean(d * d, axis=-1, keepdims=True)
    return d * lax.rsqrt(var + LN_EPS) * g + b


def _modulate(x, mod_ref, j):
    shift = mod_ref[3 * j:3 * j + 1, :]
    scale = mod_ref[3 * j + 1:3 * j + 2, :]
    return x * (1.0 + scale) + shift


def _post_norm(x, y, mod_ref, j, g_ref, b_ref):
    gate = mod_ref[3 * j + 2:3 * j + 3, :]
    return _layer_norm(ALPHA * x + gate * y, g_ref[...], b_ref[...])


def _mod_kernel(cv_ref, w_ref, b_ref, o_ref):
    cv = cv_ref[...]
    s = (cv * jax.nn.sigmoid(cv)).astype(BF16)
    o_ref[...] = jnp.dot(s, w_ref[...].astype(BF16), preferred_element_type=F32) + b_ref[...]


def _mod_call(cv, mod_w, mod_b):
    tn = 3 * D_MODEL
    n_out = N_MOD * D_MODEL
    return pl.pallas_call(
        _mod_kernel,
        out_shape=jax.ShapeDtypeStruct((DEPTH, MOD_ROWS, n_out), F32),
        grid=(DEPTH, n_out // tn),
        in_specs=[
            pl.BlockSpec((MOD_ROWS, D_MODEL), lambda i, n: (0, 0)),
            pl.BlockSpec((None, D_MODEL, tn), lambda i, n: (i, 0, n)),
            pl.BlockSpec((None, 1, tn), lambda i, n: (i, 0, n)),
        ],
        out_specs=pl.BlockSpec((None, MOD_ROWS, tn), lambda i, n: (i, 0, n)),
        compiler_params=_params(2),
        name="mod",
    )(cv, mod_w, mod_b.reshape(DEPTH, 1, n_out))


def _ffn_kernel(j, n_tiles, x_ref, mod_ref, wi_ref, wo_ref, g_ref, b_ref, o_ref, u_ref, z_ref):
    i = pl.program_id(0)

    @pl.when(i == 0)
    def _():
        z_ref[...] = jnp.zeros_like(z_ref)

    def norm_previous():
        o_ref[...] = _layer_norm(z_ref[...], g_ref[...], b_ref[...])

    @pl.when(i < n_tiles)
    def _():
        norm_previous()
        h = _modulate(x_ref[...], mod_ref, j).astype(BF16)
        start = 0
        for width in FFN_CHUNKS:
            g = jnp.dot(h, wi_ref[:, D_FF + start:D_FF + start + width], preferred_element_type=F32)
            gate = g * jax.nn.sigmoid(g)
            a = jnp.dot(h, wi_ref[:, start:start + width], preferred_element_type=F32)
            u_ref[:, start:start + width] = (a * gate).astype(BF16)
            start += width
        y = 0.5 * jnp.dot(u_ref[...], wo_ref[...], preferred_element_type=F32)
        z_ref[...] = ALPHA * x_ref[...] + mod_ref[3 * j + 2:3 * j + 3, :] * y

    @pl.when(i == n_tiles)
    def _():
        norm_previous()


def _ffn_call(x, mod, j, wi_sets, wo_sets, which, g, b, tm):
    bsz, length, _ = x.shape
    per_batch = length // tm
    n_tiles = bsz * per_batch
    cur = lambda i: jnp.minimum(i, n_tiles - 1)
    prev = lambda i: jnp.maximum(i - 1, 0)
    tile = lambda pick: pl.BlockSpec((None, tm, D_MODEL), lambda i: (pick(i) // per_batch, pick(i) % per_batch, 0))
    weight = lambda rows, cols: pl.BlockSpec((None, rows, cols), lambda i: (which, 0, 0),
                                             pipeline_mode=pl.Buffered(1))
    return pl.pallas_call(
        functools.partial(_ffn_kernel, j, n_tiles),
        out_shape=jax.ShapeDtypeStruct(x.shape, F32),
        grid=(n_tiles + 1,),
        in_specs=[
            tile(cur),
            pl.BlockSpec((None, N_MOD, D_MODEL), lambda i: (cur(i) // per_batch, 0, 0)),
            weight(D_MODEL, 2 * D_FF),
            weight(D_FF, D_MODEL),
            _resident((1, D_MODEL)),
            _resident((1, D_MODEL)),
        ],
        out_specs=tile(prev),
        scratch_shapes=[pltpu.VMEM((tm, D_FF), BF16), pltpu.VMEM((tm, D_MODEL), F32)],
        compiler_params=pltpu.CompilerParams(dimension_semantics=("arbitrary",), vmem_limit_bytes=VMEM_LIMIT),
        name="ffn",
    )(x, mod, wi_sets, wo_sets, g.reshape(1, D_MODEL), b.reshape(1, D_MODEL))


def _rope(t, cos_ref, sa_ref, sb_ref):
    half = AXIS_DIM // 2
    outs = []
    for c in range(t.shape[1] // LANES):
        tc = t[:, c * LANES:(c + 1) * LANES]
        up = pltpu.roll(tc, LANES - half, 1)
        down = pltpu.roll(tc, half, 1)
        outs.append(tc * cos_ref[...] + up * sa_ref[...] + down * sb_ref[...])
    return jnp.concatenate(outs, axis=1)


def _dup_heads(t):
    pieces = []
    for h in range(t.shape[1] // HEAD_DIM):
        head = t[:, h * HEAD_DIM:(h + 1) * HEAD_DIM]
        pieces += [head, head]
    return jnp.concatenate(pieces, axis=1)


def _qkv_kernel(x_ref, mod_ref, w_ref, b_ref, cos_ref, sa_ref, sb_ref, q_ref, k_ref, vt_ref):
    h = _modulate(x_ref[...], mod_ref, 1).astype(BF16)
    qkv = jnp.dot(h, w_ref[...], preferred_element_type=F32) + b_ref[...]
    q = qkv[:, :Q_WIDTH] * (HEAD_DIM ** -0.5 * LOG2E)
    q_ref[...] = _rope(q, cos_ref, sa_ref, sb_ref).astype(BF16)
    k = _rope(qkv[:, Q_WIDTH:Q_WIDTH + KV_WIDTH], cos_ref, sa_ref, sb_ref)
    k_ref[...] = _dup_heads(k).astype(BF16)
    v = _dup_heads(qkv[:, Q_WIDTH + KV_WIDTH:])
    for blk in range(vt_ref.shape[0]):
        vt_ref[blk] = v[blk * BLOCK:(blk + 1) * BLOCK, :].T.astype(BF16)


def _qkv_call(x, mod, w, b, cos_t, sa_t, sb_t, tm):
    bsz, length, _ = x.shape
    tok = lambda width: pl.BlockSpec((None, tm, width), lambda bi, t: (bi, t, 0))
    table = pl.BlockSpec((tm, LANES), lambda bi, t: (t, 0))
    return pl.pallas_call(
        _qkv_kernel,
        out_shape=(jax.ShapeDtypeStruct((bsz, length, Q_WIDTH), BF16),
                   jax.ShapeDtypeStruct((bsz, length, KV_DUP_WIDTH), BF16),
                   jax.ShapeDtypeStruct((bsz, length // BLOCK, KV_DUP_WIDTH, BLOCK), BF16)),
        grid=(bsz, length // tm),
        in_specs=[
            tok(D_MODEL),
            pl.BlockSpec((None, N_MOD, D_MODEL), lambda bi, t: (bi, 0, 0)),
            _resident((D_MODEL, QKV_WIDTH)),
            _resident((1, QKV_WIDTH)),
            table, table, table,
        ],
        out_specs=(tok(Q_WIDTH), tok(KV_DUP_WIDTH),
                   pl.BlockSpec((None, tm // BLOCK, KV_DUP_WIDTH, BLOCK), lambda bi, t: (bi, t, 0, 0))),
        compiler_params=_params(2),
        name="qkv",
    )(x, mod, w, b.reshape(1, QKV_WIDTH), cos_t, sa_t, sb_t)


def _kv_ctx_kernel(x_ref, mod_ref, w_ref, b_ref, k_ref, vt_ref):
    h = _modulate(x_ref[...], mod_ref, 1).astype(BF16)
    kv = jnp.dot(h, w_ref[...], preferred_element_type=F32) + b_ref[...]
    k_ref[...] = _dup_heads(kv[:, :KV_WIDTH]).astype(BF16)
    vt_ref[...] = _dup_heads(kv[:, KV_WIDTH:]).T.astype(BF16)


def _kv_ctx_call(x, mod, w, b):
    bsz, n_ctx, _ = x.shape
    return pl.pallas_call(
        _kv_ctx_kernel,
        out_shape=(jax.ShapeDtypeStruct((bsz, n_ctx, KV_DUP_WIDTH), BF16),
                   jax.ShapeDtypeStruct((bsz, KV_DUP_WIDTH, n_ctx), BF16)),
        grid=(bsz,),
        in_specs=[
            pl.BlockSpec((None, n_ctx, D_MODEL), lambda bi: (bi, 0, 0)),
            _resident((None, N_MOD, D_MODEL)),
            _resident((D_MODEL, 2 * KV_WIDTH)),
            _resident((1, 2 * KV_WIDTH)),
        ],
        out_specs=(pl.BlockSpec((None, n_ctx, KV_DUP_WIDTH), lambda bi: (bi, 0, 0)),
                   pl.BlockSpec((None, KV_DUP_WIDTH, n_ctx), lambda bi: (bi, 0, 0))),
        compiler_params=_params(1),
        name="kv_ctx",
    )(x, mod, w, b.reshape(1, 2 * KV_WIDTH))


def _attn_kernel(n_blocks, n_steps, sink_ref, q_ref, k_ref, vt_ref, kc_ref, vct_ref, xp_ref, modp_ref, wo_ref,
                 g_ref, b_ref, wi32_ref, wo32_ref, out_ref, wi16_ref, wo16_ref, o_ref):
    i = pl.program_id(0)

    @pl.when(i == 0)
    def _():
        o_ref[...] = jnp.zeros_like(o_ref)

    def project(o_prev):
        return jnp.dot(o_prev, wo_ref[...], preferred_element_type=F32)

    def norm(y):
        out_ref[...] = _post_norm(xp_ref[...], y, modp_ref, 1, g_ref, b_ref)

    def round_weights():
        wi16_ref[...] = wi32_ref[...].astype(BF16)
        wo16_ref[...] = wo32_ref[...].astype(BF16)

    @pl.when(i < n_steps)
    def _():
        round_weights()
        o_prev = o_ref[...]
        y = []
        tile_in_batch = i % (n_blocks * BLOCK // q_ref.shape[0])
        _attend(n_blocks, tile_in_batch, sink_ref, q_ref, k_ref, vt_ref, kc_ref, vct_ref, o_ref,
                after_slot={9: lambda: y.append(project(o_prev)), 21: lambda: norm(y[0])})

    @pl.when(i == n_steps)
    def _():
        round_weights()
        norm(project(o_ref[...]))


def _attend(n_blocks, tile_in_batch, sink_ref, q_ref, k_ref, vt_ref, kc_ref, vct_ref, o_ref, after_slot):
    n_sub = q_ref.shape[0] // BLOCK
    qb0 = tile_in_batch * n_sub
    cols2 = 2 * BLOCK
    key_i = lax.broadcasted_iota(jnp.int32, (BLOCK, cols2), 0)
    qry_i = lax.broadcasted_iota(jnp.int32, (BLOCK, cols2), 1) & (BLOCK - 1)
    first_head = lax.broadcasted_iota(jnp.int32, (1, cols2), 1) < BLOCK
    lo_q = lax.broadcasted_iota(jnp.int32, (cols2, LANES), 1) < HEAD_DIM
    lo_v = lax.broadcasted_iota(jnp.int32, (LANES, 1), 0) < HEAD_DIM

    def block_ids(sub):
        return [jnp.clip(qb0 + sub + d, 0, n_blocks - 1) for d in (-1, 0, 1)]

    slots_per_block = 2 * N_KV_HEADS
    p_t, inv = {}, {}

    def scores(slot):
        sub, rest = divmod(slot, slots_per_block)
        h, half = divmod(rest, 2)
        c0 = 2 * h * LANES
        q_rows = slice(sub * BLOCK, (sub + 1) * BLOCK)
        q_cat = jnp.concatenate([q_ref[q_rows, c0:c0 + LANES], q_ref[q_rows, c0 + LANES:c0 + 2 * LANES]], axis=0)
        keep = lo_q if half == 0 else jnp.logical_not(lo_q)
        q_sel = jnp.where(keep, q_cat, jnp.zeros_like(q_cat))
        lanes = slice(h * LANES, (h + 1) * LANES)
        k_h = jnp.concatenate([k_ref[pl.ds(pl.multiple_of(blk * BLOCK, BLOCK), BLOCK), lanes]
                               for blk in block_ids(sub)] + [kc_ref[:, lanes]], axis=0)
        return lax.dot_general(k_h, q_sel, (((1,), (1,)), ((), ())), preferred_element_type=F32)

    def softmax(slot, s_t):
        sub, rest = divmod(slot, slots_per_block)
        h, half = divmod(rest, 2)
        left_ok = (qry_i <= key_i) & (qb0 + sub > 0)
        right_ok = (key_i <= qry_i) & (qb0 + sub < n_blocks - 1)
        sink = jnp.where(first_head, sink_ref[4 * h + half], sink_ref[4 * h + 2 + half]) * LOG2E
        parts = [jnp.where(left_ok, s_t[:BLOCK], NEG_INF),
                 s_t[BLOCK:2 * BLOCK],
                 jnp.where(right_ok, s_t[2 * BLOCK:SPAN], NEG_INF)]
        parts += [s_t[c:c + BLOCK] for c in range(SPAN, s_t.shape[0], BLOCK)]
        m = jnp.max(functools.reduce(jnp.maximum, parts), axis=0, keepdims=True)
        m = jnp.maximum(m, sink)
        p = [jnp.exp2(t - m) for t in parts]
        denom = jnp.sum(functools.reduce(jnp.add, p), axis=0, keepdims=True) + jnp.exp2(sink - m)
        inv[slot] = 1.0 / denom
        p_t[slot] = jnp.concatenate(p, axis=0).astype(BF16)

    def weighted_values(pair):
        sub, h = divmod(pair, N_KV_HEADS)
        rows = slice(h * LANES, (h + 1) * LANES)
        v_t = jnp.concatenate([vt_ref[blk, rows, :] for blk in block_ids(sub)] + [vct_ref[rows, :]], axis=1)
        v_lo = jnp.where(lo_v, v_t, jnp.zeros_like(v_t))
        v_hi = jnp.where(lo_v, jnp.zeros_like(v_t), v_t)
        acc_t = (jnp.dot(v_lo, p_t.pop(2 * pair), preferred_element_type=F32)
                 + jnp.dot(v_hi, p_t.pop(2 * pair + 1), preferred_element_type=F32))
        out = (acc_t * jnp.where(lo_v, inv.pop(2 * pair), inv.pop(2 * pair + 1))).T.astype(BF16)
        c0 = 2 * h * LANES
        q_rows = slice(sub * BLOCK, (sub + 1) * BLOCK)
        o_ref[q_rows, c0:c0 + LANES] = out[:BLOCK]
        o_ref[q_rows, c0 + LANES:c0 + 2 * LANES] = out[BLOCK:]

    n_slots = n_sub * slots_per_block
    ahead = 1
    s_t = {slot: scores(slot) for slot in range(ahead)}
    for slot in range(n_slots):
        if slot + ahead < n_slots:
            s_t[slot + ahead] = scores(slot + ahead)
        softmax(slot, s_t.pop(slot))
        if slot % 2 == 0 and slot > 0:
            weighted_values(slot // 2 - 1)
        if slot in after_slot:
            after_slot[slot]()
    weighted_values(n_slots // 2 - 1)


def _attn_call(q, k, vt, kc, vct, sink, x, mod, wo, g, b, wi32, wo32, tq):
    bsz, length, _ = q.shape
    n_blocks = length // BLOCK
    per_batch_tiles = length // tq
    n_steps = bsz * per_batch_tiles
    n_later = wi32.shape[0] - 1
    wi_cols, wo_rows = 2 * D_FF // WEIGHT_SLABS, D_FF // WEIGHT_SLABS
    assert n_later * WEIGHT_SLABS <= n_steps + 1
    slab = lambda i: jnp.minimum(i, n_later * WEIGHT_SLABS - 1)
    cur = lambda i: jnp.minimum(i, n_steps - 1)
    prev = lambda i: jnp.maximum(i - 1, 0)
    tile = lambda pick: pl.BlockSpec(
        (None, tq, D_MODEL), lambda i, *_: (pick(i) // per_batch_tiles, pick(i) % per_batch_tiles, 0))
    per_batch = lambda arr: pl.BlockSpec(
        (None,) + arr.shape[1:], lambda i, *_: (cur(i) // per_batch_tiles,) + (0,) * (arr.ndim - 1))
    return pl.pallas_call(
        functools.partial(_attn_kernel, n_blocks, n_steps),
        out_shape=(jax.ShapeDtypeStruct(x.shape, F32),
                   jax.ShapeDtypeStruct((n_later,) + wi32.shape[1:], BF16),
                   jax.ShapeDtypeStruct((n_later,) + wo32.shape[1:], BF16)),
        grid_spec=pltpu.PrefetchScalarGridSpec(
            num_scalar_prefetch=1,
            grid=(n_steps + 1,),
            in_specs=[
                tile(cur),
                per_batch(k), per_batch(vt), per_batch(kc), per_batch(vct),
                tile(prev),
                pl.BlockSpec((None, N_MOD, D_MODEL), lambda i, *_: (prev(i) // per_batch_tiles, 0, 0)),
                _resident((D_MODEL, D_MODEL)),
                _resident((1, D_MODEL)),
                _resident((1, D_MODEL)),
                pl.BlockSpec((None, D_MODEL, wi_cols),
                             lambda i, *_: (1 + slab(i) // WEIGHT_SLABS, 0, slab(i) % WEIGHT_SLABS)),
                pl.BlockSpec((None, wo_rows, D_MODEL),
                             lambda i, *_: (1 + slab(i) // WEIGHT_SLABS, slab(i) % WEIGHT_SLABS, 0)),
            ],
            out_specs=(tile(prev),
                       pl.BlockSpec((None, D_MODEL, wi_cols),
                                    lambda i, *_: (slab(i) // WEIGHT_SLABS, 0, slab(i) % WEIGHT_SLABS)),
                       pl.BlockSpec((None, wo_rows, D_MODEL),
                                    lambda i, *_: (slab(i) // WEIGHT_SLABS, slab(i) % WEIGHT_SLABS, 0))),
            scratch_shapes=[pltpu.VMEM((tq, Q_WIDTH), BF16)],
        ),
        compiler_params=pltpu.CompilerParams(dimension_semantics=("arbitrary",), vmem_limit_bytes=VMEM_LIMIT),
        name="attn",
    )(sink, q, k, vt, kc, vct, x, mod, wo, g.reshape(1, D_MODEL), b.reshape(1, D_MODEL), wi32, wo32)


def _reverse_shift(t, first_row, flip_ref):
    rows = t.shape[0]
    blocks = [jnp.dot(flip_ref[...], t[r0:r0 + FLIP, :], preferred_element_type=F32)
              for r0 in range(rows - FLIP, -FLIP, -FLIP)]
    shifted = pltpu.roll(jnp.concatenate(blocks, axis=0), 1, 0)
    is_first = lax.broadcasted_iota(jnp.int32, shifted.shape, 0) == 0
    return jnp.where(is_first, first_row, shifted)


def _channel_dft(t, w_ref):
    return jnp.concatenate(
        [jnp.dot(t[:, g0:g0 + FOURIER_GROUP_CH], w_ref[...], preferred_element_type=F32)
         for g0 in range(0, D_MODEL, FOURIER_GROUP_CH)], axis=1)


def _dft_fold_kernel(lo_ref, up_ref, mod_ref, cc_ref, sc_ref, flip_ref, ab_ref, an_ref, carry_ref):
    @pl.when(pl.program_id(1) == 0)
    def _():
        carry_ref[...] = jnp.zeros_like(carry_ref)

    h_lo = _modulate(lo_ref[...], mod_ref, 1)
    h_up = _modulate(up_ref[...], mod_ref, 1).astype(BF16)
    partner = _reverse_shift(h_up, carry_ref[0:1, :], flip_ref)
    carry_ref[...] = h_up[0:8, :].astype(F32)
    ab_ref[0] = _channel_dft((h_lo + partner).astype(BF16), cc_ref).astype(BF16)
    ab_ref[1] = _channel_dft((h_lo - partner).astype(BF16), sc_ref).astype(BF16)
    an_ref[...] = _channel_dft(h_up[0:8, :], cc_ref)


def _dft_fold_call(x4, mod, cc, sc, flip, tm):
    bsz, _, half, _ = x4.shape
    n_t = half // tm
    ch = FOURIER_GROUP_CH
    return pl.pallas_call(
        _dft_fold_kernel,
        out_shape=(jax.ShapeDtypeStruct((bsz, 2, half, D_MODEL), BF16),
                   jax.ShapeDtypeStruct((bsz, 8, D_MODEL), F32)),
        grid=(bsz, n_t),
        in_specs=[
            pl.BlockSpec((None, None, tm, D_MODEL), lambda bi, t: (bi, 0, t, 0)),
            pl.BlockSpec((None, None, tm, D_MODEL), lambda bi, t: (bi, 1, n_t - 1 - t, 0)),
            pl.BlockSpec((None, N_MOD, D_MODEL), lambda bi, t: (bi, 0, 0)),
            _resident((ch, ch)),
            _resident((ch, ch)),
            _resident((FLIP, FLIP)),
        ],
        out_specs=(pl.BlockSpec((None, 2, tm, D_MODEL), lambda bi, t: (bi, 0, t, 0)),
                   pl.BlockSpec((None, 8, D_MODEL), lambda bi, t: (bi, 0, 0))),
        scratch_shapes=[pltpu.VMEM((8, D_MODEL), F32)],
        compiler_params=pltpu.CompilerParams(dimension_semantics=("parallel", "arbitrary"),
                                             vmem_limit_bytes=VMEM_LIMIT),
        name="dft_fold",
    )(x4, x4, mod, cc, sc, flip)


def _dft_seq_kernel(norm, n_t, wlo_ref, wup_ref, ab_ref, an_ref, alt_ref, flip_ref, x_ref, mod_ref, wo_ref,
                    g_ref, beta_ref, out_ref, carry_ref):
    t = pl.program_id(1)
    tk = wlo_ref.shape[0]
    ab = ab_ref[...]
    a_n = an_ref[0:1, :]
    row = lax.broadcasted_iota(jnp.int32, (tk, 1), 0)

    def alternating(k0):
        return (1 - 2 * ((k0 + row) & 1)).astype(F32)

    f_lo = (jnp.dot(wlo_ref[...], ab, preferred_element_type=F32) + alternating(t * tk) * a_n) * norm
    f_mir = (jnp.dot(wup_ref[...], ab, preferred_element_type=F32) + alternating((n_t - 1 - t) * tk) * a_n) * norm
    f_mir = f_mir.astype(BF16)

    @pl.when(t == 0)
    def _():
        f_n = (jnp.dot(alt_ref[...], ab, preferred_element_type=F32)[0:1, :] + a_n) * norm
        carry_ref[0:1, :] = f_n.astype(BF16).astype(F32)

    f_up = _reverse_shift(f_mir, carry_ref[0:1, :], flip_ref)
    carry_ref[...] = f_mir[0:8, :].astype(F32)
    y_lo = jnp.dot(f_lo.astype(BF16), wo_ref[...], preferred_element_type=F32)
    y_up = jnp.dot(f_up.astype(BF16), wo_ref[...], preferred_element_type=F32)
    out_ref[0] = _post_norm(x_ref[0], y_lo, mod_ref, 1, g_ref, beta_ref)
    out_ref[1] = _post_norm(x_ref[1], y_up, mod_ref, 1, g_ref, beta_ref)


def _dft_seq_call(w_lo, w_up, ab, a_n, alt, flip, x4, mod, wo, g, beta, tk):
    bsz, _, half, _ = x4.shape
    n_t = half // tk
    norm = float(1.0 / np.sqrt(2 * half * FOURIER_GROUP_CH))
    tok = pl.BlockSpec((None, 2, tk, D_MODEL), lambda bi, t: (bi, 0, t, 0))
    return pl.pallas_call(
        functools.partial(_dft_seq_kernel, norm, n_t),
        out_shape=jax.ShapeDtypeStruct(x4.shape, F32),
        grid=(bsz, n_t),
        in_specs=[
            pl.BlockSpec((tk, 2 * half), lambda bi, t: (t, 0)),
            pl.BlockSpec((tk, 2 * half), lambda bi, t: (n_t - 1 - t, 0)),
            pl.BlockSpec((None, 2 * half, D_MODEL), lambda bi, t: (bi, 0, 0)),
            pl.BlockSpec((None, 8, D_MODEL), lambda bi, t: (bi, 0, 0)),
            _resident((8, 2 * half)),
            _resident((FLIP, FLIP)),
            tok,
            pl.BlockSpec((None, N_MOD, D_MODEL), lambda bi, t: (bi, 0, 0)),
            _resident((D_MODEL, D_MODEL)),
            _resident((1, D_MODEL)),
            _resident((1, D_MODEL)),
        ],
        out_specs=tok,
        scratch_shapes=[pltpu.VMEM((8, D_MODEL), F32)],
        compiler_params=pltpu.CompilerParams(dimension_semantics=("parallel", "arbitrary"),
                                             vmem_limit_bytes=VMEM_LIMIT),
        name="dft_seq",
    )(w_lo, w_up, ab, a_n, alt, flip, x4, mod, wo, g.reshape(1, D_MODEL), beta.reshape(1, D_MODEL))


def _rope_tables(n_tokens):
    rows = n_tokens // GRID_W
    row = jnp.repeat(jnp.arange(rows), GRID_W).astype(F32)
    col = jnp.tile(jnp.arange(GRID_W), rows).astype(F32)
    inv = ROPE_BASE ** (-jnp.arange(0, AXIS_DIM, 2, dtype=F32) / AXIS_DIM)
    ang_r, ang_c = row[:, None] * inv, col[:, None] * inv
    cos_r, sin_r, cos_c, sin_c = jnp.cos(ang_r), jnp.sin(ang_r), jnp.cos(ang_c), jnp.sin(ang_c)
    zero = jnp.zeros_like(sin_r)
    cos_h = jnp.concatenate([cos_r, cos_r, cos_c, cos_c], axis=-1)
    sa_h = jnp.concatenate([-sin_r, zero, -sin_c, zero], axis=-1)
    sb_h = jnp.concatenate([zero, sin_r, zero, sin_c], axis=-1)
    rep = LANES // HEAD_DIM
    return tuple(jnp.tile(t, (1, rep)) for t in (cos_h, sa_h, sb_h))


def _dft_tables(n, rows):
    step = TWIDDLE_STEP if rows % TWIDDLE_STEP == 0 else rows
    col = jnp.arange(rows, dtype=jnp.int32)[None, :]

    def thin(k):
        ang = ((k[:, None] * col) % n).astype(F32) * (2.0 * np.pi / n)
        return jnp.cos(ang), jnp.sin(ang)

    cos1, sin1 = (t[:, None, :] for t in thin(jnp.arange(0, rows, step, dtype=jnp.int32)))
    cos0, sin0 = (t[None, :, :] for t in thin(jnp.arange(step, dtype=jnp.int32)))
    return ((cos1 * cos0 - sin1 * sin0).reshape(rows, rows),
            (sin1 * cos0 + cos1 * sin0).reshape(rows, rows))


def kernel(x, c, ctx, c_ctx, mod_w, mod_b, ln_g, ln_b, ffn_wi, ffn_wo,
           attn_wqkv, attn_bqkv, attn_wo, attn_sink, fourier_wo):
    bsz, seq, _ = x.shape
    n_ctx = ctx.shape[1]
    assert DEPTH == 2 and seq % BLOCK == 0 and seq % GRID_W == 0

    cv = jnp.concatenate([c, c_ctx[None, :], jnp.zeros((MOD_ROWS - bsz - 1, D_MODEL), F32)], axis=0)
    mod = _mod_call(cv, mod_w, mod_b).reshape(DEPTH, MOD_ROWS, N_MOD, D_MODEL)
    wi32 = ffn_wi.reshape(2 * DEPTH, D_MODEL, 2 * D_FF)
    wo32 = ffn_wo.reshape(2 * DEPTH, D_FF, D_MODEL)
    wi0, wo0 = wi32[:1].astype(BF16), wo32[:1].astype(BF16)

    mod_lat, mod_ctx = mod[0, :bsz], mod[0, bsz:bsz + 1]
    g, b = ln_g[0], ln_b[0]
    x = _ffn_call(x, mod_lat, 0, wi0, wo0, 0, g[0], b[0], tm=FFN_TILE)
    ctx_s = _ffn_call(ctx.reshape(1, bsz * n_ctx, D_MODEL), mod_ctx, 0, wi0, wo0, 0, g[0], b[0], tm=FFN_TILE)

    w_qkv = attn_wqkv[0].astype(BF16)
    cos_t, sa_t, sb_t = _rope_tables(seq)
    q, k, vt = _qkv_call(x, mod_lat, w_qkv, attn_bqkv[0], cos_t, sa_t, sb_t, tm=1024)
    kc, vct = _kv_ctx_call(ctx_s.reshape(bsz, n_ctx, D_MODEL), mod_ctx, w_qkv[:, Q_WIDTH:], attn_bqkv[0, Q_WIDTH:])
    x, wi, wo = _attn_call(q, k, vt, kc, vct, attn_sink[0], x, mod_lat, attn_wo[0].astype(BF16), g[1], b[1],
                           wi32, wo32, tq=4 * BLOCK)
    x = _ffn_call(x, mod_lat, 2, wi, wo, 0, g[2], b[2], tm=FFN_TILE)

    mod_lat = mod[1, :bsz]
    g, b = ln_g[1], ln_b[1]
    x = _ffn_call(x, mod_lat, 0, wi, wo, 1, g[0], b[0], tm=FFN_TILE)
    half = seq // 2
    cos_c, sin_c = _dft_tables(FOURIER_GROUP_CH, FOURIER_GROUP_CH)
    cos_s, sin_s = _dft_tables(seq, half)
    w_lo = jnp.concatenate([cos_s, -sin_s], axis=1).astype(BF16)
    w_up = jnp.concatenate([cos_s, sin_s], axis=1).astype(BF16)
    alt = jnp.zeros((8, seq), F32).at[0, :half].set(1.0 - 2.0 * (jnp.arange(half) % 2)).astype(BF16)
    anti = jnp.arange(FLIP)[:, None] + jnp.arange(FLIP)[None, :] == FLIP - 1
    flip = anti.astype(BF16)
    x4 = x.reshape(bsz, 2, half, D_MODEL)
    ab, a_n = _dft_fold_call(x4, mod_lat, cos_c.astype(BF16), sin_c.astype(BF16), flip, tm=1024)
    x4 = _dft_seq_call(w_lo, w_up, ab.reshape(bsz, seq, D_MODEL), a_n, alt, flip, x4, mod_lat,
                       fourier_wo[0].astype(BF16), g[1], b[1], tk=512)
    x = x4.reshape(bsz, seq, D_MODEL)
    x = _ffn_call(x, mod_lat, 2, wi, wo, 2, g[2], b[2], tm=FFN_TILE)
    return x
```

```python
import functools

import jax
import jax.numpy as jnp
import numpy as np
from jax import lax
from jax.experimental import pallas as pl
from jax.experimental.pallas import tpu as pltpu

D_MODEL = 1024
DEPTH = 2
GRID_W = 64
N_HEADS = 16
N_KV_HEADS = 4
HEAD_DIM = 64
GROUP = N_HEADS // N_KV_HEADS
Q_WIDTH = N_HEADS * HEAD_DIM
KV_WIDTH = N_KV_HEADS * HEAD_DIM
QKV_WIDTH = Q_WIDTH + 2 * KV_WIDTH
KV_DUP_WIDTH = 2 * KV_WIDTH
WINDOW = 128
BLOCK = 128
SPAN = BLOCK + 2 * WINDOW
ROPE_BASE = 10000.0
AXIS_DIM = HEAD_DIM // 2
FOURIER_GROUPS = 4
FOURIER_GROUP_CH = D_MODEL // FOURIER_GROUPS
D_FF = 2816
N_MOD = 9
LN_EPS = 1e-5
ALPHA = (2.0 * DEPTH) ** 0.25
NEG_INF = -1e30
LOG2E = float(np.log2(np.e))

LANES = 128
MOD_ROWS = 24
VMEM_LIMIT = 60 * 1024 * 1024
FFN_TILE = 1024
FFN_CHUNKS = (512,) * 5 + (256,)
FLIP = 256
TWIDDLE_STEP = 32
WEIGHT_SLABS = 11

BF16 = jnp.bfloat16
F32 = jnp.float32


def _params(n_axes):
    return pltpu.CompilerParams(dimension_semantics=("parallel",) * n_axes,
                                vmem_limit_bytes=VMEM_LIMIT)


def _resident(shape):
    return pl.BlockSpec(shape, lambda *_: (0,) * len(shape), pipeline_mode=pl.Buffered(1))


def _layer_norm(z, g, b):
    mu = jnp.mean(z, axis=-1, keepdims=True)
    d = z - mu
    var = jnp.mean(d * d, axis=-1, keepdims=True)
    return d * lax.rsqrt(var + LN_EPS) * g + b


def _modulate(x, mod_ref, j):
    shift = mod_ref[3 * j:3 * j + 1, :]
    scale = mod_ref[3 * j + 1:3 * j + 2, :]
    return x * (1.0 + scale) + shift


def _post_norm(x, y, mod_ref, j, g_ref, b_ref):
    gate = mod_ref[3 * j + 2:3 * j + 3, :]
    return _layer_norm(ALPHA * x + gate * y, g_ref[...], b_ref[...])


def _mod_kernel(cv_ref, w_ref, b_ref, o_ref):
    cv = cv_ref[...]
    s = (cv * jax.nn.sigmoid(cv)).astype(BF16)
    o_ref[...] = jnp.dot(s, w_ref[...].astype(BF16), preferred_element_type=F32) + b_ref[...]


def _mod_call(cv, mod_w, mod_b):
    tn = 3 * D_MODEL
    n_out = N_MOD * D_MODEL
    return pl.pallas_call(
        _mod_kernel,
        out_shape=jax.ShapeDtypeStruct((DEPTH, MOD_ROWS, n_out), F32),
        grid=(DEPTH, n_out // tn),
        in_specs=[
            pl.BlockSpec((MOD_ROWS, D_MODEL), lambda i, n: (0, 0)),
            pl.BlockSpec((None, D_MODEL, tn), lambda i, n: (i, 0, n)),
            pl.BlockSpec((None, 1, tn), lambda i, n: (i, 0, n)),
        ],
        out_specs=pl.BlockSpec((None, MOD_ROWS, tn), lambda i, n: (i, 0, n)),
        compiler_params=_params(2),
        name="mod",
    )(cv, mod_w, mod_b.reshape(DEPTH, 1, n_out))


def _ffn_kernel(j, n_tiles, x_ref, mod_ref, wi_ref, wo_ref, g_ref, b_ref, o_ref, u_ref, z_ref):
    i = pl.program_id(0)

    @pl.when(i == 0)
    def _():
        z_ref[...] = jnp.zeros_like(z_ref)

    def norm_previous():
        o_ref[...] = _layer_norm(z_ref[...], g_ref[...], b_ref[...])

    @pl.when(i < n_tiles)
    def _():
        norm_previous()
        h = _modulate(x_ref[...], mod_ref, j).astype(BF16)
        start = 0
        for width in FFN_CHUNKS:
            g = jnp.dot(h, wi_ref[:, D_FF + start:D_FF + start + width], preferred_element_type=F32)
            gate = g * jax.nn.sigmoid(g)
            a = jnp.dot(h, wi_ref[:, start:start + width], preferred_element_type=F32)
            u_ref[:, start:start + width] = (a * gate).astype(BF16)
            start += width
        y = 0.5 * jnp.dot(u_ref[...], wo_ref[...], preferred_element_type=F32)
        z_ref[...] = ALPHA * x_ref[...] + mod_ref[3 * j + 2:3 * j + 3, :] * y

    @pl.when(i == n_tiles)
    def _():
        norm_previous()


def _ffn_call(x, mod, j, wi_sets, wo_sets, which, g, b, tm):
    bsz, length, _ = x.shape
    per_batch = length // tm
    n_tiles = bsz * per_batch
    cur = lambda i: jnp.minimum(i, n_tiles - 1)
    prev = lambda i: jnp.maximum(i - 1, 0)
    tile = lambda pick: pl.BlockSpec((None, tm, D_MODEL), lambda i: (pick(i) // per_batch, pick(i) % per_batch, 0))
    weight = lambda rows, cols: pl.BlockSpec((None, rows, cols), lambda i: (which, 0, 0),
                                             pipeline_mode=pl.Buffered(1))
    return pl.pallas_call(
        functools.partial(_ffn_kernel, j, n_tiles),
        out_shape=jax.ShapeDtypeStruct(x.shape, F32),
        grid=(n_tiles + 1,),
        in_specs=[
            tile(cur),
            pl.BlockSpec((None, N_MOD, D_MODEL), lambda i: (cur(i) // per_batch, 0, 0)),
            weight(D_MODEL, 2 * D_FF),
            weight(D_FF, D_MODEL),
            _resident((1, D_MODEL)),
            _resident((1, D_MODEL)),
        ],
        out_specs=tile(prev),
        scratch_shapes=[pltpu.VMEM((tm, D_FF), BF16), pltpu.VMEM((tm, D_MODEL), F32)],
        compiler_params=pltpu.CompilerParams(dimension_semantics=("arbitrary",), vmem_limit_bytes=VMEM_LIMIT),
        name="ffn",
    )(x, mod, wi_sets, wo_sets, g.reshape(1, D_MODEL), b.reshape(1, D_MODEL))


def _rope(t, cos_ref, sa_ref, sb_ref):
    half = AXIS_DIM // 2
    outs = []
    for c in range(t.shape[1] // LANES):
        tc = t[:, c * LANES:(c + 1) * LANES]
        up = pltpu.roll(tc, LANES - half, 1)
        down = pltpu.roll(tc, half, 1)
        outs.append(tc * cos_ref[...] + up * sa_ref[...] + down * sb_ref[...])
    return jnp.concatenate(outs, axis=1)


def _dup_heads(t):
    pieces = []
    for h in range(t.shape[1] // HEAD_DIM):
        head = t[:, h * HEAD_DIM:(h + 1) * HEAD_DIM]
        pieces += [head, head]
    return jnp.concatenate(pieces, axis=1)


def _qkv_kernel(x_ref, mod_ref, w_ref, b_ref, cos_ref, sa_ref, sb_ref, q_ref, k_ref, vt_ref):
    h = _modulate(x_ref[...], mod_ref, 1).astype(BF16)
    qkv = jnp.dot(h, w_ref[...], preferred_element_type=F32) + b_ref[...]
    q = qkv[:, :Q_WIDTH] * (HEAD_DIM ** -0.5 * LOG2E)
    q_ref[...] = _rope(q, cos_ref, sa_ref, sb_ref).astype(BF16)
    k = _rope(qkv[:, Q_WIDTH:Q_WIDTH + KV_WIDTH], cos_ref, sa_ref, sb_ref)
    k_ref[...] = _dup_heads(k).astype(BF16)
    v = _dup_heads(qkv[:, Q_WIDTH + KV_WIDTH:])
    for blk in range(vt_ref.shape[0]):
        vt_ref[blk] = v[blk * BLOCK:(blk + 1) * BLOCK, :].T.astype(BF16)


def _qkv_call(x, mod, w, b, cos_t, sa_t, sb_t, tm):
    bsz, length, _ = x.shape
    tok = lambda width: pl.BlockSpec((None, tm, width), lambda bi, t: (bi, t, 0))
    table = pl.BlockSpec((tm, LANES), lambda bi, t: (t, 0))
    return pl.pallas_call(
        _qkv_kernel,
        out_shape=(jax.ShapeDtypeStruct((bsz, length, Q_WIDTH), BF16),
                   jax.ShapeDtypeStruct((bsz, length, KV_DUP_WIDTH), BF16),
                   jax.ShapeDtypeStruct((bsz, length // BLOCK, KV_DUP_WIDTH, BLOCK), BF16)),
        grid=(bsz, length // tm),
        in_specs=[
            tok(D_MODEL),
            pl.BlockSpec((None, N_MOD, D_MODEL), lambda bi, t: (bi, 0, 0)),
            _resident((D_MODEL, QKV_WIDTH)),
            _resident((1, QKV_WIDTH)),
            table, table, table,
        ],
        out_specs=(tok(Q_WIDTH), tok(KV_DUP_WIDTH),
                   pl.BlockSpec((None, tm // BLOCK, KV_DUP_WIDTH, BLOCK), lambda bi, t: (bi, t, 0, 0))),
        compiler_params=_params(2),
        name="qkv",
    )(x, mod, w, b.reshape(1, QKV_WIDTH), cos_t, sa_t, sb_t)


def _kv_ctx_kernel(x_ref, mod_ref, w_ref, b_ref, k_ref, vt_ref):
    h = _modulate(x_ref[...], mod_ref, 1).astype(BF16)
    kv = jnp.dot(h, w_ref[...], preferred_element_type=F32) + b_ref[...]
    k_ref[...] = _dup_heads(kv[:, :KV_WIDTH]).astype(BF16)
    vt_ref[...] = _dup_heads(kv[:, KV_WIDTH:]).T.astype(BF16)


def _kv_ctx_call(x, mod, w, b):
    bsz, n_ctx, _ = x.shape
    return pl.pallas_call(
        _kv_ctx_kernel,
        out_shape=(jax.ShapeDtypeStruct((bsz, n_ctx, KV_DUP_WIDTH), BF16),
                   jax.ShapeDtypeStruct((bsz, KV_DUP_WIDTH, n_ctx), BF16)),
        grid=(bsz,),
        in_specs=[
            pl.BlockSpec((None, n_ctx, D_MODEL), lambda bi: (bi, 0, 0)),
            _resident((None, N_MOD, D_MODEL)),
            _resident((D_MODEL, 2 * KV_WIDTH)),
            _resident((1, 2 * KV_WIDTH)),
        ],
        out_specs=(pl.BlockSpec((None, n_ctx, KV_DUP_WIDTH), lambda bi: (bi, 0, 0)),
                   pl.BlockSpec((None, KV_DUP_WIDTH, n_ctx), lambda bi: (bi, 0, 0))),
        compiler_params=_params(1),
        name="kv_ctx",
    )(x, mod, w, b.reshape(1, 2 * KV_WIDTH))


def _attn_kernel(n_blocks, n_steps, sink_ref, q_ref, k_ref, vt_ref, kc_ref, vct_ref, xp_ref, modp_ref, wo_ref,
                 g_ref, b_ref, wi32_ref, wo32_ref, out_ref, wi16_ref, wo16_ref, o_ref):
    i = pl.program_id(0)

    @pl.when(i == 0)
    def _():
        o_ref[...] = jnp.zeros_like(o_ref)

    def project(o_prev):
        return jnp.dot(o_prev, wo_ref[...], preferred_element_type=F32)

    def norm(y):
        out_ref[...] = _post_norm(xp_ref[...], y, modp_ref, 1, g_ref, b_ref)

    def round_weights():
        wi16_ref[...] = wi32_ref[...].astype(BF16)
        wo16_ref[...] = wo32_ref[...].astype(BF16)

    @pl.when(i < n_steps)
    def _():
        round_weights()
        o_prev = o_ref[...]
        y = []
        tile_in_batch = i % (n_blocks * BLOCK // q_ref.shape[0])
        _attend(n_blocks, tile_in_batch, sink_ref, q_ref, k_ref, vt_ref, kc_ref, vct_ref, o_ref,
                after_slot={9: lambda: y.append(project(o_prev)), 21: lambda: norm(y[0])})

    @pl.when(i == n_steps)
    def _():
        round_weights()
        norm(project(o_ref[...]))


def _attend(n_blocks, tile_in_batch, sink_ref, q_ref, k_ref, vt_ref, kc_ref, vct_ref, o_ref, after_slot):
    n_sub = q_ref.shape[0] // BLOCK
    qb0 = tile_in_batch * n_sub
    cols2 = 2 * BLOCK
    key_i = lax.broadcasted_iota(jnp.int32, (BLOCK, cols2), 0)
    qry_i = lax.broadcasted_iota(jnp.int32, (BLOCK, cols2), 1) & (BLOCK - 1)
    first_head = lax.broadcasted_iota(jnp.int32, (1, cols2), 1) < BLOCK
    lo_q = lax.broadcasted_iota(jnp.int32, (cols2, LANES), 1) < HEAD_DIM
    lo_v = lax.broadcasted_iota(jnp.int32, (LANES, 1), 0) < HEAD_DIM

    def block_ids(sub):
        return [jnp.clip(qb0 + sub + d, 0, n_blocks - 1) for d in (-1, 0, 1)]

    slots_per_block = 2 * N_KV_HEADS
    p_t, inv = {}, {}

    def scores(slot):
        sub, rest = divmod(slot, slots_per_block)
        h, half = divmod(rest, 2)
        c0 = 2 * h * LANES
        q_rows = slice(sub * BLOCK, (sub + 1) * BLOCK)
        q_cat = jnp.concatenate([q_ref[q_rows, c0:c0 + LANES], q_ref[q_rows, c0 + LANES:c0 + 2 * LANES]], axis=0)
        keep = lo_q if half == 0 else jnp.logical_not(lo_q)
        q_sel = jnp.where(keep, q_cat, jnp.zeros_like(q_cat))
        lanes = slice(h * LANES, (h + 1) * LANES)
        k_h = jnp.concatenate([k_ref[pl.ds(pl.multiple_of(blk * BLOCK, BLOCK), BLOCK), lanes]
                               for blk in block_ids(sub)] + [kc_ref[:, lanes]], axis=0)
        return lax.dot_general(k_h, q_sel, (((1,), (1,)), ((), ())), preferred_element_type=F32)

    def softmax(slot, s_t):
        sub, rest = divmod(slot, slots_per_block)
        h, half = divmod(rest, 2)
        left_ok = (qry_i <= key_i) & (qb0 + sub > 0)
        right_ok = (key_i <= qry_i) & (qb0 + sub < n_blocks - 1)
        sink = jnp.where(first_head, sink_ref[4 * h + half], sink_ref[4 * h + 2 + half]) * LOG2E
        parts = [jnp.where(left_ok, s_t[:BLOCK], NEG_INF),
                 s_t[BLOCK:2 * BLOCK],
                 jnp.where(right_ok, s_t[2 * BLOCK:SPAN], NEG_INF)]
        parts += [s_t[c:c + BLOCK] for c in range(SPAN, s_t.shape[0], BLOCK)]
        m = jnp.max(functools.reduce(jnp.maximum, parts), axis=0, keepdims=True)
        m = jnp.maximum(m, sink)
        p = [jnp.exp2(t - m) for t in parts]
        denom = jnp.sum(functools.reduce(jnp.add, p), axis=0, keepdims=True) + jnp.exp2(sink - m)
        inv[slot] = 1.0 / denom
        p_t[slot] = jnp.concatenate(p, axis=0).astype(BF16)

    def weighted_values(pair):
        sub, h = divmod(pair, N_KV_HEADS)
        rows = slice(h * LANES, (h + 1) * LANES)
        v_t = jnp.concatenate([vt_ref[blk, rows, :] for blk in block_ids(sub)] + [vct_ref[rows, :]], axis=1)
        v_lo = jnp.where(lo_v, v_t, jnp.zeros_like(v_t))
        v_hi = jnp.where(lo_v, jnp.zeros_like(v_t), v_t)
        acc_t = (jnp.dot(v_lo, p_t.pop(2 * pair), preferred_element_type=F32)
                 + jnp.dot(v_hi, p_t.pop(2 * pair + 1), preferred_element_type=F32))
        out = (acc_t * jnp.where(lo_v, inv.pop(2 * pair), inv.pop(2 * pair + 1))).T.astype(BF16)
        c0 = 2 * h * LANES
        q_rows = slice(sub * BLOCK, (sub + 1) * BLOCK)
        o_ref[q_rows, c0:c0 + LANES] = out[:BLOCK]
        o_ref[q_rows, c0 + LANES:c0 + 2 * LANES] = out[BLOCK:]

    n_slots = n_sub * slots_per_block
    ahead = 1
    s_t = {slot: scores(slot) for slot in range(ahead)}
    for slot in range(n_slots):
        if slot + ahead < n_slots:
            s_t[slot + ahead] = scores(slot + ahead)
        softmax(slot, s_t.pop(slot))
        if slot % 2 == 0 and slot > 0:
            weighted_values(slot // 2 - 1)
        if slot in after_slot:
            after_slot[slot]()
    weighted_values(n_slots // 2 - 1)


def _attn_call(q, k, vt, kc, vct, sink, x, mod, wo, g, b, wi32, wo32, tq):
    bsz, length, _ = q.shape
    n_blocks = length // BLOCK
    per_batch_tiles = length // tq
    n_steps = bsz * per_batch_tiles
    n_later = wi32.shape[0] - 1
    wi_cols, wo_rows = 2 * D_FF // WEIGHT_SLABS, D_FF // WEIGHT_SLABS
    assert n_later * WEIGHT_SLABS <= n_steps + 1
    slab = lambda i: jnp.minimum(i, n_later * WEIGHT_SLABS - 1)
    cur = lambda i: jnp.minimum(i, n_steps - 1)
    prev = lambda i: jnp.maximum(i - 1, 0)
    tile = lambda pick: pl.BlockSpec(
        (None, tq, D_MODEL), lambda i, *_: (pick(i) // per_batch_tiles, pick(i) % per_batch_tiles, 0))
    per_batch = lambda arr: pl.BlockSpec(
        (None,) + arr.shape[1:], lambda i, *_: (cur(i) // per_batch_tiles,) + (0,) * (arr.ndim - 1))
    return pl.pallas_call(
        functools.partial(_attn_kernel, n_blocks, n_steps),
        out_shape=(jax.ShapeDtypeStruct(x.shape, F32),
                   jax.ShapeDtypeStruct((n_later,) + wi32.shape[1:], BF16),
                   jax.ShapeDtypeStruct((n_later,) + wo32.shape[1:], BF16)),
        grid_spec=pltpu.PrefetchScalarGridSpec(
            num_scalar_prefetch=1,
            grid=(n_steps + 1,),
            in_specs=[
                tile(cur),
                per_batch(k), per_batch(vt), per_batch(kc), per_batch(vct),
                tile(prev),
                pl.BlockSpec((None, N_MOD, D_MODEL), lambda i, *_: (prev(i) // per_batch_tiles, 0, 0)),
                _resident((D_MODEL, D_MODEL)),
                _resident((1, D_MODEL)),
                _resident((1, D_MODEL)),
                pl.BlockSpec((None, D_MODEL, wi_cols),
                             lambda i, *_: (1 + slab(i) // WEIGHT_SLABS, 0, slab(i) % WEIGHT_SLABS)),
                pl.BlockSpec((None, wo_rows, D_MODEL),
                             lambda i, *_: (1 + slab(i) // WEIGHT_SLABS, slab(i) % WEIGHT_SLABS, 0)),
            ],
            out_specs=(tile(prev),
                       pl.BlockSpec((None, D_MODEL, wi_cols),
                                    lambda i, *_: (slab(i) // WEIGHT_SLABS, 0, slab(i) % WEIGHT_SLABS)),
                       pl.BlockSpec((None, wo_rows, D_MODEL),
                                    lambda i, *_: (slab(i) // WEIGHT_SLABS, slab(i) % WEIGHT_SLABS, 0))),
            scratch_shapes=[pltpu.VMEM((tq, Q_WIDTH), BF16)],
        ),
        compiler_params=pltpu.CompilerParams(dimension_semantics=("arbitrary",), vmem_limit_bytes=VMEM_LIMIT),
        name="attn",
    )(sink, q, k, vt, kc, vct, x, mod, wo, g.reshape(1, D_MODEL), b.reshape(1, D_MODEL), wi32, wo32)


def _reverse_shift(t, first_row, flip_ref):
    rows = t.shape[0]
    blocks = [jnp.dot(flip_ref[...], t[r0:r0 + FLIP, :], preferred_element_type=F32)
              for r0 in range(rows - FLIP, -FLIP, -FLIP)]
    shifted = pltpu.roll(jnp.concatenate(blocks, axis=0), 1, 0)
    is_first = lax.broadcasted_iota(jnp.int32, shifted.shape, 0) == 0
    return jnp.where(is_first, first_row, shifted)


def _channel_dft(t, w_ref):
    return jnp.concatenate(
        [jnp.dot(t[:, g0:g0 + FOURIER_GROUP_CH], w_ref[...], preferred_element_type=F32)
         for g0 in range(0, D_MODEL, FOURIER_GROUP_CH)], axis=1)


def _dft_fold_kernel(lo_ref, up_ref, mod_ref, cc_ref, sc_ref, flip_ref, ab_ref, an_ref, carry_ref):
    @pl.when(pl.program_id(1) == 0)
    def _():
        carry_ref[...] = jnp.zeros_like(carry_ref)

    h_lo = _modulate(lo_ref[...], mod_ref, 1)
    h_up = _modulate(up_ref[...], mod_ref, 1).astype(BF16)
    partner = _reverse_shift(h_up, carry_ref[0:1, :], flip_ref)
    carry_ref[...] = h_up[0:8, :].astype(F32)
    ab_ref[0] = _channel_dft((h_lo + partner).astype(BF16), cc_ref).astype(BF16)
    ab_ref[1] = _channel_dft((h_lo - partner).astype(BF16), sc_ref).astype(BF16)
    an_ref[...] = _channel_dft(h_up[0:8, :], cc_ref)


def _dft_fold_call(x4, mod, cc, sc, flip, tm):
    bsz, _, half, _ = x4.shape
    n_t = half // tm
    ch = FOURIER_GROUP_CH
    return pl.pallas_call(
        _dft_fold_kernel,
        out_shape=(jax.ShapeDtypeStruct((bsz, 2, half, D_MODEL), BF16),
                   jax.ShapeDtypeStruct((bsz, 8, D_MODEL), F32)),
        grid=(bsz, n_t),
        in_specs=[
            pl.BlockSpec((None, None, tm, D_MODEL), lambda bi, t: (bi, 0, t, 0)),
            pl.BlockSpec((None, None, tm, D_MODEL), lambda bi, t: (bi, 1, n_t - 1 - t, 0)),
            pl.BlockSpec((None, N_MOD, D_MODEL), lambda bi, t: (bi, 0, 0)),
            _resident((ch, ch)),
            _resident((ch, ch)),
            _resident((FLIP, FLIP)),
        ],
        out_specs=(pl.BlockSpec((None, 2, tm, D_MODEL), lambda bi, t: (bi, 0, t, 0)),
                   pl.BlockSpec((None, 8, D_MODEL), lambda bi, t: (bi, 0, 0))),
        scratch_shapes=[pltpu.VMEM((8, D_MODEL), F32)],
        compiler_params=pltpu.CompilerParams(dimension_semantics=("parallel", "arbitrary"),
                                             vmem_limit_bytes=VMEM_LIMIT),
        name="dft_fold",
    )(x4, x4, mod, cc, sc, flip)


def _dft_seq_kernel(norm, n_t, wlo_ref, wup_ref, ab_ref, an_ref, alt_ref, flip_ref, x_ref, mod_ref, wo_ref,
                    g_ref, beta_ref, out_ref, carry_ref):
    t = pl.program_id(1)
    tk = wlo_ref.shape[0]
    ab = ab_ref[...]
    a_n = an_ref[0:1, :]
    row = lax.broadcasted_iota(jnp.int32, (tk, 1), 0)

    def alternating(k0):
        return (1 - 2 * ((k0 + row) & 1)).astype(F32)

    f_lo = (jnp.dot(wlo_ref[...], ab, preferred_element_type=F32) + alternating(t * tk) * a_n) * norm
    f_mir = (jnp.dot(wup_ref[...], ab, preferred_element_type=F32) + alternating((n_t - 1 - t) * tk) * a_n) * norm
    f_mir = f_mir.astype(BF16)

    @pl.when(t == 0)
    def _():
        f_n = (jnp.dot(alt_ref[...], ab, preferred_element_type=F32)[0:1, :] + a_n) * norm
        carry_ref[0:1, :] = f_n.astype(BF16).astype(F32)

    f_up = _reverse_shift(f_mir, carry_ref[0:1, :], flip_ref)
    carry_ref[...] = f_mir[0:8, :].astype(F32)
    y_lo = jnp.dot(f_lo.astype(BF16), wo_ref[...], preferred_element_type=F32)
    y_up = jnp.dot(f_up.astype(BF16), wo_ref[...], preferred_element_type=F32)
    out_ref[0] = _post_norm(x_ref[0], y_lo, mod_ref, 1, g_ref, beta_ref)
    out_ref[1] = _post_norm(x_ref[1], y_up, mod_ref, 1, g_ref, beta_ref)


def _dft_seq_call(w_lo, w_up, ab, a_n, alt, flip, x4, mod, wo, g, beta, tk):
    bsz, _, half, _ = x4.shape
    n_t = half // tk
    norm = float(1.0 / np.sqrt(2 * half * FOURIER_GROUP_CH))
    tok = pl.BlockSpec((None, 2, tk, D_MODEL), lambda bi, t: (bi, 0, t, 0))
    return pl.pallas_call(
        functools.partial(_dft_seq_kernel, norm, n_t),
        out_shape=jax.ShapeDtypeStruct(x4.shape, F32),
        grid=(bsz, n_t),
        in_specs=[
            pl.BlockSpec((tk, 2 * half), lambda bi, t: (t, 0)),
            pl.BlockSpec((tk, 2 * half), lambda bi, t: (n_t - 1 - t, 0)),
            pl.BlockSpec((None, 2 * half, D_MODEL), lambda bi, t: (bi, 0, 0)),
            pl.BlockSpec((None, 8, D_MODEL), lambda bi, t: (bi, 0, 0)),
            _resident((8, 2 * half)),
            _resident((FLIP, FLIP)),
            tok,
            pl.BlockSpec((None, N_MOD, D_MODEL), lambda bi, t: (bi, 0, 0)),
            _resident((D_MODEL, D_MODEL)),
            _resident((1, D_MODEL)),
            _resident((1, D_MODEL)),
        ],
        out_specs=tok,
        scratch_shapes=[pltpu.VMEM((8, D_MODEL), F32)],
        compiler_params=pltpu.CompilerParams(dimension_semantics=("parallel", "arbitrary"),
                                             vmem_limit_bytes=VMEM_LIMIT),
        name="dft_seq",
    )(w_lo, w_up, ab, a_n, alt, flip, x4, mod, wo, g.reshape(1, D_MODEL), beta.reshape(1, D_MODEL))


def _rope_tables(n_tokens):
    rows = n_tokens // GRID_W
    row = jnp.repeat(jnp.arange(rows), GRID_W).astype(F32)
    col = jnp.tile(jnp.arange(GRID_W), rows).astype(F32)
    inv = ROPE_BASE ** (-jnp.arange(0, AXIS_DIM, 2, dtype=F32) / AXIS_DIM)
    ang_r, ang_c = row[:, None] * inv, col[:, None] * inv
    cos_r, sin_r, cos_c, sin_c = jnp.cos(ang_r), jnp.sin(ang_r), jnp.cos(ang_c), jnp.sin(ang_c)
    zero = jnp.zeros_like(sin_r)
    cos_h = jnp.concatenate([cos_r, cos_r, cos_c, cos_c], axis=-1)
    sa_h = jnp.concatenate([-sin_r, zero, -sin_c, zero], axis=-1)
    sb_h = jnp.concatenate([zero, sin_r, zero, sin_c], axis=-1)
    rep = LANES // HEAD_DIM
    return tuple(jnp.tile(t, (1, rep)) for t in (cos_h, sa_h, sb_h))


def _dft_tables(n, rows):
    step = TWIDDLE_STEP if rows % TWIDDLE_STEP == 0 else rows
    col = jnp.arange(rows, dtype=jnp.int32)[None, :]

    def thin(k):
        ang = ((k[:, None] * col) % n).astype(F32) * (2.0 * np.pi / n)
        return jnp.cos(ang), jnp.sin(ang)

    cos1, sin1 = (t[:, None, :] for t in thin(jnp.arange(0, rows, step, dtype=jnp.int32)))
    cos0, sin0 = (t[None, :, :] for t in thin(jnp.arange(step, dtype=jnp.int32)))
    return ((cos1 * cos0 - sin1 * sin0).reshape(rows, rows),
            (sin1 * cos0 + cos1 * sin0).reshape(rows, rows))


def kernel(x, c, ctx, c_ctx, mod_w, mod_b, ln_g, ln_b, ffn_wi, ffn_wo,
           attn_wqkv, attn_bqkv, attn_wo, attn_sink, fourier_wo):
    bsz, seq, _ = x.shape
    n_ctx = ctx.shape[1]
    assert DEPTH == 2 and seq % BLOCK == 0 and seq % GRID_W == 0

    cv = jnp.concatenate([c, c_ctx[None, :], jnp.zeros((MOD_ROWS - bsz - 1, D_MODEL), F32)], axis=0)
    mod = _mod_call(cv, mod_w, mod_b).reshape(DEPTH, MOD_ROWS, N_MOD, D_MODEL)
    wi32 = ffn_wi.reshape(2 * DEPTH, D_MODEL, 2 * D_FF)
    wo32 = ffn_wo.reshape(2 * DEPTH, D_FF, D_MODEL)
    wi0, wo0 = wi32[:1].astype(BF16), wo32[:1].astype(BF16)

    mod_lat, mod_ctx = mod[0, :bsz], mod[0, bsz:bsz + 1]
    g, b = ln_g[0], ln_b[0]
    x = _ffn_call(x, mod_lat, 0, wi0, wo0, 0, g[0], b[0], tm=FFN_TILE)
    ctx_s = _ffn_call(ctx.reshape(1, bsz * n_ctx, D_MODEL), mod_ctx, 0, wi0, wo0, 0, g[0], b[0], tm=FFN_TILE)

    w_qkv = attn_wqkv[0].astype(BF16)
    cos_t, sa_t, sb_t = _rope_tables(seq)
    q, k, vt = _qkv_call(x, mod_lat, w_qkv, attn_bqkv[0], cos_t, sa_t, sb_t, tm=1024)
    kc, vct = _kv_ctx_call(ctx_s.reshape(bsz, n_ctx, D_MODEL), mod_ctx, w_qkv[:, Q_WIDTH:], attn_bqkv[0, Q_WIDTH:])
    x, wi, wo = _attn_call(q, k, vt, kc, vct, attn_sink[0], x, mod_lat, attn_wo[0].astype(BF16), g[1], b[1],
                           wi32, wo32, tq=4 * BLOCK)
    x = _ffn_call(x, mod_lat, 2, wi, wo, 0, g[2], b[2], tm=FFN_TILE)

    mod_lat = mod[1, :bsz]
    g, b = ln_g[1], ln_b[1]
    x = _ffn_call(x, mod_lat, 0, wi, wo, 1, g[0], b[0], tm=FFN_TILE)
    half = seq // 2
    cos_c, sin_c = _dft_tables(FOURIER_GROUP_CH, FOURIER_GROUP_CH)
    cos_s, sin_s = _dft_tables(seq, half)
    w_lo = jnp.concatenate([cos_s, -sin_s], axis=1).astype(BF16)
    w_up = jnp.concatenate([cos_s, sin_s], axis=1).astype(BF16)
    alt = jnp.zeros((8, seq), F32).at[0, :half].set(1.0 - 2.0 * (jnp.arange(half) % 2)).astype(BF16)
    anti = jnp.arange(FLIP)[:, None] + jnp.arange(FLIP)[None, :] == FLIP - 1
    flip = anti.astype(BF16)
    x4 = x.reshape(bsz, 2, half, D_MODEL)
    ab, a_n = _dft_fold_call(x4, mod_lat, cos_c.astype(BF16), sin_c.astype(BF16), flip, tm=1024)
    x4 = _dft_seq_call(w_lo, w_up, ab.reshape(bsz, seq, D_MODEL), a_n, alt, flip, x4, mod_lat,
                       fourier_wo[0].astype(BF16), g[1], b[1], tk=512)
    x = x4.reshape(bsz, seq, D_MODEL)
    x = _ffn_call(x, mod_lat, 2, wi, wo, 2, g[2], b[2], tm=FFN_TILE)
    return x
```

```python
import functools

import jax
import jax.numpy as jnp
import numpy as np
from jax import lax
from jax.experimental import pallas as pl
from jax.experimental.pallas import tpu as pltpu

D_MODEL = 1024
DEPTH = 2
GRID_W = 64
N_HEADS = 16
N_KV_HEADS = 4
HEAD_DIM = 64
GROUP = N_HEADS // N_KV_HEADS
Q_WIDTH = N_HEADS * HEAD_DIM
KV_WIDTH = N_KV_HEADS * HEAD_DIM
QKV_WIDTH = Q_WIDTH + 2 * KV_WIDTH
KV_DUP_WIDTH = 2 * KV_WIDTH
WINDOW = 128
BLOCK = 128
SPAN = BLOCK + 2 * WINDOW
ROPE_BASE = 10000.0
AXIS_DIM = HEAD_DIM // 2
FOURIER_GROUPS = 4
FOURIER_GROUP_CH = D_MODEL // FOURIER_GROUPS
D_FF = 2816
N_MOD = 9
LN_EPS = 1e-5
ALPHA = (2.0 * DEPTH) ** 0.25
NEG_INF = -1e30
LOG2E = float(np.log2(np.e))

LANES = 128
MOD_ROWS = 24
VMEM_LIMIT = 60 * 1024 * 1024
FFN_TILE = 1024
FFN_CHUNKS = (256,) * 11
FLIP = 256
TWIDDLE_STEP = 32
WEIGHT_SLABS = 11

BF16 = jnp.bfloat16
F32 = jnp.float32


def _params(n_axes):
    return pltpu.CompilerParams(dimension_semantics=("parallel",) * n_axes,
                                vmem_limit_bytes=VMEM_LIMIT)


def _resident(shape):
    return pl.BlockSpec(shape, lambda *_: (0,) * len(shape), pipeline_mode=pl.Buffered(1))


def _layer_norm(z, g, b):
    mu = jnp.mean(z, axis=-1, keepdims=True)
    d = z - mu
    var = jnp.mean(d * d, axis=-1, keepdims=True)
    return d * lax.rsqrt(var + LN_EPS) * g + b


def _modulate(x, mod_ref, j):
    shift = mod_ref[3 * j:3 * j + 1, :]
    scale = mod_ref[3 * j + 1:3 * j + 2, :]
    return x * (1.0 + scale) + shift


def _post_norm(x, y, mod_ref, j, g_ref, b_ref):
    gate = mod_ref[3 * j + 2:3 * j + 3, :]
    return _layer_norm(ALPHA * x + gate * y, g_ref[...], b_ref[...])


def _mod_kernel(cv_ref, w_ref, b_ref, o_ref):
    cv = cv_ref[...]
    s = (cv * jax.nn.sigmoid(cv)).astype(BF16)
    o_ref[...] = jnp.dot(s, w_ref[...].astype(BF16), preferred_element_type=F32) + b_ref[...]


def _mod_call(cv, mod_w, mod_b):
    tn = 3 * D_MODEL
    n_out = N_MOD * D_MODEL
    return pl.pallas_call(
        _mod_kernel,
        out_shape=jax.ShapeDtypeStruct((DEPTH, MOD_ROWS, n_out), F32),
        grid=(DEPTH, n_out // tn),
        in_specs=[
            pl.BlockSpec((MOD_ROWS, D_MODEL), lambda i, n: (0, 0)),
            pl.BlockSpec((None, D_MODEL, tn), lambda i, n: (i, 0, n)),
            pl.BlockSpec((None, 1, tn), lambda i, n: (i, 0, n)),
        ],
        out_specs=pl.BlockSpec((None, MOD_ROWS, tn), lambda i, n: (i, 0, n)),
        compiler_params=_params(2),
        name="mod",
    )(cv, mod_w, mod_b.reshape(DEPTH, 1, n_out))


def _ffn_kernel(j, n_tiles, x_ref, mod_ref, wi_ref, wo_ref, g_ref, b_ref, o_ref, u_ref, z_ref):
    i = pl.program_id(0)

    @pl.when(i == 0)
    def _():
        z_ref[...] = jnp.zeros_like(z_ref)

    def norm_previous():
        o_ref[...] = _layer_norm(z_ref[...], g_ref[...], b_ref[...])

    @pl.when(i < n_tiles)
    def _():
        norm_previous()
        h = _modulate(x_ref[...], mod_ref, j).astype(BF16)
        start = 0
        for width in FFN_CHUNKS:
            g = jnp.dot(h, wi_ref[:, D_FF + start:D_FF + start + width], preferred_element_type=F32)
            gate = g * jax.nn.sigmoid(g)
            a = jnp.dot(h, wi_ref[:, start:start + width], preferred_element_type=F32)
            u_ref[:, start:start + width] = (a * gate).astype(BF16)
            start += width
        y = 0.5 * jnp.dot(u_ref[...], wo_ref[...], preferred_element_type=F32)
        z_ref[...] = ALPHA * x_ref[...] + mod_ref[3 * j + 2:3 * j + 3, :] * y

    @pl.when(i == n_tiles)
    def _():
        norm_previous()


def _ffn_call(x, mod, j, wi_sets, wo_sets, which, g, b, tm):
    bsz, length, _ = x.shape
    per_batch = length // tm
    n_tiles = bsz * per_batch
    cur = lambda i: jnp.minimum(i, n_tiles - 1)
    prev = lambda i: jnp.maximum(i - 1, 0)
    tile = lambda pick: pl.BlockSpec((None, tm, D_MODEL), lambda i: (pick(i) // per_batch, pick(i) % per_batch, 0))
    weight = lambda rows, cols: pl.BlockSpec((None, rows, cols), lambda i: (which, 0, 0),
                                             pipeline_mode=pl.Buffered(1))
    return pl.pallas_call(
        functools.partial(_ffn_kernel, j, n_tiles),
        out_shape=jax.ShapeDtypeStruct(x.shape, F32),
        grid=(n_tiles + 1,),
        in_specs=[
            tile(cur),
            pl.BlockSpec((None, N_MOD, D_MODEL), lambda i: (cur(i) // per_batch, 0, 0)),
            weight(D_MODEL, 2 * D_FF),
            weight(D_FF, D_MODEL),
            _resident((1, D_MODEL)),
            _resident((1, D_MODEL)),
        ],
        out_specs=tile(prev),
        scratch_shapes=[pltpu.VMEM((tm, D_FF), BF16), pltpu.VMEM((tm, D_MODEL), F32)],
        compiler_params=pltpu.CompilerParams(dimension_semantics=("arbitrary",), vmem_limit_bytes=VMEM_LIMIT),
        name="ffn",
    )(x, mod, wi_sets, wo_sets, g.reshape(1, D_MODEL), b.reshape(1, D_MODEL))


def _rope(t, cos_ref, sa_ref, sb_ref):
    half = AXIS_DIM // 2
    outs = []
    for c in range(t.shape[1] // LANES):
        tc = t[:, c * LANES:(c + 1) * LANES]
        up = pltpu.roll(tc, LANES - half, 1)
        down = pltpu.roll(tc, half, 1)
        outs.append(tc * cos_ref[...] + up * sa_ref[...] + down * sb_ref[...])
    return jnp.concatenate(outs, axis=1)


def _dup_heads(t):
    pieces = []
    for h in range(t.shape[1] // HEAD_DIM):
        head = t[:, h * HEAD_DIM:(h + 1) * HEAD_DIM]
        pieces += [head, head]
    return jnp.concatenate(pieces, axis=1)


def _qkv_kernel(x_ref, mod_ref, w_ref, b_ref, cos_ref, sa_ref, sb_ref, q_ref, k_ref, vt_ref):
    h = _modulate(x_ref[...], mod_ref, 1).astype(BF16)
    qkv = jnp.dot(h, w_ref[...], preferred_element_type=F32) + b_ref[...]
    q = qkv[:, :Q_WIDTH] * (HEAD_DIM ** -0.5 * LOG2E)
    q_ref[...] = _rope(q, cos_ref, sa_ref, sb_ref).astype(BF16)
    k = _rope(qkv[:, Q_WIDTH:Q_WIDTH + KV_WIDTH], cos_ref, sa_ref, sb_ref)
    k_ref[...] = _dup_heads(k).astype(BF16)
    v = _dup_heads(qkv[:, Q_WIDTH + KV_WIDTH:])
    for blk in range(vt_ref.shape[0]):
        vt_ref[blk] = v[blk * BLOCK:(blk + 1) * BLOCK, :].T.astype(BF16)


def _qkv_call(x, mod, w, b, cos_t, sa_t, sb_t, tm):
    bsz, length, _ = x.shape
    tok = lambda width: pl.BlockSpec((None, tm, width), lambda bi, t: (bi, t, 0))
    table = pl.BlockSpec((tm, LANES), lambda bi, t: (t, 0))
    return pl.pallas_call(
        _qkv_kernel,
        out_shape=(jax.ShapeDtypeStruct((bsz, length, Q_WIDTH), BF16),
                   jax.ShapeDtypeStruct((bsz, length, KV_DUP_WIDTH), BF16),
                   jax.ShapeDtypeStruct((bsz, length // BLOCK, KV_DUP_WIDTH, BLOCK), BF16)),
        grid=(bsz, length // tm),
        in_specs=[
            tok(D_MODEL),
            pl.BlockSpec((None, N_MOD, D_MODEL), lambda bi, t: (bi, 0, 0)),
            _resident((D_MODEL, QKV_WIDTH)),
            _resident((1, QKV_WIDTH)),
            table, table, table,
        ],
        out_specs=(tok(Q_WIDTH), tok(KV_DUP_WIDTH),
                   pl.BlockSpec((None, tm // BLOCK, KV_DUP_WIDTH, BLOCK), lambda bi, t: (bi, t, 0, 0))),
        compiler_params=_params(2),
        name="qkv",
    )(x, mod, w, b.reshape(1, QKV_WIDTH), cos_t, sa_t, sb_t)


def _kv_ctx_kernel(x_ref, mod_ref, w_ref, b_ref, k_ref, vt_ref):
    h = _modulate(x_ref[...], mod_ref, 1).astype(BF16)
    kv = jnp.dot(h, w_ref[...], preferred_element_type=F32) + b_ref[...]
    k_ref[...] = _dup_heads(kv[:, :KV_WIDTH]).astype(BF16)
    vt_ref[...] = _dup_heads(kv[:, KV_WIDTH:]).T.astype(BF16)


def _kv_ctx_call(x, mod, w, b):
    bsz, n_ctx, _ = x.shape
    return pl.pallas_call(
        _kv_ctx_kernel,
        out_shape=(jax.ShapeDtypeStruct((bsz, n_ctx, KV_DUP_WIDTH), BF16),
                   jax.ShapeDtypeStruct((bsz, KV_DUP_WIDTH, n_ctx), BF16)),
        grid=(bsz,),
        in_specs=[
            pl.BlockSpec((None, n_ctx, D_MODEL), lambda bi: (bi, 0, 0)),
            _resident((None, N_MOD, D_MODEL)),
            _resident((D_MODEL, 2 * KV_WIDTH)),
            _resident((1, 2 * KV_WIDTH)),
        ],
        out_specs=(pl.BlockSpec((None, n_ctx, KV_DUP_WIDTH), lambda bi: (bi, 0, 0)),
                   pl.BlockSpec((None, KV_DUP_WIDTH, n_ctx), lambda bi: (bi, 0, 0))),
        compiler_params=_params(1),
        name="kv_ctx",
    )(x, mod, w, b.reshape(1, 2 * KV_WIDTH))


def _attn_kernel(n_blocks, n_steps, sink_ref, q_ref, k_ref, vt_ref, kc_ref, vct_ref, xp_ref, modp_ref, wo_ref,
                 g_ref, b_ref, wi32_ref, wo32_ref, out_ref, wi16_ref, wo16_ref, o_ref):
    i = pl.program_id(0)

    @pl.when(i == 0)
    def _():
        o_ref[...] = jnp.zeros_like(o_ref)

    def project(o_prev):
        return jnp.dot(o_prev, wo_ref[...], preferred_element_type=F32)

    def norm(y):
        out_ref[...] = _post_norm(xp_ref[...], y, modp_ref, 1, g_ref, b_ref)

    def round_weights():
        wi16_ref[...] = wi32_ref[...].astype(BF16)
        wo16_ref[...] = wo32_ref[...].astype(BF16)

    @pl.when(i < n_steps)
    def _():
        round_weights()
        o_prev = o_ref[...]
        y = []
        tile_in_batch = i % (n_blocks * BLOCK // q_ref.shape[0])
        _attend(n_blocks, tile_in_batch, sink_ref, q_ref, k_ref, vt_ref, kc_ref, vct_ref, o_ref,
                after_slot={9: lambda: y.append(project(o_prev)), 21: lambda: norm(y[0])})

    @pl.when(i == n_steps)
    def _():
        round_weights()
        norm(project(o_ref[...]))


def _attend(n_blocks, tile_in_batch, sink_ref, q_ref, k_ref, vt_ref, kc_ref, vct_ref, o_ref, after_slot):
    n_sub = q_ref.shape[0] // BLOCK
    qb0 = tile_in_batch * n_sub
    cols2 = 2 * BLOCK
    key_i = lax.broadcasted_iota(jnp.int32, (BLOCK, cols2), 0)
    qry_i = lax.broadcasted_iota(jnp.int32, (BLOCK, cols2), 1) & (BLOCK - 1)
    first_head = lax.broadcasted_iota(jnp.int32, (1, cols2), 1) < BLOCK
    lo_q = lax.broadcasted_iota(jnp.int32, (cols2, LANES), 1) < HEAD_DIM
    lo_v = lax.broadcasted_iota(jnp.int32, (LANES, 1), 0) < HEAD_DIM

    def block_ids(sub):
        return [jnp.clip(qb0 + sub + d, 0, n_blocks - 1) for d in (-1, 0, 1)]

    slots_per_block = 2 * N_KV_HEADS
    p_t, inv = {}, {}

    def scores(slot):
        sub, rest = divmod(slot, slots_per_block)
        h, half = divmod(rest, 2)
        c0 = 2 * h * LANES
        q_rows = slice(sub * BLOCK, (sub + 1) * BLOCK)
        q_cat = jnp.concatenate([q_ref[q_rows, c0:c0 + LANES], q_ref[q_rows, c0 + LANES:c0 + 2 * LANES]], axis=0)
        keep = lo_q if half == 0 else jnp.logical_not(lo_q)
        q_sel = jnp.where(keep, q_cat, jnp.zeros_like(q_cat))
        lanes = slice(h * LANES, (h + 1) * LANES)
        k_h = jnp.concatenate([k_ref[pl.ds(pl.multiple_of(blk * BLOCK, BLOCK), BLOCK), lanes]
                               for blk in block_ids(sub)] + [kc_ref[:, lanes]], axis=0)
        return lax.dot_general(k_h, q_sel, (((1,), (1,)), ((), ())), preferred_element_type=F32)

    def softmax(slot, s_t):
        sub, rest = divmod(slot, slots_per_block)
        h, half = divmod(rest, 2)
        left_ok = (qry_i <= key_i) & (qb0 + sub > 0)
        right_ok = (key_i <= qry_i) & (qb0 + sub < n_blocks - 1)
        sink = jnp.where(first_head, sink_ref[4 * h + half], sink_ref[4 * h + 2 + half]) * LOG2E
        parts = [jnp.where(left_ok, s_t[:BLOCK], NEG_INF),
                 s_t[BLOCK:2 * BLOCK],
                 jnp.where(right_ok, s_t[2 * BLOCK:SPAN], NEG_INF)]
        parts += [s_t[c:c + BLOCK] for c in range(SPAN, s_t.shape[0], BLOCK)]
        m = jnp.max(functools.reduce(jnp.maximum, parts), axis=0, keepdims=True)
        m = jnp.maximum(m, sink)
        p = [jnp.exp2(t - m) for t in parts]
        denom = jnp.sum(functools.reduce(jnp.add, p), axis=0, keepdims=True) + jnp.exp2(sink - m)
        inv[slot] = 1.0 / denom
        p_t[slot] = jnp.concatenate(p, axis=0).astype(BF16)

    def weighted_values(pair):
        sub, h = divmod(pair, N_KV_HEADS)
        rows = slice(h * LANES, (h + 1) * LANES)
        v_t = jnp.concatenate([vt_ref[blk, rows, :] for blk in block_ids(sub)] + [vct_ref[rows, :]], axis=1)
        v_lo = jnp.where(lo_v, v_t, jnp.zeros_like(v_t))
        v_hi = jnp.where(lo_v, jnp.zeros_like(v_t), v_t)
        acc_t = (jnp.dot(v_lo, p_t.pop(2 * pair), preferred_element_type=F32)
                 + jnp.dot(v_hi, p_t.pop(2 * pair + 1), preferred_element_type=F32))
        out = (acc_t * jnp.where(lo_v, inv.pop(2 * pair), inv.pop(2 * pair + 1))).T.astype(BF16)
        c0 = 2 * h * LANES
        q_rows = slice(sub * BLOCK, (sub + 1) * BLOCK)
        o_ref[q_rows, c0:c0 + LANES] = out[:BLOCK]
        o_ref[q_rows, c0 + LANES:c0 + 2 * LANES] = out[BLOCK:]

    n_slots = n_sub * slots_per_block
    ahead = 1
    s_t = {slot: scores(slot) for slot in range(ahead)}
    for slot in range(n_slots):
        if slot + ahead < n_slots:
            s_t[slot + ahead] = scores(slot + ahead)
        softmax(slot, s_t.pop(slot))
        if slot % 2 == 0 and slot > 0:
            weighted_values(slot // 2 - 1)
        if slot in after_slot:
            after_slot[slot]()
    weighted_values(n_slots // 2 - 1)


def _attn_call(q, k, vt, kc, vct, sink, x, mod, wo, g, b, wi32, wo32, tq):
    bsz, length, _ = q.shape
    n_blocks = length // BLOCK
    per_batch_tiles = length // tq
    n_steps = bsz * per_batch_tiles
    n_later = wi32.shape[0] - 1
    wi_cols, wo_rows = 2 * D_FF // WEIGHT_SLABS, D_FF // WEIGHT_SLABS
    assert n_later * WEIGHT_SLABS <= n_steps + 1
    slab = lambda i: jnp.minimum(i, n_later * WEIGHT_SLABS - 1)
    cur = lambda i: jnp.minimum(i, n_steps - 1)
    prev = lambda i: jnp.maximum(i - 1, 0)
    tile = lambda pick: pl.BlockSpec(
        (None, tq, D_MODEL), lambda i, *_: (pick(i) // per_batch_tiles, pick(i) % per_batch_tiles, 0))
    per_batch = lambda arr: pl.BlockSpec(
        (None,) + arr.shape[1:], lambda i, *_: (cur(i) // per_batch_tiles,) + (0,) * (arr.ndim - 1))
    return pl.pallas_call(
        functools.partial(_attn_kernel, n_blocks, n_steps),
        out_shape=(jax.ShapeDtypeStruct(x.shape, F32),
                   jax.ShapeDtypeStruct((n_later,) + wi32.shape[1:], BF16),
                   jax.ShapeDtypeStruct((n_later,) + wo32.shape[1:], BF16)),
        grid_spec=pltpu.PrefetchScalarGridSpec(
            num_scalar_prefetch=1,
            grid=(n_steps + 1,),
            in_specs=[
                tile(cur),
                per_batch(k), per_batch(vt), per_batch(kc), per_batch(vct),
                tile(prev),
                pl.BlockSpec((None, N_MOD, D_MODEL), lambda i, *_: (prev(i) // per_batch_tiles, 0, 0)),
                _resident((D_MODEL, D_MODEL)),
                _resident((1, D_MODEL)),
                _resident((1, D_MODEL)),
                pl.BlockSpec((None, D_MODEL, wi_cols),
                             lambda i, *_: (1 + slab(i) // WEIGHT_SLABS, 0, slab(i) % WEIGHT_SLABS)),
                pl.BlockSpec((None, wo_rows, D_MODEL),
                             lambda i, *_: (1 + slab(i) // WEIGHT_SLABS, slab(i) % WEIGHT_SLABS, 0)),
            ],
            out_specs=(tile(prev),
                       pl.BlockSpec((None, D_MODEL, wi_cols),
                                    lambda i, *_: (slab(i) // WEIGHT_SLABS, 0, slab(i) % WEIGHT_SLABS)),
                       pl.BlockSpec((None, wo_rows, D_MODEL),
                                    lambda i, *_: (slab(i) // WEIGHT_SLABS, slab(i) % WEIGHT_SLABS, 0))),
            scratch_shapes=[pltpu.VMEM((tq, Q_WIDTH), BF16)],
        ),
        compiler_params=pltpu.CompilerParams(dimension_semantics=("arbitrary",), vmem_limit_bytes=VMEM_LIMIT),
        name="attn",
    )(sink, q, k, vt, kc, vct, x, mod, wo, g.reshape(1, D_MODEL), b.reshape(1, D_MODEL), wi32, wo32)


def _reverse_shift(t, first_row, flip_ref):
    rows = t.shape[0]
    blocks = [jnp.dot(flip_ref[...], t[r0:r0 + FLIP, :], preferred_element_type=F32)
              for r0 in range(rows - FLIP, -FLIP, -FLIP)]
    shifted = pltpu.roll(jnp.concatenate(blocks, axis=0), 1, 0)
    is_first = lax.broadcasted_iota(jnp.int32, shifted.shape, 0) == 0
    return jnp.where(is_first, first_row, shifted)


def _channel_dft(t, w_ref):
    return jnp.concatenate(
        [jnp.dot(t[:, g0:g0 + FOURIER_GROUP_CH], w_ref[...], preferred_element_type=F32)
         for g0 in range(0, D_MODEL, FOURIER_GROUP_CH)], axis=1)


def _dft_fold_kernel(lo_ref, up_ref, mod_ref, cc_ref, sc_ref, flip_ref, ab_ref, an_ref, carry_ref):
    @pl.when(pl.program_id(1) == 0)
    def _():
        carry_ref[...] = jnp.zeros_like(carry_ref)

    h_lo = _modulate(lo_ref[...], mod_ref, 1)
    h_up = _modulate(up_ref[...], mod_ref, 1).astype(BF16)
    partner = _reverse_shift(h_up, carry_ref[0:1, :], flip_ref)
    carry_ref[...] = h_up[0:8, :].astype(F32)
    ab_ref[0] = _channel_dft((h_lo + partner).astype(BF16), cc_ref).astype(BF16)
    ab_ref[1] = _channel_dft((h_lo - partner).astype(BF16), sc_ref).astype(BF16)
    an_ref[...] = _channel_dft(h_up[0:8, :], cc_ref)


def _dft_fold_call(x4, mod, cc, sc, flip, tm):
    bsz, _, half, _ = x4.shape
    n_t = half // tm
    ch = FOURIER_GROUP_CH
    return pl.pallas_call(
        _dft_fold_kernel,
        out_shape=(jax.ShapeDtypeStruct((bsz, 2, half, D_MODEL), BF16),
                   jax.ShapeDtypeStruct((bsz, 8, D_MODEL), F32)),
        grid=(bsz, n_t),
        in_specs=[
            pl.BlockSpec((None, None, tm, D_MODEL), lambda bi, t: (bi, 0, t, 0)),
            pl.BlockSpec((None, None, tm, D_MODEL), lambda bi, t: (bi, 1, n_t - 1 - t, 0)),
            pl.BlockSpec((None, N_MOD, D_MODEL), lambda bi, t: (bi, 0, 0)),
            _resident((ch, ch)),
            _resident((ch, ch)),
            _resident((FLIP, FLIP)),
        ],
        out_specs=(pl.BlockSpec((None, 2, tm, D_MODEL), lambda bi, t: (bi, 0, t, 0)),
                   pl.BlockSpec((None, 8, D_MODEL), lambda bi, t: (bi, 0, 0))),
        scratch_shapes=[pltpu.VMEM((8, D_MODEL), F32)],
        compiler_params=pltpu.CompilerParams(dimension_semantics=("parallel", "arbitrary"),
                                             vmem_limit_bytes=VMEM_LIMIT),
        name="dft_fold",
    )(x4, x4, mod, cc, sc, flip)


def _dft_seq_kernel(norm, n_t, wlo_ref, wup_ref, ab_ref, an_ref, alt_ref, flip_ref, x_ref, mod_ref, wo_ref,
                    g_ref, beta_ref, out_ref, carry_ref):
    t = pl.program_id(1)
    tk = wlo_ref.shape[0]
    ab = ab_ref[...]
    a_n = an_ref[0:1, :]
    row = lax.broadcasted_iota(jnp.int32, (tk, 1), 0)

    def alternating(k0):
        return (1 - 2 * ((k0 + row) & 1)).astype(F32)

    f_lo = (jnp.dot(wlo_ref[...], ab, preferred_element_type=F32) + alternating(t * tk) * a_n) * norm
    f_mir = (jnp.dot(wup_ref[...], ab, preferred_element_type=F32) + alternating((n_t - 1 - t) * tk) * a_n) * norm
    f_mir = f_mir.astype(BF16)

    @pl.when(t == 0)
    def _():
        f_n = (jnp.dot(alt_ref[...], ab, preferred_element_type=F32)[0:1, :] + a_n) * norm
        carry_ref[0:1, :] = f_n.astype(BF16).astype(F32)

    f_up = _reverse_shift(f_mir, carry_ref[0:1, :], flip_ref)
    carry_ref[...] = f_mir[0:8, :].astype(F32)
    y_lo = jnp.dot(f_lo.astype(BF16), wo_ref[...], preferred_element_type=F32)
    y_up = jnp.dot(f_up.astype(BF16), wo_ref[...], preferred_element_type=F32)
    out_ref[0] = _post_norm(x_ref[0], y_lo, mod_ref, 1, g_ref, beta_ref)
    out_ref[1] = _post_norm(x_ref[1], y_up, mod_ref, 1, g_ref, beta_ref)


def _dft_seq_call(w_lo, w_up, ab, a_n, alt, flip, x4, mod, wo, g, beta, tk):
    bsz, _, half, _ = x4.shape
    n_t = half // tk
    norm = float(1.0 / np.sqrt(2 * half * FOURIER_GROUP_CH))
    tok = pl.BlockSpec((None, 2, tk, D_MODEL), lambda bi, t: (bi, 0, t, 0))
    return pl.pallas_call(
        functools.partial(_dft_seq_kernel, norm, n_t),
        out_shape=jax.ShapeDtypeStruct(x4.shape, F32),
        grid=(bsz, n_t),
        in_specs=[
            pl.BlockSpec((tk, 2 * half), lambda bi, t: (t, 0)),
            pl.BlockSpec((tk, 2 * half), lambda bi, t: (n_t - 1 - t, 0)),
            pl.BlockSpec((None, 2 * half, D_MODEL), lambda bi, t: (bi, 0, 0)),
            pl.BlockSpec((None, 8, D_MODEL), lambda bi, t: (bi, 0, 0)),
            _resident((8, 2 * half)),
            _resident((FLIP, FLIP)),
            tok,
            pl.BlockSpec((None, N_MOD, D_MODEL), lambda bi, t: (bi, 0, 0)),
            _resident((D_MODEL, D_MODEL)),
            _resident((1, D_MODEL)),
            _resident((1, D_MODEL)),
        ],
        out_specs=tok,
        scratch_shapes=[pltpu.VMEM((8, D_MODEL), F32)],
        compiler_params=pltpu.CompilerParams(dimension_semantics=("parallel", "arbitrary"),
                                             vmem_limit_bytes=VMEM_LIMIT),
        name="dft_seq",
    )(w_lo, w_up, ab, a_n, alt, flip, x4, mod, wo, g.reshape(1, D_MODEL), beta.reshape(1, D_MODEL))


def _rope_tables(n_tokens):
    rows = n_tokens // GRID_W
    row = jnp.repeat(jnp.arange(rows), GRID_W).astype(F32)
    col = jnp.tile(jnp.arange(GRID_W), rows).astype(F32)
    inv = ROPE_BASE ** (-jnp.arange(0, AXIS_DIM, 2, dtype=F32) / AXIS_DIM)
    ang_r, ang_c = row[:, None] * inv, col[:, None] * inv
    cos_r, sin_r, cos_c, sin_c = jnp.cos(ang_r), jnp.sin(ang_r), jnp.cos(ang_c), jnp.sin(ang_c)
    zero = jnp.zeros_like(sin_r)
    cos_h = jnp.concatenate([cos_r, cos_r, cos_c, cos_c], axis=-1)
    sa_h = jnp.concatenate([-sin_r, zero, -sin_c, zero], axis=-1)
    sb_h = jnp.concatenate([zero, sin_r, zero, sin_c], axis=-1)
    rep = LANES // HEAD_DIM
    return tuple(jnp.tile(t, (1, rep)) for t in (cos_h, sa_h, sb_h))


def _dft_tables(n, rows):
    step = TWIDDLE_STEP if rows % TWIDDLE_STEP == 0 else rows
    col = jnp.arange(rows, dtype=jnp.int32)[None, :]

    def thin(k):
        ang = ((k[:, None] * col) % n).astype(F32) * (2.0 * np.pi / n)
        return jnp.cos(ang), jnp.sin(ang)

    cos1, sin1 = (t[:, None, :] for t in thin(jnp.arange(0, rows, step, dtype=jnp.int32)))
    cos0, sin0 = (t[None, :, :] for t in thin(jnp.arange(step, dtype=jnp.int32)))
    return ((cos1 * cos0 - sin1 * sin0).reshape(rows, rows),
            (sin1 * cos0 + cos1 * sin0).reshape(rows, rows))


def kernel(x, c, ctx, c_ctx, mod_w, mod_b, ln_g, ln_b, ffn_wi, ffn_wo,
           attn_wqkv, attn_bqkv, attn_wo, attn_sink, fourier_wo):
    bsz, seq, _ = x.shape
    n_ctx = ctx.shape[1]
    assert DEPTH == 2 and seq % BLOCK == 0 and seq % GRID_W == 0

    cv = jnp.concatenate([c, c_ctx[None, :], jnp.zeros((MOD_ROWS - bsz - 1, D_MODEL), F32)], axis=0)
    mod = _mod_call(cv, mod_w, mod_b).reshape(DEPTH, MOD_ROWS, N_MOD, D_MODEL)
    wi32 = ffn_wi.reshape(2 * DEPTH, D_MODEL, 2 * D_FF)
    wo32 = ffn_wo.reshape(2 * DEPTH, D_FF, D_MODEL)
    wi0, wo0 = wi32[:1].astype(BF16), wo32[:1].astype(BF16)

    mod_lat, mod_ctx = mod[0, :bsz], mod[0, bsz:bsz + 1]
    g, b = ln_g[0], ln_b[0]
    x = _ffn_call(x, mod_lat, 0, wi0, wo0, 0, g[0], b[0], tm=FFN_TILE)
    ctx_s = _ffn_call(ctx.reshape(1, bsz * n_ctx, D_MODEL), mod_ctx, 0, wi0, wo0, 0, g[0], b[0], tm=FFN_TILE)

    w_qkv = attn_wqkv[0].astype(BF16)
    cos_t, sa_t, sb_t = _rope_tables(seq)
    q, k, vt = _qkv_call(x, mod_lat, w_qkv, attn_bqkv[0], cos_t, sa_t, sb_t, tm=1024)
    kc, vct = _kv_ctx_call(ctx_s.reshape(bsz, n_ctx, D_MODEL), mod_ctx, w_qkv[:, Q_WIDTH:], attn_bqkv[0, Q_WIDTH:])
    x, wi, wo = _attn_call(q, k, vt, kc, vct, attn_sink[0], x, mod_lat, attn_wo[0].astype(BF16), g[1], b[1],
                           wi32, wo32, tq=4 * BLOCK)
    x = _ffn_call(x, mod_lat, 2, wi, wo, 0, g[2], b[2], tm=FFN_TILE)

    mod_lat = mod[1, :bsz]
    g, b = ln_g[1], ln_b[1]
    x = _ffn_call(x, mod_lat, 0, wi, wo, 1, g[0], b[0], tm=FFN_TILE)
    half = seq // 2
    cos_c, sin_c = _dft_tables(FOURIER_GROUP_CH, FOURIER_GROUP_CH)
    cos_s, sin_s = _dft_tables(seq, half)
    w_lo = jnp.concatenate([cos_s, -sin_s], axis=1).astype(BF16)
    w_up = jnp.concatenate([cos_s, sin_s], axis=1).astype(BF16)
    alt = jnp.zeros((8, seq), F32).at[0, :half].set(1.0 - 2.0 * (jnp.arange(half) % 2)).astype(BF16)
    anti = jnp.arange(FLIP)[:, None] + jnp.arange(FLIP)[None, :] == FLIP - 1
    flip = anti.astype(BF16)
    x4 = x.reshape(bsz, 2, half, D_MODEL)
    ab, a_n = _dft_fold_call(x4, mod_lat, cos_c.astype(BF16), sin_c.astype(BF16), flip, tm=1024)
    x4 = _dft_seq_call(w_lo, w_up, ab.reshape(bsz, seq, D_MODEL), a_n, alt, flip, x4, mod_lat,
                       fourier_wo[0].astype(BF16), g[1], b[1], tk=512)
    x = x4.reshape(bsz, seq, D_MODEL)
    x = _ffn_call(x, mod_lat, 2, wi, wo, 2, g[2], b[2], tm=FFN_TILE)
    return x
```

```python
import functools

import jax
import jax.numpy as jnp
import numpy as np
from jax import lax
from jax.experimental import pallas as pl
from jax.experimental.pallas import tpu as pltpu

D_MODEL = 1024
DEPTH = 2
GRID_W = 64
N_HEADS = 16
N_KV_HEADS = 4
HEAD_DIM = 64
GROUP = N_HEADS // N_KV_HEADS
Q_WIDTH = N_HEADS * HEAD_DIM
KV_WIDTH = N_KV_HEADS * HEAD_DIM
QKV_WIDTH = Q_WIDTH + 2 * KV_WIDTH
KV_DUP_WIDTH = 2 * KV_WIDTH
WINDOW = 128
BLOCK = 128
SPAN = BLOCK + 2 * WINDOW
ROPE_BASE = 10000.0
AXIS_DIM = HEAD_DIM // 2
FOURIER_GROUPS = 4
FOURIER_GROUP_CH = D_MODEL // FOURIER_GROUPS
D_FF = 2816
N_MOD = 9
LN_EPS = 1e-5
ALPHA = (2.0 * DEPTH) ** 0.25
NEG_INF = -1e30
LOG2E = float(np.log2(np.e))

LANES = 128
MOD_ROWS = 24
VMEM_LIMIT = 60 * 1024 * 1024
FFN_TILE = 1024
FFN_CHUNKS = (256,) * 11
FLIP = 256
TWIDDLE_STEP = 32
WEIGHT_SLABS = 11

BF16 = jnp.bfloat16
F32 = jnp.float32


def _params(n_axes):
    return pltpu.CompilerParams(dimension_semantics=("parallel",) * n_axes,
                                vmem_limit_bytes=VMEM_LIMIT)


def _resident(shape):
    return pl.BlockSpec(shape, lambda *_: (0,) * len(shape), pipeline_mode=pl.Buffered(1))


def _layer_norm(z, g, b):
    mu = jnp.mean(z, axis=-1, keepdims=True)
    d = z - mu
    var = jnp.mean(d * d, axis=-1, keepdims=True)
    return d * lax.rsqrt(var + LN_EPS) * g + b


def _modulate(x, mod_ref, j):
    shift = mod_ref[3 * j:3 * j + 1, :]
    scale = mod_ref[3 * j + 1:3 * j + 2, :]
    return x * (1.0 + scale) + shift


def _post_norm(x, y, mod_ref, j, g_ref, b_ref):
    gate = mod_ref[3 * j + 2:3 * j + 3, :]
    return _layer_norm(ALPHA * x + gate * y, g_ref[...], b_ref[...])


def _mod_kernel(cv_ref, w_ref, b_ref, o_ref):
    cv = cv_ref[...]
    s = (cv * jax.nn.sigmoid(cv)).astype(BF16)
    o_ref[...] = jnp.dot(s, w_ref[...].astype(BF16), preferred_element_type=F32) + b_ref[...]


def _mod_call(cv, mod_w, mod_b):
    tn = 3 * D_MODEL
    n_out = N_MOD * D_MODEL
    return pl.pallas_call(
        _mod_kernel,
        out_shape=jax.ShapeDtypeStruct((DEPTH, MOD_ROWS, n_out), F32),
        grid=(DEPTH, n_out // tn),
        in_specs=[
            pl.BlockSpec((MOD_ROWS, D_MODEL), lambda i, n: (0, 0)),
            pl.BlockSpec((None, D_MODEL, tn), lambda i, n: (i, 0, n)),
            pl.BlockSpec((None, 1, tn), lambda i, n: (i, 0, n)),
        ],
        out_specs=pl.BlockSpec((None, MOD_ROWS, tn), lambda i, n: (i, 0, n)),
        compiler_params=_params(2),
        name="mod",
    )(cv, mod_w, mod_b.reshape(DEPTH, 1, n_out))


def _ffn_kernel(j, n_tiles, x_ref, mod_ref, wi_ref, wo_ref, g_ref, b_ref, o_ref, u_ref, z_ref):
    i = pl.program_id(0)

    @pl.when(i == 0)
    def _():
        z_ref[...] = jnp.zeros_like(z_ref)

    def norm_previous():
        o_ref[...] = _layer_norm(z_ref[...], g_ref[...], b_ref[...])

    @pl.when(i < n_tiles)
    def _():
        norm_previous()
        h = _modulate(x_ref[...], mod_ref, j).astype(BF16)
        start = 0
        for width in FFN_CHUNKS:
            g = jnp.dot(h, wi_ref[:, D_FF + start:D_FF + start + width], preferred_element_type=F32)
            gate = g * jax.nn.sigmoid(g)
            a = jnp.dot(h, wi_ref[:, start:start + width], preferred_element_type=F32)
            u_ref[:, start:start + width] = (a * gate).astype(BF16)
            start += width
        y = 0.5 * jnp.dot(u_ref[...], wo_ref[...], preferred_element_type=F32)
        z_ref[...] = ALPHA * x_ref[...] + mod_ref[3 * j + 2:3 * j + 3, :] * y

    @pl.when(i == n_tiles)
    def _():
        norm_previous()


def _ffn_call(x, mod, j, wi_sets, wo_sets, which, g, b, tm):
    bsz, length, _ = x.shape
    per_batch = length // tm
    n_tiles = bsz * per_batch
    cur = lambda i: jnp.minimum(i, n_tiles - 1)
    prev = lambda i: jnp.maximum(i - 1, 0)
    tile = lambda pick: pl.BlockSpec((None, tm, D_MODEL), lambda i: (pick(i) // per_batch, pick(i) % per_batch, 0))
    weight = lambda rows, cols: pl.BlockSpec((None, rows, cols), lambda i: (which, 0, 0),
                                             pipeline_mode=pl.Buffered(1))
    return pl.pallas_call(
        functools.partial(_ffn_kernel, j, n_tiles),
        out_shape=jax.ShapeDtypeStruct(x.shape, F32),
        grid=(n_tiles + 1,),
        in_specs=[
            tile(cur),
            pl.BlockSpec((None, N_MOD, D_MODEL), lambda i: (cur(i) // per_batch, 0, 0)),
            weight(D_MODEL, 2 * D_FF),
            weight(D_FF, D_MODEL),
            _resident((1, D_MODEL)),
            _resident((1, D_MODEL)),
        ],
        out_specs=tile(prev),
        scratch_shapes=[pltpu.VMEM((tm, D_FF), BF16), pltpu.VMEM((tm, D_MODEL), F32)],
        compiler_params=pltpu.CompilerParams(dimension_semantics=("arbitrary",), vmem_limit_bytes=VMEM_LIMIT),
        name="ffn",
    )(x, mod, wi_sets, wo_sets, g.reshape(1, D_MODEL), b.reshape(1, D_MODEL))


def _rope(t, cos_ref, sa_ref, sb_ref):
    half = AXIS_DIM // 2
    outs = []
    for c in range(t.shape[1] // LANES):
        tc = t[:, c * LANES:(c + 1) * LANES]
        up = pltpu.roll(tc, LANES - half, 1)
        down = pltpu.roll(tc, half, 1)
        outs.append(tc * cos_ref[...] + up * sa_ref[...] + down * sb_ref[...])
    return jnp.concatenate(outs, axis=1)


def _dup_heads(t):
    pieces = []
    for h in range(t.shape[1] // HEAD_DIM):
        head = t[:, h * HEAD_DIM:(h + 1) * HEAD_DIM]
        pieces += [head, head]
    return jnp.concatenate(pieces, axis=1)


def _qkv_kernel(x_ref, mod_ref, w_ref, b_ref, cos_ref, sa_ref, sb_ref, q_ref, k_ref, vt_ref):
    h = _modulate(x_ref[...], mod_ref, 1).astype(BF16)
    qkv = jnp.dot(h, w_ref[...], preferred_element_type=F32) + b_ref[...]
    q = qkv[:, :Q_WIDTH] * (HEAD_DIM ** -0.5 * LOG2E)
    q_ref[...] = _rope(q, cos_ref, sa_ref, sb_ref).astype(BF16)
    k = _rope(qkv[:, Q_WIDTH:Q_WIDTH + KV_WIDTH], cos_ref, sa_ref, sb_ref)
    k_ref[...] = _dup_heads(k).astype(BF16)
    v = _dup_heads(qkv[:, Q_WIDTH + KV_WIDTH:])
    for blk in range(vt_ref.shape[0]):
        vt_ref[blk] = v[blk * BLOCK:(blk + 1) * BLOCK, :].T.astype(BF16)


def _qkv_call(x, mod, w, b, cos_t, sa_t, sb_t, tm):
    bsz, length, _ = x.shape
    tok = lambda width: pl.BlockSpec((None, tm, width), lambda bi, t: (bi, t, 0))
    table = pl.BlockSpec((tm, LANES), lambda bi, t: (t, 0))
    return pl.pallas_call(
        _qkv_kernel,
        out_shape=(jax.ShapeDtypeStruct((bsz, length, Q_WIDTH), BF16),
                   jax.ShapeDtypeStruct((bsz, length, KV_DUP_WIDTH), BF16),
                   jax.ShapeDtypeStruct((bsz, length // BLOCK, KV_DUP_WIDTH, BLOCK), BF16)),
        grid=(bsz, length // tm),
        in_specs=[
            tok(D_MODEL),
            pl.BlockSpec((None, N_MOD, D_MODEL), lambda bi, t: (bi, 0, 0)),
            _resident((D_MODEL, QKV_WIDTH)),
            _resident((1, QKV_WIDTH)),
            table, table, table,
        ],
        out_specs=(tok(Q_WIDTH), tok(KV_DUP_WIDTH),
                   pl.BlockSpec((None, tm // BLOCK, KV_DUP_WIDTH, BLOCK), lambda bi, t: (bi, t, 0, 0))),
        compiler_params=_params(2),
        name="qkv",
    )(x, mod, w, b.reshape(1, QKV_WIDTH), cos_t, sa_t, sb_t)


def _kv_ctx_kernel(x_ref, mod_ref, w_ref, b_ref, k_ref, vt_ref):
    h = _modulate(x_ref[...], mod_ref, 1).astype(BF16)
    kv = jnp.dot(h, w_ref[...], preferred_element_type=F32) + b_ref[...]
    k_ref[...] = _dup_heads(kv[:, :KV_WIDTH]).astype(BF16)
    vt_ref[...] = _dup_heads(kv[:, KV_WIDTH:]).T.astype(BF16)


def _kv_ctx_call(x, mod, w, b):
    bsz, n_ctx, _ = x.shape
    return pl.pallas_call(
        _kv_ctx_kernel,
        out_shape=(jax.ShapeDtypeStruct((bsz, n_ctx, KV_DUP_WIDTH), BF16),
                   jax.ShapeDtypeStruct((bsz, KV_DUP_WIDTH, n_ctx), BF16)),
        grid=(bsz,),
        in_specs=[
            pl.BlockSpec((None, n_ctx, D_MODEL), lambda bi: (bi, 0, 0)),
            _resident((None, N_MOD, D_MODEL)),
            _resident((D_MODEL, 2 * KV_WIDTH)),
            _resident((1, 2 * KV_WIDTH)),
        ],
        out_specs=(pl.BlockSpec((None, n_ctx, KV_DUP_WIDTH), lambda bi: (bi, 0, 0)),
                   pl.BlockSpec((None, KV_DUP_WIDTH, n_ctx), lambda bi: (bi, 0, 0))),
        compiler_params=_params(1),
        name="kv_ctx",
    )(x, mod, w, b.reshape(1, 2 * KV_WIDTH))


def _attn_kernel(n_blocks, n_steps, sink_ref, q_ref, k_ref, vt_ref, kc_ref, vct_ref, xp_ref, modp_ref, wo_ref,
                 g_ref, b_ref, wi32_ref, wo32_ref, out_ref, wi16_ref, wo16_ref, o_ref):
    i = pl.program_id(0)

    @pl.when(i == 0)
    def _():
        o_ref[...] = jnp.zeros_like(o_ref)

    def project(o_prev):
        return jnp.dot(o_prev, wo_ref[...], preferred_element_type=F32)

    def norm(y):
        out_ref[...] = _post_norm(xp_ref[...], y, modp_ref, 1, g_ref, b_ref)

    def round_weights():
        wi16_ref[...] = wi32_ref[...].astype(BF16)
        wo16_ref[...] = wo32_ref[...].astype(BF16)

    @pl.when(i < n_steps)
    def _():
        round_weights()
        o_prev = o_ref[...]
        y = []
        tile_in_batch = i % (n_blocks * BLOCK // q_ref.shape[0])
        _attend(n_blocks, tile_in_batch, sink_ref, q_ref, k_ref, vt_ref, kc_ref, vct_ref, o_ref,
                after_slot={9: lambda: y.append(project(o_prev)), 21: lambda: norm(y[0])})

    @pl.when(i == n_steps)
    def _():
        round_weights()
        norm(project(o_ref[...]))


def _attend(n_blocks, tile_in_batch, sink_ref, q_ref, k_ref, vt_ref, kc_ref, vct_ref, o_ref, after_slot):
    n_sub = q_ref.shape[0] // BLOCK
    qb0 = tile_in_batch * n_sub
    cols2 = 2 * BLOCK
    key_i = lax.broadcasted_iota(jnp.int32, (BLOCK, cols2), 0)
    qry_i = lax.broadcasted_iota(jnp.int32, (BLOCK, cols2), 1) & (BLOCK - 1)
    first_head = lax.broadcasted_iota(jnp.int32, (1, cols2), 1) < BLOCK
    lo_q = lax.broadcasted_iota(jnp.int32, (cols2, LANES), 1) < HEAD_DIM
    lo_v = lax.broadcasted_iota(jnp.int32, (LANES, 1), 0) < HEAD_DIM

    def block_ids(sub):
        return [jnp.clip(qb0 + sub + d, 0, n_blocks - 1) for d in (-1, 0, 1)]

    slots_per_block = 2 * N_KV_HEADS
    p_t, inv = {}, {}

    def scores(slot):
        sub, rest = divmod(slot, slots_per_block)
        h, half = divmod(rest, 2)
        c0 = 2 * h * LANES
        q_rows = slice(sub * BLOCK, (sub + 1) * BLOCK)
        q_cat = jnp.concatenate([q_ref[q_rows, c0:c0 + LANES], q_ref[q_rows, c0 + LANES:c0 + 2 * LANES]], axis=0)
        keep = lo_q if half == 0 else jnp.logical_not(lo_q)
        q_sel = jnp.where(keep, q_cat, jnp.zeros_like(q_cat))
        lanes = slice(h * LANES, (h + 1) * LANES)
        k_h = jnp.concatenate([k_ref[pl.ds(pl.multiple_of(blk * BLOCK, BLOCK), BLOCK), lanes]
                               for blk in block_ids(sub)] + [kc_ref[:, lanes]], axis=0)
        return lax.dot_general(k_h, q_sel, (((1,), (1,)), ((), ())), preferred_element_type=F32)

    def softmax(slot, s_t):
        sub, rest = divmod(slot, slots_per_block)
        h, half = divmod(rest, 2)
        left_ok = (qry_i <= key_i) & (qb0 + sub > 0)
        right_ok = (key_i <= qry_i) & (qb0 + sub < n_blocks - 1)
        sink = jnp.where(first_head, sink_ref[4 * h + half], sink_ref[4 * h + 2 + half]) * LOG2E
        parts = [jnp.where(left_ok, s_t[:BLOCK], NEG_INF),
                 s_t[BLOCK:2 * BLOCK],
                 jnp.where(right_ok, s_t[2 * BLOCK:SPAN], NEG_INF)]
        parts += [s_t[c:c + BLOCK] for c in range(SPAN, s_t.shape[0], BLOCK)]
        m = jnp.max(functools.reduce(jnp.maximum, parts), axis=0, keepdims=True)
        m = jnp.maximum(m, sink)
        p = [jnp.exp2(t - m) for t in parts]
        denom = jnp.sum(functools.reduce(jnp.add, p), axis=0, keepdims=True) + jnp.exp2(sink - m)
        inv[slot] = 1.0 / denom
        p_t[slot] = jnp.concatenate(p, axis=0).astype(BF16)

    def weighted_values(pair):
        sub, h = divmod(pair, N_KV_HEADS)
        rows = slice(h * LANES, (h + 1) * LANES)
        v_t = jnp.concatenate([vt_ref[blk, rows, :] for blk in block_ids(sub)] + [vct_ref[rows, :]], axis=1)
        v_lo = jnp.where(lo_v, v_t, jnp.zeros_like(v_t))
        v_hi = jnp.where(lo_v, jnp.zeros_like(v_t), v_t)
        acc_t = (jnp.dot(v_lo, p_t.pop(2 * pair), preferred_element_type=F32)
                 + jnp.dot(v_hi, p_t.pop(2 * pair + 1), preferred_element_type=F32))
        out = (acc_t * jnp.where(lo_v, inv.pop(2 * pair), inv.pop(2 * pair + 1))).T.astype(BF16)
        c0 = 2 * h * LANES
        q_rows = slice(sub * BLOCK, (sub + 1) * BLOCK)
        o_ref[q_rows, c0:c0 + LANES] = out[:BLOCK]
        o_ref[q_rows, c0 + LANES:c0 + 2 * LANES] = out[BLOCK:]

    n_slots = n_sub * slots_per_block
    ahead = 2
    s_t = {slot: scores(slot) for slot in range(ahead)}
    for slot in range(n_slots):
        if slot + ahead < n_slots:
            s_t[slot + ahead] = scores(slot + ahead)
        softmax(slot, s_t.pop(slot))
        if slot % 2 == 0 and slot > 0:
            weighted_values(slot // 2 - 1)
        if slot in after_slot:
            after_slot[slot]()
    weighted_values(n_slots // 2 - 1)


def _attn_call(q, k, vt, kc, vct, sink, x, mod, wo, g, b, wi32, wo32, tq):
    bsz, length, _ = q.shape
    n_blocks = length // BLOCK
    per_batch_tiles = length // tq
    n_steps = bsz * per_batch_tiles
    n_later = wi32.shape[0] - 1
    wi_cols, wo_rows = 2 * D_FF // WEIGHT_SLABS, D_FF // WEIGHT_SLABS
    assert n_later * WEIGHT_SLABS <= n_steps + 1
    slab = lambda i: jnp.minimum(i, n_later * WEIGHT_SLABS - 1)
    cur = lambda i: jnp.minimum(i, n_steps - 1)
    prev = lambda i: jnp.maximum(i - 1, 0)
    tile = lambda pick: pl.BlockSpec(
        (None, tq, D_MODEL), lambda i, *_: (pick(i) // per_batch_tiles, pick(i) % per_batch_tiles, 0))
    per_batch = lambda arr: pl.BlockSpec(
        (None,) + arr.shape[1:], lambda i, *_: (cur(i) // per_batch_tiles,) + (0,) * (arr.ndim - 1))
    return pl.pallas_call(
        functools.partial(_attn_kernel, n_blocks, n_steps),
        out_shape=(jax.ShapeDtypeStruct(x.shape, F32),
                   jax.ShapeDtypeStruct((n_later,) + wi32.shape[1:], BF16),
                   jax.ShapeDtypeStruct((n_later,) + wo32.shape[1:], BF16)),
        grid_spec=pltpu.PrefetchScalarGridSpec(
            num_scalar_prefetch=1,
            grid=(n_steps + 1,),
            in_specs=[
                tile(cur),
                per_batch(k), per_batch(vt), per_batch(kc), per_batch(vct),
                tile(prev),
                pl.BlockSpec((None, N_MOD, D_MODEL), lambda i, *_: (prev(i) // per_batch_tiles, 0, 0)),
                _resident((D_MODEL, D_MODEL)),
                _resident((1, D_MODEL)),
                _resident((1, D_MODEL)),
                pl.BlockSpec((None, D_MODEL, wi_cols),
                             lambda i, *_: (1 + slab(i) // WEIGHT_SLABS, 0, slab(i) % WEIGHT_SLABS)),
                pl.BlockSpec((None, wo_rows, D_MODEL),
                             lambda i, *_: (1 + slab(i) // WEIGHT_SLABS, slab(i) % WEIGHT_SLABS, 0)),
            ],
            out_specs=(tile(prev),
                       pl.BlockSpec((None, D_MODEL, wi_cols),
                                    lambda i, *_: (slab(i) // WEIGHT_SLABS, 0, slab(i) % WEIGHT_SLABS)),
                       pl.BlockSpec((None, wo_rows, D_MODEL),
                                    lambda i, *_: (slab(i) // WEIGHT_SLABS, slab(i) % WEIGHT_SLABS, 0))),
            scratch_shapes=[pltpu.VMEM((tq, Q_WIDTH), BF16)],
        ),
        compiler_params=pltpu.CompilerParams(dimension_semantics=("arbitrary",), vmem_limit_bytes=VMEM_LIMIT),
        name="attn",
    )(sink, q, k, vt, kc, vct, x, mod, wo, g.reshape(1, D_MODEL), b.reshape(1, D_MODEL), wi32, wo32)


def _reverse_shift(t, first_row, flip_ref):
    rows = t.shape[0]
    blocks = [jnp.dot(flip_ref[...], t[r0:r0 + FLIP, :], preferred_element_type=F32)
              for r0 in range(rows - FLIP, -FLIP, -FLIP)]
    shifted = pltpu.roll(jnp.concatenate(blocks, axis=0), 1, 0)
    is_first = lax.broadcasted_iota(jnp.int32, shifted.shape, 0) == 0
    return jnp.where(is_first, first_row, shifted)


def _channel_dft(t, w_ref):
    return jnp.concatenate(
        [jnp.dot(t[:, g0:g0 + FOURIER_GROUP_CH], w_ref[...], preferred_element_type=F32)
         for g0 in range(0, D_MODEL, FOURIER_GROUP_CH)], axis=1)


def _dft_fold_kernel(lo_ref, up_ref, mod_ref, cc_ref, sc_ref, flip_ref, ab_ref, an_ref, carry_ref):
    @pl.when(pl.program_id(1) == 0)
    def _():
        carry_ref[...] = jnp.zeros_like(carry_ref)

    h_lo = _modulate(lo_ref[...], mod_ref, 1)
    h_up = _modulate(up_ref[...], mod_ref, 1).astype(BF16)
    partner = _reverse_shift(h_up, carry_ref[0:1, :], flip_ref)
    carry_ref[...] = h_up[0:8, :].astype(F32)
    ab_ref[0] = _channel_dft((h_lo + partner).astype(BF16), cc_ref).astype(BF16)
    ab_ref[1] = _channel_dft((h_lo - partner).astype(BF16), sc_ref).astype(BF16)
    an_ref[...] = _channel_dft(h_up[0:8, :], cc_ref)


def _dft_fold_call(x4, mod, cc, sc, flip, tm):
    bsz, _, half, _ = x4.shape
    n_t = half // tm
    ch = FOURIER_GROUP_CH
    return pl.pallas_call(
        _dft_fold_kernel,
        out_shape=(jax.ShapeDtypeStruct((bsz, 2, half, D_MODEL), BF16),
                   jax.ShapeDtypeStruct((bsz, 8, D_MODEL), F32)),
        grid=(bsz, n_t),
        in_specs=[
            pl.BlockSpec((None, None, tm, D_MODEL), lambda bi, t: (bi, 0, t, 0)),
            pl.BlockSpec((None, None, tm, D_MODEL), lambda bi, t: (bi, 1, n_t - 1 - t, 0)),
            pl.BlockSpec((None, N_MOD, D_MODEL), lambda bi, t: (bi, 0, 0)),
            _resident((ch, ch)),
            _resident((ch, ch)),
            _resident((FLIP, FLIP)),
        ],
        out_specs=(pl.BlockSpec((None, 2, tm, D_MODEL), lambda bi, t: (bi, 0, t, 0)),
                   pl.BlockSpec((None, 8, D_MODEL), lambda bi, t: (bi, 0, 0))),
        scratch_shapes=[pltpu.VMEM((8, D_MODEL), F32)],
        compiler_params=pltpu.CompilerParams(dimension_semantics=("parallel", "arbitrary"),
                                             vmem_limit_bytes=VMEM_LIMIT),
        name="dft_fold",
    )(x4, x4, mod, cc, sc, flip)


def _dft_seq_kernel(norm, n_t, wlo_ref, wup_ref, ab_ref, an_ref, alt_ref, flip_ref, x_ref, mod_ref, wo_ref,
                    g_ref, beta_ref, out_ref, carry_ref):
    t = pl.program_id(1)
    tk = wlo_ref.shape[0]
    ab = ab_ref[...]
    a_n = an_ref[0:1, :]
    row = lax.broadcasted_iota(jnp.int32, (tk, 1), 0)

    def alternating(k0):
        return (1 - 2 * ((k0 + row) & 1)).astype(F32)

    f_lo = (jnp.dot(wlo_ref[...], ab, preferred_element_type=F32) + alternating(t * tk) * a_n) * norm
    f_mir = (jnp.dot(wup_ref[...], ab, preferred_element_type=F32) + alternating((n_t - 1 - t) * tk) * a_n) * norm
    f_mir = f_mir.astype(BF16)

    @pl.when(t == 0)
    def _():
        f_n = (jnp.dot(alt_ref[...], ab, preferred_element_type=F32)[0:1, :] + a_n) * norm
        carry_ref[0:1, :] = f_n.astype(BF16).astype(F32)

    f_up = _reverse_shift(f_mir, carry_ref[0:1, :], flip_ref)
    carry_ref[...] = f_mir[0:8, :].astype(F32)
    y_lo = jnp.dot(f_lo.astype(BF16), wo_ref[...], preferred_element_type=F32)
    y_up = jnp.dot(f_up.astype(BF16), wo_ref[...], preferred_element_type=F32)
    out_ref[0] = _post_norm(x_ref[0], y_lo, mod_ref, 1, g_ref, beta_ref)
    out_ref[1] = _post_norm(x_ref[1], y_up, mod_ref, 1, g_ref, beta_ref)


def _dft_seq_call(w_lo, w_up, ab, a_n, alt, flip, x4, mod, wo, g, beta, tk):
    bsz, _, half, _ = x4.shape
    n_t = half // tk
    norm = float(1.0 / np.sqrt(2 * half * FOURIER_GROUP_CH))
    tok = pl.BlockSpec((None, 2, tk, D_MODEL), lambda bi, t: (bi, 0, t, 0))
    return pl.pallas_call(
        functools.partial(_dft_seq_kernel, norm, n_t),
        out_shape=jax.ShapeDtypeStruct(x4.shape, F32),
        grid=(bsz, n_t),
        in_specs=[
            pl.BlockSpec((tk, 2 * half), lambda bi, t: (t, 0)),
            pl.BlockSpec((tk, 2 * half), lambda bi, t: (n_t - 1 - t, 0)),
            pl.BlockSpec((None, 2 * half, D_MODEL), lambda bi, t: (bi, 0, 0)),
            pl.BlockSpec((None, 8, D_MODEL), lambda bi, t: (bi, 0, 0)),
            _resident((8, 2 * half)),
            _resident((FLIP, FLIP)),
            tok,
            pl.BlockSpec((None, N_MOD, D_MODEL), lambda bi, t: (bi, 0, 0)),
            _resident((D_MODEL, D_MODEL)),
            _resident((1, D_MODEL)),
            _resident((1, D_MODEL)),
        ],
        out_specs=tok,
        scratch_shapes=[pltpu.VMEM((8, D_MODEL), F32)],
        compiler_params=pltpu.CompilerParams(dimension_semantics=("parallel", "arbitrary"),
                                             vmem_limit_bytes=VMEM_LIMIT),
        name="dft_seq",
    )(w_lo, w_up, ab, a_n, alt, flip, x4, mod, wo, g.reshape(1, D_MODEL), beta.reshape(1, D_MODEL))


def _rope_tables(n_tokens):
    rows = n_tokens // GRID_W
    row = jnp.repeat(jnp.arange(rows), GRID_W).astype(F32)
    col = jnp.tile(jnp.arange(GRID_W), rows).astype(F32)
    inv = ROPE_BASE ** (-jnp.arange(0, AXIS_DIM, 2, dtype=F32) / AXIS_DIM)
    ang_r, ang_c = row[:, None] * inv, col[:, None] * inv
    cos_r, sin_r, cos_c, sin_c = jnp.cos(ang_r), jnp.sin(ang_r), jnp.cos(ang_c), jnp.sin(ang_c)
    zero = jnp.zeros_like(sin_r)
    cos_h = jnp.concatenate([cos_r, cos_r, cos_c, cos_c], axis=-1)
    sa_h = jnp.concatenate([-sin_r, zero, -sin_c, zero], axis=-1)
    sb_h = jnp.concatenate([zero, sin_r, zero, sin_c], axis=-1)
    rep = LANES // HEAD_DIM
    return tuple(jnp.tile(t, (1, rep)) for t in (cos_h, sa_h, sb_h))


def _dft_tables(n, rows):
    step = TWIDDLE_STEP if rows % TWIDDLE_STEP == 0 else rows
    col = jnp.arange(rows, dtype=jnp.int32)[None, :]

    def thin(k):
        ang = ((k[:, None] * col) % n).astype(F32) * (2.0 * np.pi / n)
        return jnp.cos(ang), jnp.sin(ang)

    cos1, sin1 = (t[:, None, :] for t in thin(jnp.arange(0, rows, step, dtype=jnp.int32)))
    cos0, sin0 = (t[None, :, :] for t in thin(jnp.arange(step, dtype=jnp.int32)))
    return ((cos1 * cos0 - sin1 * sin0).reshape(rows, rows),
            (sin1 * cos0 + cos1 * sin0).reshape(rows, rows))


def kernel(x, c, ctx, c_ctx, mod_w, mod_b, ln_g, ln_b, ffn_wi, ffn_wo,
           attn_wqkv, attn_bqkv, attn_wo, attn_sink, fourier_wo):
    bsz, seq, _ = x.shape
    n_ctx = ctx.shape[1]
    assert DEPTH == 2 and seq % BLOCK == 0 and seq % GRID_W == 0

    cv = jnp.concatenate([c, c_ctx[None, :], jnp.zeros((MOD_ROWS - bsz - 1, D_MODEL), F32)], axis=0)
    mod = _mod_call(cv, mod_w, mod_b).reshape(DEPTH, MOD_ROWS, N_MOD, D_MODEL)
    wi32 = ffn_wi.reshape(2 * DEPTH, D_MODEL, 2 * D_FF)
    wo32 = ffn_wo.reshape(2 * DEPTH, D_FF, D_MODEL)
    wi0, wo0 = wi32[:1].astype(BF16), wo32[:1].astype(BF16)

    mod_lat, mod_ctx = mod[0, :bsz], mod[0, bsz:bsz + 1]
    g, b = ln_g[0], ln_b[0]
    x = _ffn_call(x, mod_lat, 0, wi0, wo0, 0, g[0], b[0], tm=FFN_TILE)
    ctx_s = _ffn_call(ctx.reshape(1, bsz * n_ctx, D_MODEL), mod_ctx, 0, wi0, wo0, 0, g[0], b[0], tm=FFN_TILE)

    w_qkv = attn_wqkv[0].astype(BF16)
    cos_t, sa_t, sb_t = _rope_tables(seq)
    q, k, vt = _qkv_call(x, mod_lat, w_qkv, attn_bqkv[0], cos_t, sa_t, sb_t, tm=1024)
    kc, vct = _kv_ctx_call(ctx_s.reshape(bsz, n_ctx, D_MODEL), mod_ctx, w_qkv[:, Q_WIDTH:], attn_bqkv[0, Q_WIDTH:])
    x, wi, wo = _attn_call(q, k, vt, kc, vct, attn_sink[0], x, mod_lat, attn_wo[0].astype(BF16), g[1], b[1],
                           wi32, wo32, tq=4 * BLOCK)
    x = _ffn_call(x, mod_lat, 2, wi, wo, 0, g[2], b[2], tm=FFN_TILE)

    mod_lat = mod[1, :bsz]
    g, b = ln_g[1], ln_b[1]
    x = _ffn_call(x, mod_lat, 0, wi, wo, 1, g[0], b[0], tm=FFN_TILE)
    half = seq // 2
    cos_c, sin_c = _dft_tables(FOURIER_GROUP_CH, FOURIER_GROUP_CH)
    cos_s, sin_s = _dft_tables(seq, half)
    w_lo = jnp.concatenate([cos_s, -sin_s], axis=1).astype(BF16)
    w_up = jnp.concatenate([cos_s, sin_s], axis=1).astype(BF16)
    alt = jnp.zeros((8, seq), F32).at[0, :half].set(1.0 - 2.0 * (jnp.arange(half) % 2)).astype(BF16)
    anti = jnp.arange(FLIP)[:, None] + jnp.arange(FLIP)[None, :] == FLIP - 1
    flip = anti.astype(BF16)
    x4 = x.reshape(bsz, 2, half, D_MODEL)
    ab, a_n = _dft_fold_call(x4, mod_lat, cos_c.astype(BF16), sin_c.astype(BF16), flip, tm=1024)
    x4 = _dft_seq_call(w_lo, w_up, ab.reshape(bsz, seq, D_MODEL), a_n, alt, flip, x4, mod_lat,
                       fourier_wo[0].astype(BF16), g[1], b[1], tk=512)
    x = x4.reshape(bsz, seq, D_MODEL)
    x = _ffn_call(x, mod_lat, 2, wi, wo, 2, g[2], b[2], tm=FFN_TILE)
    return x
```
